```python
import math
import jax, jax.numpy as jnp
from jax import lax
import numpy as np


D_MODEL = 1024
BATCH = 4
SEQ = 4096
DEPTH = 1

PLE_DIM = 256
GDN_HEADS = 4
GDN_DK = 128
GDN_DV = 128
CONV_WIDTH = 4
CHUNK = 64
DIFF_HEADS = 4
DIFF_D = 64
DIFF_DV = 2 * DIFF_D
Q_BLOCK = 128
ROPE_THETA = 500000.0
ROPE_DIM = DIFF_D // 4
N_EXPERTS = 32
TOP_K = 4
D_FF = D_MODEL
SWIGLU_LIMIT = 7.0
SWIGLU_ALPHA = 1.702
MOE_BLOCK = 128
DN_ALPHA = (2 * DEPTH) ** 0.25
DN_BETA = (8 * DEPTH) ** -0.25
LN_EPS = 1e-5
RMS_EPS = 1e-6

GDN_QK_W = GDN_HEADS * GDN_DK
GDN_V_W = GDN_HEADS * GDN_DV
DIFF_QK_W = DIFF_HEADS * 2 * DIFF_D
DIFF_V_W = DIFF_HEADS * DIFF_DV
MIX_W = GDN_V_W + DIFF_V_W
IN_SIZES = (GDN_QK_W, GDN_QK_W, GDN_V_W, GDN_V_W, GDN_HEADS, GDN_HEADS, DIFF_QK_W, DIFF_QK_W, DIFF_V_W)
IN_OFFSETS = tuple(int(o) for o in np.cumsum(IN_SIZES)[:-1])
IN_W = int(sum(IN_SIZES))
CONV_CH = 2 * GDN_QK_W + GDN_V_W

kernel_name = 'hybrid_gdn_diffattn_moe_deepnorm'

F32 = jnp.float32


def layer_norm(x, g, b):
    xf = x.astype(F32)
    mu = jnp.mean(xf, -1, keepdims=True)
    var = jnp.mean(jnp.square(xf - mu), -1, keepdims=True)
    return ((xf - mu) * lax.rsqrt(var + LN_EPS) * g.astype(F32) + b.astype(F32)).astype(x.dtype)


def rms_norm(x, g):
    xf = x.astype(F32)
    return (xf * lax.rsqrt(jnp.mean(xf * xf, -1, keepdims=True) + RMS_EPS) * g.astype(F32)).astype(x.dtype)


def l2_normalize(x):
    xf = x.astype(F32)
    return xf * lax.rsqrt(jnp.sum(xf * xf, -1, keepdims=True) + 1e-6)


def causal_depthwise_conv(x, w):
    c = x.shape[-1]
    return lax.conv_general_dilated(x, w[:, None, :].astype(x.dtype), window_strides=(1,),
                                    padding=[(CONV_WIDTH - 1, 0)],
                                    dimension_numbers=('NWC', 'WIO', 'NWC'),
                                    feature_group_count=c)


def partial_rotary(x, cos, sin):
    xr, xp = x[..., :ROPE_DIM], x[..., ROPE_DIM:]
    x1, x2 = xr[..., :ROPE_DIM // 2], xr[..., ROPE_DIM // 2:]
    rot = jnp.concatenate([-x2, x1], -1)
    return jnp.concatenate([xr * cos + rot * sin, xp], -1)


def gated_delta_rule_chunked(q, k, v, g, beta):
    b_, s_, h_, dk = q.shape
    dv = v.shape[-1]
    n = s_ // CHUNK

    def chunks(t):
        t = jnp.moveaxis(t, 2, 1)
        return t.reshape((b_, h_, n, CHUNK) + t.shape[3:])

    q = chunks(l2_normalize(q) * (dk ** -0.5))
    k = chunks(l2_normalize(k))
    v = chunks(v.astype(F32))
    beta = chunks(beta.astype(F32))
    g = jnp.cumsum(chunks(g.astype(F32)), axis=-1)
    idx = jnp.arange(CHUNK)
    incl = idx[:, None] >= idx[None, :]
    strict = idx[:, None] > idx[None, :]
    decay = jnp.exp(jnp.where(incl, g[..., :, None] - g[..., None, :], -jnp.inf))
    k_beta = k * beta[..., None]
    a_mat = jnp.where(strict, jnp.einsum('bhncd,bhnjd->bhncj', k_beta, k) * decay, 0.0)
    lhs = a_mat + jnp.eye(CHUNK, dtype=F32)
    rhs = jnp.concatenate([v * beta[..., None], k_beta * jnp.exp(g)[..., None]], -1)
    sol = lax.linalg.triangular_solve(lhs, rhs, left_side=True, lower=True, unit_diagonal=True)
    u, w = sol[..., :dv], sol[..., dv:]
    qk = jnp.where(incl, jnp.einsum('bhncd,bhnjd->bhncj', q, k) * decay, 0.0)
    q_dec = q * jnp.exp(g)[..., None]
    k_dec = k * jnp.exp(g[..., -1:] - g)[..., None]
    g_last = jnp.exp(g[..., -1])

    def step(state, xs):
        u_c, w_c, qk_c, qd_c, kd_c, gl_c = xs
        v_new = u_c - jnp.einsum('bhck,bhkv->bhcv', w_c, state)
        o = jnp.einsum('bhck,bhkv->bhcv', qd_c, state) + jnp.einsum('bhcj,bhjv->bhcv', qk_c, v_new)
        state = state * gl_c[..., None, None] + jnp.einsum('bhck,bhcv->bhkv', kd_c, v_new)
        return state, o

    xs = tuple(jnp.moveaxis(t, 2, 0) for t in (u, w, qk, q_dec, k_dec, g_last))
    state0 = jnp.zeros((b_, h_, dk, dv), F32)
    _, o = lax.scan(step, state0, xs)
    o = jnp.moveaxis(o, 0, 2).reshape(b_, h_, s_, dv)
    return jnp.moveaxis(o, 1, 2)


def diff_attention(q, k, v, lam):
    b_, s_, h_, _, d = q.shape
    dv = v.shape[-1]
    nb = s_ // Q_BLOCK
    kh = jnp.transpose(k, (0, 2, 3, 1, 4))
    vh = jnp.transpose(v, (0, 2, 1, 3))
    qb = jnp.transpose(q, (0, 2, 3, 1, 4)).reshape(b_, h_, 2, nb, Q_BLOCK, d)
    qb = jnp.moveaxis(qb, 3, 0)
    kpos = jnp.arange(s_)
    scale = d ** -0.5

    def block(args):
        q_blk, i = args
        sc = jnp.einsum('bhcqd,bhckd->bhcqk', q_blk, kh, preferred_element_type=F32) * scale
        qpos = i * Q_BLOCK + jnp.arange(Q_BLOCK)
        sc = jnp.where(kpos[None, :] <= qpos[:, None], sc, -jnp.inf)
        pr = jax.nn.softmax(sc, axis=-1)
        a = pr[:, :, 0] - lam * pr[:, :, 1]
        return jnp.einsum('bhqk,bhkv->bhqv', a.astype(vh.dtype), vh)

    o = lax.map(block, (qb, jnp.arange(nb)))
    o = jnp.moveaxis(o, 0, 2).reshape(b_, h_, s_, dv)
    return jnp.transpose(o, (0, 2, 1, 3))


def hybrid_mixer(h, cos, sin, w_in, conv_w, a_log, dt_bias, gdn_norm_w,
                 lam_q1, lam_k1, lam_q2, lam_k2, diff_norm_w, w_out, lam_init):
    b_, s_, _ = h.shape
    proj = h @ w_in
    gq, gk, gv, gz, gb, ga, dq, dk, dv = jnp.split(proj, IN_OFFSETS, axis=-1)
    qkv = jax.nn.silu(causal_depthwise_conv(jnp.concatenate([gq, gk, gv], -1), conv_w))
    gq, gk, gv = jnp.split(qkv, [GDN_QK_W, 2 * GDN_QK_W], axis=-1)
    beta = jax.nn.sigmoid(gb.astype(F32))
    log_decay = -jnp.exp(a_log.astype(F32)) * jax.nn.softplus(ga.astype(F32) + dt_bias.astype(F32))
    o_gdn = gated_delta_rule_chunked(gq.reshape(b_, s_, GDN_HEADS, GDN_DK),
                                     gk.reshape(b_, s_, GDN_HEADS, GDN_DK),
                                     gv.reshape(b_, s_, GDN_HEADS, GDN_DV), log_decay, beta)
    o_gdn = rms_norm(o_gdn, gdn_norm_w) * jax.nn.silu(gz.reshape(b_, s_, GDN_HEADS, GDN_DV).astype(F32))
    o_gdn = o_gdn.reshape(b_, s_, GDN_V_W).astype(h.dtype)
    dq = partial_rotary(dq.reshape(b_, s_, 2 * DIFF_HEADS, DIFF_D), cos, sin).reshape(b_, s_, DIFF_HEADS, 2, DIFF_D)
    dk = partial_rotary(dk.reshape(b_, s_, 2 * DIFF_HEADS, DIFF_D), cos, sin).reshape(b_, s_, DIFF_HEADS, 2, DIFF_D)
    lam = (jnp.exp(jnp.sum(lam_q1.astype(F32) * lam_k1.astype(F32)))
           - jnp.exp(jnp.sum(lam_q2.astype(F32) * lam_k2.astype(F32))) + lam_init)
    o_diff = diff_attention(dq, dk, dv.reshape(b_, s_, DIFF_HEADS, DIFF_DV), lam)
    o_diff = (rms_norm(o_diff, diff_norm_w) * (1.0 - lam_init)).reshape(b_, s_, DIFF_V_W)
    return jnp.concatenate([o_gdn, o_diff.astype(h.dtype)], -1) @ w_out


def moe_ffn(x2d, router_w, router_b, w_gu, b_gu, w_down, b_down):
    n_tok = x2d.shape[0]
    nk = n_tok * TOP_K
    m_pad = nk + N_EXPERTS * MOE_BLOCK
    n_blocks = m_pad // MOE_BLOCK
    logits = (x2d @ router_w + router_b).astype(F32)
    top_vals, top_idx = lax.top_k(logits, TOP_K)
    gates = jax.nn.softmax(top_vals, axis=-1)
    flat_e = top_idx.reshape(-1)
    flat_tok = jnp.repeat(jnp.arange(n_tok, dtype=jnp.int32), TOP_K)
    flat_gate = gates.reshape(-1)
    order = jnp.argsort(flat_e)
    sorted_e = flat_e[order]
    counts = jnp.bincount(flat_e, length=N_EXPERTS)
    padded = ((counts + MOE_BLOCK - 1) // MOE_BLOCK) * MOE_BLOCK
    pad_end = jnp.cumsum(padded)
    pad_start = pad_end - padded
    grp_start = jnp.cumsum(counts) - counts
    rank = jnp.arange(nk) - grp_start[sorted_e]
    dest = pad_start[sorted_e] + rank
    slot_tok = jnp.zeros((m_pad,), jnp.int32).at[dest].set(flat_tok[order])
    slot_gate = jnp.zeros((m_pad,), F32).at[dest].set(flat_gate[order])
    block_start = jnp.arange(n_blocks) * MOE_BLOCK
    block_expert = jnp.clip(jnp.searchsorted(pad_end, block_start, side='right'), 0, N_EXPERTS - 1)
    xs = x2d[slot_tok].reshape(n_blocks, MOE_BLOCK, x2d.shape[-1])

    def expert_block(args):
        xb, e = args
        hgu = xb @ w_gu[e] + b_gu[e]
        gate = jnp.minimum(hgu[:, :D_FF], SWIGLU_LIMIT)
        up = jnp.clip(hgu[:, D_FF:], -SWIGLU_LIMIT, SWIGLU_LIMIT)
        act = (up + 1.0) * gate * jax.nn.sigmoid(SWIGLU_ALPHA * gate)
        return act @ w_down[e] + b_down[e]

    ys = lax.map(expert_block, (xs, block_expert)).reshape(m_pad, -1)
    contrib = (ys.astype(F32) * slot_gate[:, None]).astype(x2d.dtype)
    return jnp.zeros_like(x2d).at[slot_tok].add(contrib)


def setup_inputs(seed: int = 0) -> dict:
    key = jax.random.key(seed)
    ks = jax.random.split(key, 32)
    L = DEPTH
    nrm = lambda k, shape, s: jax.random.normal(k, shape, F32) * s
    x = jax.random.normal(ks[0], (BATCH, SEQ, D_MODEL), F32)
    p = jax.random.normal(ks[1], (DEPTH, BATCH, SEQ, PLE_DIM), F32)
    positions = (jax.random.randint(ks[2], (BATCH, 1), 0, 1024) + jnp.arange(SEQ)[None, :]).astype(jnp.int32)
    col_scale = np.ones((IN_W,), np.float32)
    col_scale[IN_OFFSETS[1]:IN_OFFSETS[2]] = DN_BETA
    col_scale[IN_OFFSETS[7]:] = DN_BETA
    w_in = nrm(ks[3], (L, D_MODEL, IN_W), D_MODEL ** -0.5) * jnp.asarray(col_scale)
    conv_w = nrm(ks[4], (L, CONV_WIDTH, CONV_CH), CONV_WIDTH ** -0.5)
    a_log = jnp.log(jax.random.uniform(ks[5], (L, GDN_HEADS), F32, 1.0, 16.0))
    dt = jnp.exp(jax.random.uniform(ks[6], (L, GDN_HEADS), F32, math.log(1e-3), math.log(1e-1)))
    dt_bias = dt + jnp.log(-jnp.expm1(-dt))
    gdn_norm_w = 1.0 + nrm(ks[7], (L, GDN_DV), 0.02)
    lam_q1 = nrm(ks[8], (L, DIFF_D), 0.1)
    lam_k1 = nrm(ks[9], (L, DIFF_D), 0.1)
    lam_q2 = nrm(ks[10], (L, DIFF_D), 0.1)
    lam_k2 = nrm(ks[11], (L, DIFF_D), 0.1)
    diff_norm_w = 1.0 + nrm(ks[12], (L, DIFF_DV), 0.02)
    w_out = nrm(ks[13], (L, MIX_W, D_MODEL), MIX_W ** -0.5 * DN_BETA)
    ln1_g = 1.0 + nrm(ks[14], (L, D_MODEL), 0.02)
    ln1_b = nrm(ks[15], (L, D_MODEL), 0.02)
    router_w = nrm(ks[16], (L, D_MODEL, N_EXPERTS), D_MODEL ** -0.5)
    router_b = nrm(ks[17], (L, N_EXPERTS), 0.01)
    w_gu = nrm(ks[18], (L, N_EXPERTS, D_MODEL, 2 * D_FF), D_MODEL ** -0.5)
    b_gu = nrm(ks[19], (L, N_EXPERTS, 2 * D_FF), 0.01)
    w_down = nrm(ks[20], (L, N_EXPERTS, D_FF, D_MODEL), D_FF ** -0.5 * DN_BETA)
    b_down = nrm(ks[21], (L, N_EXPERTS, D_MODEL), 0.01)
    ln2_g = 1.0 + nrm(ks[22], (L, D_MODEL), 0.02)
    ln2_b = nrm(ks[23], (L, D_MODEL), 0.02)
    ple_w = nrm(ks[24], (L, PLE_DIM, D_MODEL), PLE_DIM ** -0.5 * DN_BETA)
    ple_gate_w = nrm(ks[25], (L, D_MODEL, D_MODEL), D_MODEL ** -0.5)
    ple_gate_b = nrm(ks[26], (L, D_MODEL), 0.01)
    ln3_g = 1.0 + nrm(ks[27], (L, D_MODEL), 0.02)
    ln3_b = nrm(ks[28], (L, D_MODEL), 0.02)
    return {'x': x, 'p': p, 'positions': positions, 'w_in': w_in, 'conv_w': conv_w,
            'a_log': a_log, 'dt_bias': dt_bias, 'gdn_norm_w': gdn_norm_w,
            'lam_q1': lam_q1, 'lam_k1': lam_k1, 'lam_q2': lam_q2, 'lam_k2': lam_k2,
            'diff_norm_w': diff_norm_w, 'w_out': w_out, 'ln1_g': ln1_g, 'ln1_b': ln1_b,
            'router_w': router_w, 'router_b': router_b, 'w_gu': w_gu, 'b_gu': b_gu,
            'w_down': w_down, 'b_down': b_down, 'ln2_g': ln2_g, 'ln2_b': ln2_b,
            'ple_w': ple_w, 'ple_gate_w': ple_gate_w, 'ple_gate_b': ple_gate_b,
            'ln3_g': ln3_g, 'ln3_b': ln3_b}


def reference(x, p, positions, w_in, conv_w, a_log, dt_bias, gdn_norm_w,
              lam_q1, lam_k1, lam_q2, lam_k2, diff_norm_w, w_out, ln1_g, ln1_b,
              router_w, router_b, w_gu, b_gu, w_down, b_down, ln2_g, ln2_b,
              ple_w, ple_gate_w, ple_gate_b, ln3_g, ln3_b):
    b_, s_, d_ = x.shape
    inv_freq = ROPE_THETA ** (-jnp.arange(0, ROPE_DIM, 2, dtype=F32) / ROPE_DIM)
    ang = positions.astype(F32)[..., None] * inv_freq
    ang = jnp.concatenate([ang, ang], -1)[:, :, None, :]
    cos = jnp.cos(ang).astype(x.dtype)
    sin = jnp.sin(ang).astype(x.dtype)
    h = x
    for i in range(DEPTH):
        lam_init = 0.8 - 0.6 * math.exp(-0.3 * i)
        mix = hybrid_mixer(h, cos, sin, w_in[i], conv_w[i], a_log[i], dt_bias[i], gdn_norm_w[i],
                           lam_q1[i], lam_k1[i], lam_q2[i], lam_k2[i], diff_norm_w[i], w_out[i], lam_init)
        h = layer_norm(DN_ALPHA * h + mix, ln1_g[i], ln1_b[i])
        ffn = moe_ffn(h.reshape(b_ * s_, d_), router_w[i], router_b[i], w_gu[i], b_gu[i],
                      w_down[i], b_down[i]).reshape(b_, s_, d_)
        h = layer_norm(DN_ALPHA * h + ffn, ln2_g[i], ln2_b[i])
        gate = jax.nn.sigmoid(h @ ple_gate_w[i] + ple_gate_b[i])
        ple = gate * (p[i].astype(h.dtype) @ ple_w[i])
        h = layer_norm(DN_ALPHA * h + ple, ln3_g[i], ln3_b[i])
    return h
```

```python
import functools
import math

import jax
import jax.numpy as jnp
import numpy as np
from jax import lax
from jax.experimental import pallas as pl
from jax.experimental.pallas import tpu as pltpu

F32 = jnp.float32
BF16 = jnp.bfloat16
HI = lax.Precision.HIGHEST

D_MODEL = 1024
PLE_DIM = 256
GDN_HEADS = 4
GDN_DK = 128
GDN_DV = 128
CONV_WIDTH = 4
CHUNK = 64
DIFF_HEADS = 4
DIFF_D = 64
DIFF_DV = 2 * DIFF_D
ROPE_THETA = 500000.0
ROPE_DIM = DIFF_D // 4
N_EXPERTS = 32
TOP_K = 4
D_FF = D_MODEL
SWIGLU_LIMIT = 7.0
SWIGLU_ALPHA = 1.702
DEPTH = 1
DN_ALPHA = (2 * DEPTH) ** 0.25
LN_EPS = 1e-5
RMS_EPS = 1e-6

LANES = 128
GDN_W = GDN_HEADS * GDN_DK
CONV_CH = 3 * GDN_W
A_COLS = 4 * GDN_W
DIFF_W = DIFF_HEADS * DIFF_DV
IN_PAD_W = A_COLS + 3 * DIFF_W + LANES

VMEM_LIMIT = 56 * 1024 * 1024

TM_PROJ = 512
T_GDN = 256
TQ = 256
TKV = 256
TM_MOE = 256
TC_COMB = 256


def _layer_norm(y, g, b):
    mu = jnp.mean(y, -1, keepdims=True)
    d = y - mu
    var = jnp.mean(d * d, -1, keepdims=True)
    return d * lax.rsqrt(var + LN_EPS) * g + b


def _sigmoid(x):
    return 1.0 / (1.0 + jnp.exp(-x))


def _inproj_kernel(x_ref, w_ref, c_ref, sa_ref, sb_ref, a_ref, ba_ref, q_ref, k_ref, v_ref):
    xb = x_ref[...].astype(BF16)
    a_ref[...] = jnp.dot(xb, w_ref[:, :A_COLS], preferred_element_type=F32)
    ba_ref[...] = jnp.dot(xb, w_ref[:, A_COLS + 3 * DIFF_W:], preferred_element_type=F32)
    c = c_ref[...]
    sa = sa_ref[...]
    sb = sb_ref[...]

    def rot(t):
        return t * c + pltpu.roll(t, 8, 1) * sa + pltpu.roll(t, LANES - 8, 1) * sb

    for h in range(DIFF_HEADS):
        lo = A_COLS + LANES * h
        q = jnp.dot(xb, w_ref[:, lo:lo + LANES], preferred_element_type=F32)
        q_ref[:, LANES * h:LANES * (h + 1)] = (rot(q) * (DIFF_D ** -0.5)).astype(BF16)
        lo = A_COLS + DIFF_W + LANES * h
        k = jnp.dot(xb, w_ref[:, lo:lo + LANES], preferred_element_type=F32)
        k_ref[:, LANES * h:LANES * (h + 1)] = rot(k).astype(BF16)
    lo = A_COLS + 2 * DIFF_W
    v_ref[...] = jnp.dot(xb, w_ref[:, lo:lo + DIFF_W], preferred_element_type=F32).astype(BF16)


def _inproj(x2d, w_r, ctab, satab, sbtab):
    n = x2d.shape[0]
    tm = TM_PROJ
    row = lambda w: pl.BlockSpec((tm, w), lambda i: (i, 0))
    return pl.pallas_call(
        _inproj_kernel,
        grid=(n // tm,),
        in_specs=[row(D_MODEL),
                  pl.BlockSpec((D_MODEL, IN_PAD_W), lambda i: (0, 0)),
                  row(LANES), row(LANES), row(LANES)],
        out_specs=[row(A_COLS), row(LANES), row(DIFF_W), row(DIFF_W), row(DIFF_W)],
        out_shape=[jax.ShapeDtypeStruct((n, A_COLS), F32),
                   jax.ShapeDtypeStruct((n, LANES), F32),
                   jax.ShapeDtypeStruct((n, DIFF_W), BF16),
                   jax.ShapeDtypeStruct((n, DIFF_W), BF16),
                   jax.ShapeDtypeStruct((n, DIFF_W), BF16)],
        compiler_params=pltpu.CompilerParams(dimension_semantics=("arbitrary",),
                                             vmem_limit_bytes=VMEM_LIMIT),
        name="inproj",
    )(x2d, w_r, ctab, satab, sbtab)


def _gdn_kernel(alog_ref, dtb_ref, a_ref, ba_ref, cw_ref, nw_ref, o_ref,
                xe_ref, q_s, k_s, v_s, state_ref):
    t_rows = a_ref.shape[0]
    st = pl.program_id(1)

    @pl.when(st == 0)
    def _():
        xe_ref[0:8, :] = jnp.zeros((8, CONV_CH), F32)
        state_ref[...] = jnp.zeros(state_ref.shape, F32)

    xe_ref[8:8 + t_rows, :] = a_ref[:, :CONV_CH]
    for s in range(CONV_CH // LANES):
        cs = slice(LANES * s, LANES * (s + 1))
        y = jnp.zeros((t_rows, LANES), F32)
        for j in range(CONV_WIDTH):
            off = 8 - (CONV_WIDTH - 1) + j
            y = y + xe_ref[off:off + t_rows, cs] * cw_ref[j:j + 1, cs]
        y = y * _sigmoid(y)
        grp, h = divmod(s, GDN_HEADS)
        hs = slice(LANES * h, LANES * (h + 1))
        if grp == 0:
            q_s[:, hs] = y * lax.rsqrt(jnp.sum(y * y, -1, keepdims=True) + 1e-6) * (GDN_DK ** -0.5)
        elif grp == 1:
            k_s[:, hs] = y * lax.rsqrt(jnp.sum(y * y, -1, keepdims=True) + 1e-6)
        else:
            v_s[:, hs] = y
    xe_ref[0:8, :] = xe_ref[t_rows:t_rows + 8, :]

    ri = lax.broadcasted_iota(jnp.int32, (CHUNK, CHUNK), 0)
    ci = lax.broadcasted_iota(jnp.int32, (CHUNK, CHUNK), 1)
    incl = ri >= ci
    strict = ri > ci
    tri = incl.astype(F32)
    eye = (ri == ci).astype(F32)
    lane0 = (lax.broadcasted_iota(jnp.int32, (CHUNK, LANES), 1) == 0).astype(F32)
    nt = (((1,), (1,)), ((), ()))
    tn = (((0,), (0,)), ((), ()))
    dot = functools.partial(jnp.dot, precision=HI, preferred_element_type=F32)
    dotg = functools.partial(lax.dot_general, precision=HI, preferred_element_type=F32)

    for c in range(t_rows // CHUNK):
        rows = slice(CHUNK * c, CHUNK * (c + 1))
        ba = ba_ref[rows, :]
        for h in range(GDN_HEADS):
            hs = slice(LANES * h, LANES * (h + 1))
            q = q_s[rows, hs]
            k = k_s[rows, hs]
            v = v_s[rows, hs]
            beta = _sigmoid(ba[:, h:h + 1])
            ga = ba[:, GDN_HEADS + h:GDN_HEADS + h + 1] + dtb_ref[h]
            softplus = jnp.maximum(ga, 0.0) + jnp.log(1.0 + jnp.exp(-jnp.abs(ga)))
            gl = -jnp.exp(jnp.zeros((1, 1), F32) + alog_ref[h]) * softplus
            gcb = dot(tri, jnp.broadcast_to(gl, (CHUNK, LANES)))
            grow = dotg(lane0, gcb, nt)
            decay = jnp.where(incl, jnp.exp(jnp.minimum(gcb[:, :CHUNK] - grow, 0.0)), 0.0)
            kb = k * beta
            amat = jnp.where(strict, dotg(kb, k, nt) * decay, 0.0)
            xm = -amat
            tinv = eye + xm
            for _ in range(5):
                xm = dot(xm, xm)
                tinv = tinv + dot(tinv, xm)
            eg = jnp.exp(gcb)
            u = dot(tinv, v * beta)
            w = dot(tinv, kb * eg)
            qk = jnp.where(incl, dotg(q, k, nt) * decay, 0.0)
            glast = gcb[CHUNK - 1:CHUNK, :]
            k_dec = k * jnp.exp(glast - gcb)
            state = state_ref[h]
            v_new = u - dot(w, state)
            o = dot(q * eg, state) + dot(qk, v_new)
            state_ref[h] = state * jnp.exp(glast) + dotg(k_dec, v_new, tn)
            z = a_ref[rows, CONV_CH + LANES * h:CONV_CH + LANES * (h + 1)]
            o = o * lax.rsqrt(jnp.mean(o * o, -1, keepdims=True) + RMS_EPS) * nw_ref[...]
            o_ref[rows, hs] = (o * (z * _sigmoid(z))).astype(o_ref.dtype)


def _gdn(a_proj, ba_proj, conv_w, a_log, dt_bias, norm_w, batch, seq):
    t = T_GDN
    nst = seq // t
    rowblk = lambda w: pl.BlockSpec((t, w), lambda b, s, *_: (b * nst + s, 0))
    grid_spec = pltpu.PrefetchScalarGridSpec(
        num_scalar_prefetch=2,
        grid=(batch, nst),
        in_specs=[rowblk(A_COLS), rowblk(LANES),
                  pl.BlockSpec((CONV_WIDTH, CONV_CH), lambda b, s, *_: (0, 0)),
                  pl.BlockSpec((1, GDN_DV), lambda b, s, *_: (0, 0))],
        out_specs=rowblk(GDN_W),
        scratch_shapes=[pltpu.VMEM((t + 8, CONV_CH), F32),
                        pltpu.VMEM((t, GDN_W), F32),
                        pltpu.VMEM((t, GDN_W), F32),
                        pltpu.VMEM((t, GDN_W), F32),
                        pltpu.VMEM((GDN_HEADS, GDN_DK, GDN_DV), F32)],
    )
    return pl.pallas_call(
        _gdn_kernel,
        grid_spec=grid_spec,
        out_shape=jax.ShapeDtypeStruct((batch * seq, GDN_W), BF16),
        compiler_params=pltpu.CompilerParams(dimension_semantics=("arbitrary", "arbitrary"),
                                             vmem_limit_bytes=VMEM_LIMIT),
        name="gdn",
    )(a_log, dt_bias, a_proj, ba_proj, conv_w, norm_w.reshape(1, GDN_DV))


def _attn_kernel(q_ref, k_ref, v_ref, lq1_ref, lk1_ref, lq2_ref, lk2_ref, nw_ref, o_ref, *, lam_init):
    tq = q_ref.shape[0]
    qi = pl.program_id(2)
    q = q_ref[...]
    lane = lax.broadcasted_iota(jnp.int32, (1, LANES), 1)
    zero = jnp.zeros_like(q)
    q0 = jnp.where(lane < DIFF_D, q, zero)
    q1 = jnp.where(lane >= DIFF_D, q, zero)
    row = qi * tq + lax.broadcasted_iota(jnp.int32, (tq, 1), 0)
    nt = (((1,), (1,)), ((), ()))

    def body(kj, carry):
        off = pl.multiple_of(kj * TKV, TKV)
        k = k_ref[pl.ds(off, TKV), :]
        v = v_ref[pl.ds(off, TKV), :]
        col = off + lax.broadcasted_iota(jnp.int32, (1, TKV), 1)
        msk = col <= row

        def upd(qc, m, l, acc):
            s = lax.dot_general(qc, k, nt, preferred_element_type=F32)
            s = jnp.where(msk, s, -1e30)
            mn = jnp.maximum(m, jnp.max(s, -1, keepdims=True))
            p = jnp.exp(s - mn)
            al = jnp.exp(m - mn)
            l = al * l + jnp.sum(p, -1, keepdims=True)
            acc = al * acc + jnp.dot(p.astype(BF16), v, preferred_element_type=F32)
            return mn, l, acc

        m0, l0, a0, m1, l1, a1 = carry
        m0, l0, a0 = upd(q0, m0, l0, a0)
        m1, l1, a1 = upd(q1, m1, l1, a1)
        return m0, l0, a0, m1, l1, a1

    neg = jnp.full((tq, 1), -1e30, F32)
    zl = jnp.zeros((tq, 1), F32)
    za = jnp.zeros((tq, DIFF_DV), F32)
    nkv = (qi * tq + tq + TKV - 1) // TKV
    m0, l0, a0, m1, l1, a1 = lax.fori_loop(0, nkv, body, (neg, zl, za, neg, zl, za))
    lam = (jnp.exp(jnp.sum(lq1_ref[...] * lk1_ref[...], -1, keepdims=True))
           - jnp.exp(jnp.sum(lq2_ref[...] * lk2_ref[...], -1, keepdims=True)) + lam_init)
    o = a0 * (1.0 / l0) - lam * (a1 * (1.0 / l1))
    o = o * lax.rsqrt(jnp.mean(o * o, -1, keepdims=True) + RMS_EPS) * nw_ref[...] * (1.0 - lam_init)
    o_ref[...] = o.astype(o_ref.dtype)


def _diff_attention(q, k, v, lq1, lk1, lq2, lk2, norm_w, lam_init, batch, seq):
    q3 = q.reshape(batch, seq, DIFF_W)
    k3 = k.reshape(batch, seq, DIFF_W)
    v3 = v.reshape(batch, seq, DIFF_W)
    small = lambda w: pl.BlockSpec((1, w), lambda b, h, i: (0, 0))
    out = pl.pallas_call(
        functools.partial(_attn_kernel, lam_init=lam_init),
        grid=(batch, DIFF_HEADS, seq // TQ),
        in_specs=[pl.BlockSpec((None, TQ, LANES), lambda b, h, i: (b, i, h)),
                  pl.BlockSpec((None, seq, LANES), lambda b, h, i: (b, 0, h)),
                  pl.BlockSpec((None, seq, LANES), lambda b, h, i: (b, 0, h)),
                  small(DIFF_D), small(DIFF_D), small(DIFF_D), small(DIFF_D), small(DIFF_DV)],
        out_specs=pl.BlockSpec((None, TQ, LANES), lambda b, h, i: (b, i, h)),
        out_shape=jax.ShapeDtypeStruct((batch, seq, DIFF_W), BF16),
        compiler_params=pltpu.CompilerParams(
            dimension_semantics=("arbitrary", "arbitrary", "arbitrary"),
            vmem_limit_bytes=VMEM_LIMIT),
        name="diff_attn",
    )(q3, k3, v3, lq1.reshape(1, -1), lk1.reshape(1, -1), lq2.reshape(1, -1), lk2.reshape(1, -1),
      norm_w.reshape(1, -1))
    return out.reshape(batch * seq, DIFF_W)


def _outproj_kernel(og_ref, od_ref, x_ref, wo_ref, g_ref, b_ref, rw_ref, rb_ref,
                    h_ref, idx_ref, gate_ref):
    mix = (jnp.dot(og_ref[...], wo_ref[:GDN_W, :], preferred_element_type=F32)
           + jnp.dot(od_ref[...], wo_ref[GDN_W:, :], preferred_element_type=F32))
    h = _layer_norm(DN_ALPHA * x_ref[...] + mix, g_ref[...], b_ref[...])
    h_ref[...] = h
    logits = jnp.dot(h, rw_ref[...], precision=HI, preferred_element_type=F32) + rb_ref[...]
    tm = logits.shape[0]
    lane = lax.broadcasted_iota(jnp.int32, (tm, LANES), 1)
    lane_f = lane.astype(F32)
    work = logits
    vals, idxs = [], []
    for _ in range(TOP_K):
        m = jnp.max(work, -1, keepdims=True)
        sel = jnp.min(jnp.where(work == m, lane_f, float(LANES)), -1, keepdims=True)
        vals.append(m)
        idxs.append(sel)
        work = jnp.where(lane_f == sel, -jnp.inf, work)
    exps = [jnp.exp(vv - vals[0]) for vv in vals]
    inv = 1.0 / (exps[0] + exps[1] + exps[2] + exps[3])
    idx_out = jnp.zeros((tm, LANES), F32)
    gate_out = jnp.zeros((tm, LANES), F32)
    for j in range(TOP_K):
        idx_out = jnp.where(lane == j, idxs[j], idx_out)
        gate_out = jnp.where(lane == j, exps[j] * inv, gate_out)
    idx_ref[...] = idx_out.astype(jnp.int32)
    gate_ref[...] = gate_out


def _outproj(og, od, x2d, w_out_b, ln_g, ln_b, rw_pad, rb_pad):
    n = x2d.shape[0]
    tm = TM_PROJ
    row = lambda w: pl.BlockSpec((tm, w), lambda i: (i, 0))
    full = lambda r, c: pl.BlockSpec((r, c), lambda i: (0, 0))
    return pl.pallas_call(
        _outproj_kernel,
        grid=(n // tm,),
        in_specs=[row(GDN_W), row(DIFF_W), row(D_MODEL), full(GDN_W + DIFF_W, D_MODEL),
                  full(1, D_MODEL), full(1, D_MODEL), full(D_MODEL, LANES), full(1, LANES)],
        out_specs=[row(D_MODEL), row(LANES), row(LANES)],
        out_shape=[jax.ShapeDtypeStruct((n, D_MODEL), F32),
                   jax.ShapeDtypeStruct((n, LANES), jnp.int32),
                   jax.ShapeDtypeStruct((n, LANES), F32)],
        compiler_params=pltpu.CompilerParams(dimension_semantics=("arbitrary",),
                                             vmem_limit_bytes=VMEM_LIMIT),
        name="outproj_ln1_router",
    )(og, od, x2d, w_out_b, ln_g.reshape(1, -1), ln_b.reshape(1, -1), rw_pad, rb_pad)


def _moe_kernel(be_ref, nu_ref, slot_ref, h_hbm, wgu_ref, bgu_ref, wd_ref, bd_ref, ys_ref,
                xbuf, wgu_b, wd_b, sem):
    tm = ys_ref.shape[0]
    i = pl.program_id(0)
    n_used = nu_ref[0]

    def issue(blk, buf):
        base = blk * tm

        def body(r, carry):
            tok = slot_ref[base + r]
            pltpu.make_async_copy(h_hbm.at[pl.ds(tok, 1), :], xbuf.at[buf, pl.ds(r, 1), :],
                                  sem.at[buf]).start()
            return carry

        lax.fori_loop(0, tm, body, 0)

    @pl.when(i == 0)
    def _():
        issue(0, 0)

    @pl.when(i + 1 < n_used)
    def _():
        issue(i + 1, (i + 1) % 2)

    @pl.when(i < n_used)
    def _():
        buf = i % 2
        pltpu.make_async_copy(h_hbm.at[pl.ds(0, tm), :], xbuf.at[buf], sem.at[buf]).wait()

        @pl.when((i == 0) | (be_ref[i] != be_ref[jnp.maximum(i - 1, 0)]))
        def _():
            rows = 128

            def cast(j, carry):
                r0 = pl.multiple_of(j * rows, rows)
                wgu_b[pl.ds(r0, rows), :] = wgu_ref[0, pl.ds(r0, rows), :].astype(BF16)
                wd_b[pl.ds(r0, rows), :] = wd_ref[0, pl.ds(r0, rows), :].astype(BF16)
                return carry

            lax.fori_loop(0, D_MODEL // rows, cast, 0)

        xb = xbuf[buf].astype(BF16)
        hgu = jnp.dot(xb, wgu_b[...], preferred_element_type=F32) + bgu_ref[0]
        gate = jnp.minimum(hgu[:, :D_FF], SWIGLU_LIMIT)
        up = jnp.clip(hgu[:, D_FF:], -SWIGLU_LIMIT, SWIGLU_LIMIT)
        act = (up + 1.0) * gate * _sigmoid(SWIGLU_ALPHA * gate)
        ys_ref[...] = jnp.dot(act.astype(BF16), wd_b[...], preferred_element_type=F32) + bd_ref[0]

    @pl.when(i >= n_used)
    def _():
        ys_ref[...] = jnp.zeros(ys_ref.shape, F32)


def _moe(block_expert, n_used, slot_tok, h1, w_gu, b_gu, w_down, b_down):
    tm = TM_MOE
    m_pad = slot_tok.shape[0]
    n_blocks = m_pad // tm
    grid_spec = pltpu.PrefetchScalarGridSpec(
        num_scalar_prefetch=3,
        grid=(n_blocks,),
        in_specs=[pl.BlockSpec(memory_space=pl.ANY),
                  pl.BlockSpec((1, D_MODEL, 2 * D_FF), lambda i, be, nu, sl: (be[i], 0, 0)),
                  pl.BlockSpec((1, 1, 2 * D_FF), lambda i, be, nu, sl: (be[i], 0, 0)),
                  pl.BlockSpec((1, D_FF, D_MODEL), lambda i, be, nu, sl: (be[i], 0, 0)),
                  pl.BlockSpec((1, 1, D_MODEL), lambda i, be, nu, sl: (be[i], 0, 0))],
        out_specs=pl.BlockSpec((tm, D_MODEL), lambda i, be, nu, sl: (i, 0)),
        scratch_shapes=[pltpu.VMEM((2, tm, D_MODEL), F32),
                        pltpu.VMEM((D_MODEL, 2 * D_FF), BF16),
                        pltpu.VMEM((D_FF, D_MODEL), BF16),
                        pltpu.SemaphoreType.DMA((2,))],
    )
    return pl.pallas_call(
        _moe_kernel,
        grid_spec=grid_spec,
        out_shape=jax.ShapeDtypeStruct((m_pad, D_MODEL), F32),
        compiler_params=pltpu.CompilerParams(dimension_semantics=("arbitrary",),
                                             vmem_limit_bytes=VMEM_LIMIT),
        name="moe_ffn",
    )(block_expert, n_used, slot_tok, h1, w_gu, b_gu.reshape(N_EXPERTS, 1, -1),
      w_down, b_down.reshape(N_EXPERTS, 1, -1))


def _combine_kernel(pos_ref, h_ref, gate_ref, p_ref, ys_hbm, g2_ref, b2_ref, wg_ref, bg_ref, wp_ref,
                    g3_ref, b3_ref, o_ref, ybuf, sem):
    tc = h_ref.shape[0]
    i = pl.program_id(0)
    n = pl.num_programs(0)

    def issue(blk, buf):
        base = blk * (TOP_K * tc)

        def body(r, carry):
            for j in range(TOP_K):
                s = pos_ref[base + j * tc + r]
                pltpu.make_async_copy(ys_hbm.at[pl.ds(s, 1), :], ybuf.at[buf, j, pl.ds(r, 1), :],
                                      sem.at[buf]).start()
            return carry

        lax.fori_loop(0, tc, body, 0)

    @pl.when(i == 0)
    def _():
        issue(0, 0)

    @pl.when(i + 1 < n)
    def _():
        issue(i + 1, (i + 1) % 2)

    buf = i % 2
    for j in range(TOP_K):
        pltpu.make_async_copy(ys_hbm.at[pl.ds(0, tc), :], ybuf.at[buf, j], sem.at[buf]).wait()
    gates = gate_ref[...]
    ffn = ybuf[buf, 0] * gates[:, 0:1]
    for j in range(1, TOP_K):
        ffn = ffn + ybuf[buf, j] * gates[:, j:j + 1]
    h2 = _layer_norm(DN_ALPHA * h_ref[...] + ffn, g2_ref[...], b2_ref[...])
    gate = _sigmoid(jnp.dot(h2.astype(BF16), wg_ref[...], preferred_element_type=F32) + bg_ref[...])
    ple = gate * jnp.dot(p_ref[...].astype(BF16), wp_ref[...], preferred_element_type=F32)
    o_ref[...] = _layer_norm(DN_ALPHA * h2 + ple, g3_ref[...], b3_ref[...])


def _combine(pos_blk, h1, gates, p2d, ys, g2, b2, wg_b, bg, wp_b, g3, b3):
    n = h1.shape[0]
    tc = TC_COMB
    row = lambda w: pl.BlockSpec((tc, w), lambda i, pos: (i, 0))
    full = lambda r, c: pl.BlockSpec((r, c), lambda i, pos: (0, 0))
    grid_spec = pltpu.PrefetchScalarGridSpec(
        num_scalar_prefetch=1,
        grid=(n // tc,),
        in_specs=[row(D_MODEL), row(LANES), row(PLE_DIM), pl.BlockSpec(memory_space=pl.ANY),
                  full(1, D_MODEL), full(1, D_MODEL), full(D_MODEL, D_MODEL), full(1, D_MODEL),
                  full(PLE_DIM, D_MODEL), full(1, D_MODEL), full(1, D_MODEL)],
        out_specs=row(D_MODEL),
        scratch_shapes=[pltpu.VMEM((2, TOP_K, tc, D_MODEL), F32),
                        pltpu.SemaphoreType.DMA((2,))],
    )
    return pl.pallas_call(
        _combine_kernel,
        grid_spec=grid_spec,
        out_shape=jax.ShapeDtypeStruct((n, D_MODEL), F32),
        compiler_params=pltpu.CompilerParams(dimension_semantics=("arbitrary",),
                                             vmem_limit_bytes=VMEM_LIMIT),
        name="combine_ln2_ple_ln3",
    )(pos_blk, h1, gates, p2d, ys, g2.reshape(1, -1), b2.reshape(1, -1), wg_b, bg.reshape(1, -1), wp_b,
      g3.reshape(1, -1), b3.reshape(1, -1))


def _routing(top_idx, tm, tc):
    n_tok = top_idx.shape[0]
    nk = n_tok * TOP_K
    m_pad = nk + N_EXPERTS * tm
    n_blocks = m_pad // tm
    flat_e = top_idx.reshape(-1)
    order = jnp.argsort(flat_e, stable=True)
    sorted_e = flat_e[order]
    counts = jnp.bincount(flat_e, length=N_EXPERTS)
    padded = ((counts + tm - 1) // tm) * tm
    pad_end = jnp.cumsum(padded)
    pad_start = pad_end - padded
    grp_start = jnp.cumsum(counts) - counts
    rank = jnp.arange(nk, dtype=jnp.int32) - grp_start[sorted_e]
    dest = (pad_start[sorted_e] + rank).astype(jnp.int32)
    slot_tok = jnp.zeros((m_pad,), jnp.int32).at[dest].set((order // TOP_K).astype(jnp.int32))
    pos = jnp.zeros((nk,), jnp.int32).at[order].set(dest)
    block_start = jnp.arange(n_blocks, dtype=jnp.int32) * tm
    block_expert = jnp.clip(jnp.searchsorted(pad_end, block_start, side='right'), 0,
                            N_EXPERTS - 1).astype(jnp.int32)
    n_used = (pad_end[-1] // tm).astype(jnp.int32).reshape(1)
    pos_blk = pos.reshape(n_tok // tc, tc, TOP_K).transpose(0, 2, 1).reshape(-1)
    return block_expert, n_used, slot_tok, pos_blk


def _layer(h, p_i, cos8, sin8, lam_init, w_in, conv_w, a_log, dt_bias, gdn_norm_w,
           lam_q1, lam_k1, lam_q2, lam_k2, diff_norm_w, w_out, ln1_g, ln1_b,
           router_w, router_b, w_gu, b_gu, w_down, b_down, ln2_g, ln2_b,
           ple_w, ple_gate_w, ple_gate_b, ln3_g, ln3_b):
    batch, seq, d = h.shape
    n = batch * seq
    x2d = h.reshape(n, d)
    o_ba = 4 * GDN_W
    o_d = o_ba + 2 * GDN_HEADS
    w_r = jnp.concatenate([w_in[:, :o_ba], w_in[:, o_d:], w_in[:, o_ba:o_d],
                           jnp.zeros((d, LANES - 2 * GDN_HEADS), w_in.dtype)], -1).astype(BF16)
    ones = jnp.ones((n, DIFF_D - ROPE_DIM), F32)
    zeros = jnp.zeros((n, DIFF_D - ROPE_DIM), F32)
    z8 = jnp.zeros_like(sin8)
    ctab = jnp.tile(jnp.concatenate([cos8, cos8, ones], -1), (1, 2))
    satab = jnp.tile(jnp.concatenate([z8, sin8, zeros], -1), (1, 2))
    sbtab = jnp.tile(jnp.concatenate([-sin8, z8, zeros], -1), (1, 2))
    a_proj, ba_proj, dq, dk, dv = _inproj(x2d, w_r, ctab, satab, sbtab)
    o_gdn = _gdn(a_proj, ba_proj, conv_w, a_log, dt_bias, gdn_norm_w, batch, seq)
    o_diff = _diff_attention(dq, dk, dv, lam_q1, lam_k1, lam_q2, lam_k2, diff_norm_w, lam_init, batch, seq)
    rw_pad = jnp.concatenate([router_w, jnp.zeros((d, LANES - N_EXPERTS), F32)], -1)
    rb_pad = jnp.concatenate([router_b, jnp.full((LANES - N_EXPERTS,), -jnp.inf, F32)]).reshape(1, LANES)
    h1, idx, gates = _outproj(o_gdn, o_diff, x2d, w_out.astype(BF16), ln1_g, ln1_b, rw_pad, rb_pad)
    block_expert, n_used, slot_tok, pos_blk = _routing(idx[:, :TOP_K], TM_MOE, TC_COMB)
    ys = _moe(block_expert, n_used, slot_tok, h1, w_gu, b_gu, w_down, b_down)
    out = _combine(pos_blk, h1, gates, p_i.reshape(n, PLE_DIM), ys, ln2_g, ln2_b,
                   ple_gate_w.astype(BF16), ple_gate_b, ple_w.astype(BF16), ln3_g, ln3_b)
    return out.reshape(batch, seq, d)


def kernel(x, p, positions, w_in, conv_w, a_log, dt_bias, gdn_norm_w, lam_q1, lam_k1, lam_q2, lam_k2,
           diff_norm_w, w_out, ln1_g, ln1_b, router_w, router_b, w_gu, b_gu, w_down, b_down, ln2_g, ln2_b,
           ple_w, ple_gate_w, ple_gate_b, ln3_g, ln3_b):
    batch, seq, _ = x.shape
    inv_freq = ROPE_THETA ** (-jnp.arange(0, ROPE_DIM, 2, dtype=F32) / ROPE_DIM)
    ang = (positions.astype(F32)[..., None] * inv_freq).reshape(batch * seq, ROPE_DIM // 2)
    cos8 = jnp.cos(ang)
    sin8 = jnp.sin(ang)
    h = x
    for i in range(w_in.shape[0]):
        lam_init = 0.8 - 0.6 * math.exp(-0.3 * i)
        h = _layer(h, p[i], cos8, sin8, lam_init, w_in[i], conv_w[i], a_log[i], dt_bias[i], gdn_norm_w[i],
                   lam_q1[i], lam_k1[i], lam_q2[i], lam_k2[i], diff_norm_w[i], w_out[i], ln1_g[i], ln1_b[i],
                   router_w[i], router_b[i], w_gu[i], b_gu[i], w_down[i], b_down[i], ln2_g[i], ln2_b[i],
                   ple_w[i], ple_gate_w[i], ple_gate_b[i], ln3_g[i], ln3_b[i])
    return h
```

```python
import functools
import math

import jax
import jax.numpy as jnp
from jax import lax
from jax.experimental import pallas as pl
from jax.experimental.pallas import tpu as pltpu

F32 = jnp.float32
BF16 = jnp.bfloat16
HI = lax.Precision.HIGHEST
LOG2E = 1.4426950408889634

D_MODEL = 1024
PLE_DIM = 256
GDN_HEADS = 4
GDN_DK = 128
GDN_DV = 128
CONV_WIDTH = 4
CHUNK = 64
DIFF_HEADS = 4
DIFF_D = 64
DIFF_DV = 2 * DIFF_D
ROPE_THETA = 500000.0
ROPE_DIM = DIFF_D // 4
N_EXPERTS = 32
TOP_K = 4
D_FF = D_MODEL
SWIGLU_LIMIT = 7.0
SWIGLU_ALPHA = 1.702
DEPTH = 1
DN_ALPHA = (2 * DEPTH) ** 0.25
LN_EPS = 1e-5
RMS_EPS = 1e-6

LANES = 128
GDN_W = GDN_HEADS * GDN_DK
CONV_CH = 3 * GDN_W
A_COLS = 4 * GDN_W
DIFF_W = DIFF_HEADS * DIFF_DV
IN_PAD_W = A_COLS + 3 * DIFF_W + LANES
HSTACK = GDN_HEADS * CHUNK
SUBBLK = 16

VMEM_LIMIT = 56 * 1024 * 1024

TM_PROJ = 512
T_GDN = 256
TQ = 256
TKV = 256
TM_MOE = 256
TC_DISP = 512
TC_COMB = 256


def _layer_norm(y, g, b):
    mu = jnp.mean(y, -1, keepdims=True)
    d = y - mu
    var = jnp.mean(d * d, -1, keepdims=True)
    return d * lax.rsqrt(var + LN_EPS) * g + b


def _sigmoid(x):
    return 1.0 / (1.0 + jnp.exp(-x))


def _inproj_kernel(x_ref, w_ref, c_ref, sa_ref, sb_ref, a_ref, ba_ref, q_ref, k_ref, v_ref):
    xb = x_ref[...].astype(BF16)
    a_ref[...] = jnp.dot(xb, w_ref[:, :A_COLS], preferred_element_type=F32)
    ba_ref[...] = jnp.dot(xb, w_ref[:, A_COLS + 3 * DIFF_W:], preferred_element_type=F32)
    c = c_ref[...]
    sa = sa_ref[...]
    sb = sb_ref[...]

    def rot(t):
        return t * c + pltpu.roll(t, 8, 1) * sa + pltpu.roll(t, LANES - 8, 1) * sb

    for h in range(DIFF_HEADS):
        lo = A_COLS + LANES * h
        q = jnp.dot(xb, w_ref[:, lo:lo + LANES], preferred_element_type=F32)
        q_ref[:, LANES * h:LANES * (h + 1)] = (rot(q) * (DIFF_D ** -0.5 * LOG2E)).astype(BF16)
        lo = A_COLS + DIFF_W + LANES * h
        k = jnp.dot(xb, w_ref[:, lo:lo + LANES], preferred_element_type=F32)
        k_ref[:, LANES * h:LANES * (h + 1)] = rot(k).astype(BF16)
    lo = A_COLS + 2 * DIFF_W
    v_ref[...] = jnp.dot(xb, w_ref[:, lo:lo + DIFF_W], preferred_element_type=F32).astype(BF16)


def _inproj(x2d, w_r, ctab, satab, sbtab):
    n = x2d.shape[0]
    tm = TM_PROJ
    row = lambda w: pl.BlockSpec((tm, w), lambda i: (i, 0))
    return pl.pallas_call(
        _inproj_kernel,
        grid=(n // tm,),
        in_specs=[row(D_MODEL),
                  pl.BlockSpec((D_MODEL, IN_PAD_W), lambda i: (0, 0)),
                  row(LANES), row(LANES), row(LANES)],
        out_specs=[row(A_COLS), row(LANES), row(DIFF_W), row(DIFF_W), row(DIFF_W)],
        out_shape=[jax.ShapeDtypeStruct((n, A_COLS), F32),
                   jax.ShapeDtypeStruct((n, LANES), F32),
                   jax.ShapeDtypeStruct((n, DIFF_W), BF16),
                   jax.ShapeDtypeStruct((n, DIFF_W), BF16),
                   jax.ShapeDtypeStruct((n, DIFF_W), BF16)],
        compiler_params=pltpu.CompilerParams(dimension_semantics=("arbitrary",),
                                             vmem_limit_bytes=VMEM_LIMIT),
        name="inproj",
    )(x2d, w_r, ctab, satab, sbtab)


def _split2(x):
    hi = x.astype(BF16)
    lo = (x - hi.astype(F32)).astype(BF16)
    return hi, lo


def _dot3(a, b):
    ah, al = _split2(a)
    bh, bl = _split2(b)
    lhs = jnp.concatenate([ah, ah, al], axis=1)
    rhs = jnp.concatenate([bh, bl, bh], axis=0)
    return jnp.dot(lhs, rhs, preferred_element_type=F32)


def _gdn_kernel(a_ref, ba_ref, cw_ref, aux_ref, nw_ref, o_ref, xe_ref, q_s, k_s, v_s, state_ref):
    t_rows = a_ref.shape[0]
    st = pl.program_id(1)

    @pl.when(st == 0)
    def _():
        xe_ref[0:8, :] = jnp.zeros((8, CONV_CH), F32)
        state_ref[...] = jnp.zeros(state_ref.shape, F32)

    xe_ref[8:8 + t_rows, :] = a_ref[:, :CONV_CH]
    for s in range(CONV_CH // LANES):
        cs = slice(LANES * s, LANES * (s + 1))
        y = jnp.zeros((t_rows, LANES), F32)
        for j in range(CONV_WIDTH):
            off = 8 - (CONV_WIDTH - 1) + j
            y = y + xe_ref[off:off + t_rows, cs] * cw_ref[j:j + 1, cs]
        y = y * _sigmoid(y)
        grp, h = divmod(s, GDN_HEADS)
        hs = slice(LANES * h, LANES * (h + 1))
        if grp == 0:
            q_s[:, hs] = y * lax.rsqrt(jnp.sum(y * y, -1, keepdims=True) + 1e-6) * (GDN_DK ** -0.5)
        elif grp == 1:
            k_s[:, hs] = y * lax.rsqrt(jnp.sum(y * y, -1, keepdims=True) + 1e-6)
        else:
            v_s[:, hs] = y
    xe_ref[0:8, :] = xe_ref[t_rows:t_rows + 8, :]

    ba = ba_ref[...]
    beta_t = _sigmoid(ba)
    gx = ba + aux_ref[1:2, :]
    g_t = -jnp.exp(aux_ref[0:1, :]) * (jnp.maximum(gx, 0.0) + jnp.log(1.0 + jnp.exp(-jnp.abs(gx))))

    ri = lax.broadcasted_iota(jnp.int32, (HSTACK, HSTACK), 0)
    ci = lax.broadcasted_iota(jnp.int32, (HSTACK, HSTACK), 1)
    head_start = ri - (ri & (CHUNK - 1))
    in_head = ci >= head_start
    incl_f = jnp.where(in_head, jnp.where(ci <= ri, 1.0, 0.0), 0.0)
    strict_f = jnp.where(in_head, jnp.where(ci < ri, 1.0, 0.0), 0.0)
    sub_f = jnp.where(ci >= ri - (ri & (SUBBLK - 1)), 1.0, 0.0)
    eye = jnp.where(ri == ci, 1.0, 0.0)
    tri_b = (lax.broadcasted_iota(jnp.int32, (CHUNK, CHUNK), 0)
             >= lax.broadcasted_iota(jnp.int32, (CHUNK, CHUNK), 1)).astype(BF16)
    nt = (((1,), (1,)), ((), ()))
    tn = (((0,), (0,)), ((), ()))
    bdot = functools.partial(jnp.dot, preferred_element_type=F32)

    def stack(fn):
        return jnp.concatenate([fn(h) for h in range(GDN_HEADS)], axis=0)

    for c in range(t_rows // CHUNK):
        rows = slice(CHUNK * c, CHUNK * (c + 1))
        g = g_t[rows]
        g1 = g.astype(BF16)
        r1 = g - g1.astype(F32)
        g2 = r1.astype(BF16)
        g3 = (r1 - g2.astype(F32)).astype(BF16)
        gc = bdot(tri_b, g1) + bdot(tri_b, g2) + bdot(tri_b, g3)
        gct = gc.T
        beta = beta_t[rows]
        gcol = stack(lambda h: jnp.broadcast_to(gc[:, GDN_HEADS + h:GDN_HEADS + h + 1], (CHUNK, LANES)))
        glast = stack(lambda h: jnp.broadcast_to(gc[CHUNK - 1:CHUNK, GDN_HEADS + h:GDN_HEADS + h + 1],
                                                 (CHUNK, LANES)))
        bcol = stack(lambda h: jnp.broadcast_to(beta[:, h:h + 1], (CHUNK, LANES)))
        grow = jnp.concatenate([gct[GDN_HEADS + h:GDN_HEADS + h + 1, :] for h in range(GDN_HEADS)], axis=1)
        kk = stack(lambda h: k_s[rows, LANES * h:LANES * (h + 1)])
        qq = stack(lambda h: q_s[rows, LANES * h:LANES * (h + 1)])
        vv = stack(lambda h: v_s[rows, LANES * h:LANES * (h + 1)])

        gcol2 = jnp.concatenate([gcol, gcol], axis=1)
        dec = jnp.exp(jnp.minimum(gcol2 - grow, 0.0)) * incl_f
        kb = kk * bcol
        kb16 = kk.astype(BF16)
        amat = lax.dot_general(kb.astype(BF16), kb16, nt, preferred_element_type=F32) * dec * strict_f
        bm = amat * sub_f
        nm = amat - bm
        b2 = _dot3(bm, bm)
        b4 = _dot3(b2, b2)
        b8 = _dot3(b4, b4)
        dinv = _dot3(_dot3(_dot3(eye - bm, eye + b2), eye + b4), eye + b8)
        mm = _dot3(dinv, nm)
        m2 = _dot3(mm, mm)
        tinv = _dot3(_dot3(eye - mm, eye + m2), dinv)
        eg = jnp.exp(gcol)
        sol = _dot3(tinv, jnp.concatenate([vv * bcol, kb * eg], axis=1))
        u = sol[:, :LANES]
        w = sol[:, LANES:]
        qk = lax.dot_general(qq.astype(BF16), kb16, nt, preferred_element_type=F32) * dec
        qd = qq * eg
        kd = kk * jnp.exp(glast - gcol)

        ws, qs = [], []
        for h in range(GDN_HEADS):
            hr = slice(CHUNK * h, CHUNK * (h + 1))
            lhs = jnp.concatenate([w[hr], qd[hr]], axis=0).astype(BF16)
            r = bdot(lhs, state_ref[h].astype(BF16))
            ws.append(r[:CHUNK])
            qs.append(r[CHUNK:])
        vn = u - jnp.concatenate(ws, axis=0)
        vn16 = vn.astype(BF16)
        o = jnp.concatenate(qs, axis=0) + bdot(qk.astype(BF16), vn16)
        kd16 = kd.astype(BF16)
        for h in range(GDN_HEADS):
            hr = slice(CHUNK * h, CHUNK * (h + 1))
            hs = slice(LANES * h, LANES * (h + 1))
            gl = jnp.exp(gc[CHUNK - 1:CHUNK, GDN_HEADS + h:GDN_HEADS + h + 1])
            state_ref[h] = state_ref[h] * gl + lax.dot_general(kd16[hr], vn16[hr], tn,
                                                               preferred_element_type=F32)
            oh = o[hr]
            z = a_ref[rows, CONV_CH + LANES * h:CONV_CH + LANES * (h + 1)]
            oh = oh * lax.rsqrt(jnp.mean(oh * oh, -1, keepdims=True) + RMS_EPS) * nw_ref[...]
            o_ref[rows, hs] = (oh * (z * _sigmoid(z))).astype(o_ref.dtype)


def _gdn(a_proj, ba_proj, conv_w, a_log, dt_bias, norm_w, batch, seq):
    t = T_GDN
    nst = seq // t
    rowblk = lambda w: pl.BlockSpec((t, w), lambda b, s: (b * nst + s, 0))
    aux = jnp.zeros((8, LANES), F32)
    aux = aux.at[0, GDN_HEADS:2 * GDN_HEADS].set(a_log).at[1, GDN_HEADS:2 * GDN_HEADS].set(dt_bias)
    return pl.pallas_call(
        _gdn_kernel,
        grid=(batch, nst),
        in_specs=[rowblk(A_COLS), rowblk(LANES),
                  pl.BlockSpec((CONV_WIDTH, CONV_CH), lambda b, s: (0, 0)),
                  pl.BlockSpec((8, LANES), lambda b, s: (0, 0)),
                  pl.BlockSpec((1, GDN_DV), lambda b, s: (0, 0))],
        out_specs=rowblk(GDN_W),
        out_shape=jax.ShapeDtypeStruct((batch * seq, GDN_W), BF16),
        scratch_shapes=[pltpu.VMEM((t + 8, CONV_CH), F32),
                        pltpu.VMEM((t, GDN_W), F32),
                        pltpu.VMEM((t, GDN_W), F32),
                        pltpu.VMEM((t, GDN_W), F32),
                        pltpu.VMEM((GDN_HEADS, GDN_DK, GDN_DV), F32)],
        compiler_params=pltpu.CompilerParams(dimension_semantics=("arbitrary", "arbitrary"),
                                             vmem_limit_bytes=VMEM_LIMIT),
        name="gdn",
    )(a_proj, ba_proj, conv_w, aux, norm_w.reshape(1, GDN_DV))


def _attn_kernel(q_ref, k_ref, v_ref, lq1_ref, lk1_ref, lq2_ref, lk2_ref, nw_ref, o_ref, *, lam_init):
    tq = q_ref.shape[0]
    qi = pl.program_id(2)
    q = q_ref[...]
    lane = lax.broadcasted_iota(jnp.int32, (1, LANES), 1)
    zero = jnp.zeros_like(q)
    q2 = jnp.concatenate([jnp.where(lane < DIFF_D, q, zero), jnp.where(lane >= DIFF_D, q, zero)], axis=0)
    r1 = lax.broadcasted_iota(jnp.int32, (tq, 1), 0)
    row = qi * tq + jnp.concatenate([r1, r1], axis=0)
    nt = (((1,), (1,)), ((), ()))

    def make_body(masked):
        def body(kj, carry):
            m, l, acc = carry
            off = pl.multiple_of(kj * TKV, TKV)
            k = k_ref[pl.ds(off, TKV), :]
            v = v_ref[pl.ds(off, TKV), :]
            s = lax.dot_general(q2, k, nt, preferred_element_type=F32)
            if masked:
                col = off + lax.broadcasted_iota(jnp.int32, (1, TKV), 1)
                s = jnp.where(col <= row, s, -1e30)
            mn = jnp.maximum(m, jnp.max(s, -1, keepdims=True))
            p = jnp.exp2(s - mn)
            al = jnp.exp2(m - mn)
            l = al * l + jnp.sum(p, -1, keepdims=True)
            acc = al * acc + jnp.dot(p.astype(BF16), v, preferred_element_type=F32)
            return mn, l, acc
        return body

    init = (jnp.full((2 * tq, 1), -1e30, F32), jnp.zeros((2 * tq, 1), F32), jnp.zeros((2 * tq, DIFF_DV), F32))
    n_full = (qi * tq) // TKV
    n_all = (qi * tq + tq + TKV - 1) // TKV
    carry = lax.fori_loop(0, n_full, make_body(False), init)
    m, l, acc = lax.fori_loop(n_full, n_all, make_body(True), carry)
    lam = (jnp.exp(jnp.sum(lq1_ref[...] * lk1_ref[...], -1, keepdims=True))
           - jnp.exp(jnp.sum(lq2_ref[...] * lk2_ref[...], -1, keepdims=True)) + lam_init)
    on = acc * (1.0 / l)
    o = on[:tq] - lam * on[tq:]
    o = o * lax.rsqrt(jnp.mean(o * o, -1, keepdims=True) + RMS_EPS) * nw_ref[...] * (1.0 - lam_init)
    o_ref[...] = o.astype(o_ref.dtype)


def _diff_attention(q, k, v, lq1, lk1, lq2, lk2, norm_w, lam_init, batch, seq):
    q3 = q.reshape(batch, seq, DIFF_W)
    k3 = k.reshape(batch, seq, DIFF_W)
    v3 = v.reshape(batch, seq, DIFF_W)
    small = lambda w: pl.BlockSpec((1, w), lambda b, h, i: (0, 0))
    out = pl.pallas_call(
        functools.partial(_attn_kernel, lam_init=lam_init),
        grid=(batch, DIFF_HEADS, seq // TQ),
        in_specs=[pl.BlockSpec((None, TQ, LANES), lambda b, h, i: (b, i, h)),
                  pl.BlockSpec((None, seq, LANES), lambda b, h, i: (b, 0, h)),
                  pl.BlockSpec((None, seq, LANES), lambda b, h, i: (b, 0, h)),
                  small(DIFF_D), small(DIFF_D), small(DIFF_D), small(DIFF_D), small(DIFF_DV)],
        out_specs=pl.BlockSpec((None, TQ, LANES), lambda b, h, i: (b, i, h)),
        out_shape=jax.ShapeDtypeStruct((batch, seq, DIFF_W), BF16),
        compiler_params=pltpu.CompilerParams(
            dimension_semantics=("arbitrary", "arbitrary", "arbitrary"),
            vmem_limit_bytes=VMEM_LIMIT),
        name="diff_attn",
    )(q3, k3, v3, lq1.reshape(1, -1), lk1.reshape(1, -1), lq2.reshape(1, -1), lk2.reshape(1, -1),
      norm_w.reshape(1, -1))
    return out.reshape(batch * seq, DIFF_W)


def _outproj_kernel(og_ref, od_ref, x_ref, wo_ref, g_ref, b_ref, rw_ref, rb_ref,
                    h_ref, idx_ref, gate_ref, cnt_out_ref, cnt_ref):
    @pl.when(pl.program_id(0) == 0)
    def _():
        cnt_ref[...] = jnp.zeros(cnt_ref.shape, F32)

    mix = (jnp.dot(og_ref[...], wo_ref[:GDN_W, :], preferred_element_type=F32)
           + jnp.dot(od_ref[...], wo_ref[GDN_W:, :], preferred_element_type=F32))
    h = _layer_norm(DN_ALPHA * x_ref[...] + mix, g_ref[...], b_ref[...])
    h_ref[...] = h
    logits = jnp.dot(h, rw_ref[...], precision=HI, preferred_element_type=F32) + rb_ref[...]
    tm = logits.shape[0]
    lane = lax.broadcasted_iota(jnp.int32, (tm, LANES), 1)
    lane_f = lane.astype(F32)
    work = logits
    vals, idxs = [], []
    for _ in range(TOP_K):
        m = jnp.max(work, -1, keepdims=True)
        sel = jnp.min(jnp.where(work == m, lane_f, float(LANES)), -1, keepdims=True)
        vals.append(m)
        idxs.append(sel)
        work = jnp.where(lane_f == sel, -jnp.inf, work)
    exps = [jnp.exp(vv - vals[0]) for vv in vals]
    inv = 1.0 / (exps[0] + exps[1] + exps[2] + exps[3])

    hot = jnp.zeros((tm, LANES), F32)
    for j in range(TOP_K):
        hot = hot + jnp.where(lane_f == idxs[j], 1.0, 0.0)
    ri = lax.broadcasted_iota(jnp.int32, (tm, tm), 0)
    ci = lax.broadcasted_iota(jnp.int32, (tm, tm), 1)
    before = jnp.where(ci < ri, 1.0, 0.0).astype(BF16)
    prefix = jnp.dot(before, hot.astype(BF16), preferred_element_type=F32) + cnt_ref[...]
    cnt_ref[...] = cnt_ref[...] + jnp.sum(hot, 0, keepdims=True)
    cnt_out_ref[...] = cnt_ref[...].astype(jnp.int32)

    idx_out = jnp.zeros((tm, LANES), F32)
    gate_out = jnp.zeros((tm, LANES), F32)
    for j in range(TOP_K):
        rank = jnp.sum(jnp.where(lane_f == idxs[j], prefix, 0.0), -1, keepdims=True)
        idx_out = jnp.where(lane == j, idxs[j], idx_out)
        idx_out = jnp.where(lane == TOP_K + j, rank, idx_out)
        gate_out = jnp.where(lane == j, exps[j] * inv, gate_out)
    idx_ref[...] = idx_out.astype(jnp.int32)
    gate_ref[...] = gate_out


def _outproj(og, od, x2d, w_out_b, ln_g, ln_b, rw_pad, rb_pad):
    n = x2d.shape[0]
    tm = TM_PROJ
    row = lambda w: pl.BlockSpec((tm, w), lambda i: (i, 0))
    full = lambda r, c: pl.BlockSpec((r, c), lambda i: (0, 0))
    return pl.pallas_call(
        _outproj_kernel,
        grid=(n // tm,),
        in_specs=[row(GDN_W), row(DIFF_W), row(D_MODEL), full(GDN_W + DIFF_W, D_MODEL),
                  full(1, D_MODEL), full(1, D_MODEL), full(D_MODEL, LANES), full(1, LANES)],
        out_specs=[row(D_MODEL), row(LANES), row(LANES), full(1, LANES)],
        out_shape=[jax.ShapeDtypeStruct((n, D_MODEL), F32),
                   jax.ShapeDtypeStruct((n, LANES), jnp.int32),
                   jax.ShapeDtypeStruct((n, LANES), F32),
                   jax.ShapeDtypeStruct((1, LANES), jnp.int32)],
        scratch_shapes=[pltpu.VMEM((1, LANES), F32)],
        compiler_params=pltpu.CompilerParams(dimension_semantics=("arbitrary",),
                                             vmem_limit_bytes=VMEM_LIMIT),
        name="outproj_ln1_router",
    )(og, od, x2d, w_out_b, ln_g.reshape(1, -1), ln_b.reshape(1, -1), rw_pad, rb_pad)


def _dispatch_kernel(pos_ref, h_ref, xs_hbm, sem):
    tc = h_ref.shape[0]
    base = pl.program_id(0) * (tc * TOP_K)

    def body(r, carry):
        for j in range(TOP_K):
            s = pos_ref[base + r * TOP_K + j]
            pltpu.make_async_copy(h_ref.at[pl.ds(r, 1), :], xs_hbm.at[pl.ds(s, 1), :], sem).start()
        return carry

    lax.fori_loop(0, tc, body, 0, unroll=4)
    for j in range(TOP_K):
        pltpu.make_async_copy(h_ref, xs_hbm.at[pl.ds(0, tc), :], sem).wait()


def _dispatch(pos, h1):
    n = h1.shape[0]
    tc = TC_DISP
    grid_spec = pltpu.PrefetchScalarGridSpec(
        num_scalar_prefetch=1,
        grid=(n // tc,),
        in_specs=[pl.BlockSpec((tc, D_MODEL), lambda i, pos: (i, 0))],
        out_specs=pl.BlockSpec(memory_space=pl.ANY),
        scratch_shapes=[pltpu.SemaphoreType.DMA],
    )
    return pl.pallas_call(
        _dispatch_kernel,
        grid_spec=grid_spec,
        out_shape=jax.ShapeDtypeStruct((n * TOP_K, D_MODEL), F32),
        compiler_params=pltpu.CompilerParams(dimension_semantics=("arbitrary",),
                                             vmem_limit_bytes=VMEM_LIMIT),
        name="dispatch",
    )(pos, h1)


def _moe_kernel(we_ref, wt_ref, lo_ref, hi_ref, nw_ref, xs_ref, wgu_ref, bgu_ref, wd_ref, bd_ref, ys_ref,
                wgu_b, wd_b):
    tm = ys_ref.shape[0]
    w = pl.program_id(0)

    @pl.when(w < nw_ref[0])
    def _():
        prev = jnp.maximum(w - 1, 0)

        @pl.when((w == 0) | (we_ref[w] != we_ref[prev]))
        def _():
            rows = 128

            def cast(j, carry):
                r0 = pl.multiple_of(j * rows, rows)
                wgu_b[pl.ds(r0, rows), :] = wgu_ref[0, pl.ds(r0, rows), :].astype(BF16)
                wd_b[pl.ds(r0, rows), :] = wd_ref[0, pl.ds(r0, rows), :].astype(BF16)
                return carry

            lax.fori_loop(0, D_MODEL // rows, cast, 0)

        xb = xs_ref[...].astype(BF16)
        hgu = jnp.dot(xb, wgu_b[...], preferred_element_type=F32) + bgu_ref[0]
        gate = jnp.minimum(hgu[:, :D_FF], SWIGLU_LIMIT)
        up = jnp.clip(hgu[:, D_FF:], -SWIGLU_LIMIT, SWIGLU_LIMIT)
        act = (up + 1.0) * gate * _sigmoid(SWIGLU_ALPHA * gate)
        y = jnp.dot(act.astype(BF16), wd_b[...], preferred_element_type=F32) + bd_ref[0]
        rid = lax.broadcasted_iota(jnp.int32, (tm, 1), 0)
        mine = jnp.where(rid >= lo_ref[w], jnp.where(rid < hi_ref[w], 1.0, 0.0), 0.0) > 0.5
        first = (w == 0) | (wt_ref[w] != wt_ref[prev])

        @pl.when(first)
        def _():
            ys_ref[...] = jnp.where(mine, y, 0.0)

        @pl.when(jnp.logical_not(first))
        def _():
            ys_ref[...] = jnp.where(mine, y, ys_ref[...])


def _moe(work, xs, w_gu, b_gu, w_down, b_down):
    tm = TM_MOE
    m = xs.shape[0]
    n_items = m // tm + N_EXPERTS - 1
    widx = lambda f: (lambda w, we, wt, lo, hi, nw: f(w, we, wt))
    grid_spec = pltpu.PrefetchScalarGridSpec(
        num_scalar_prefetch=5,
        grid=(n_items,),
        in_specs=[pl.BlockSpec((tm, D_MODEL), widx(lambda w, we, wt: (wt[w], 0))),
                  pl.BlockSpec((1, D_MODEL, 2 * D_FF), widx(lambda w, we, wt: (we[w], 0, 0))),
                  pl.BlockSpec((1, 1, 2 * D_FF), widx(lambda w, we, wt: (we[w], 0, 0))),
                  pl.BlockSpec((1, D_FF, D_MODEL), widx(lambda w, we, wt: (we[w], 0, 0))),
                  pl.BlockSpec((1, 1, D_MODEL), widx(lambda w, we, wt: (we[w], 0, 0)))],
        out_specs=pl.BlockSpec((tm, D_MODEL), widx(lambda w, we, wt: (wt[w], 0))),
        scratch_shapes=[pltpu.VMEM((D_MODEL, 2 * D_FF), BF16),
                        pltpu.VMEM((D_FF, D_MODEL), BF16)],
    )
    return pl.pallas_call(
        _moe_kernel,
        grid_spec=grid_spec,
        out_shape=jax.ShapeDtypeStruct((m, D_MODEL), F32),
        compiler_params=pltpu.CompilerParams(dimension_semantics=("arbitrary",),
                                             vmem_limit_bytes=VMEM_LIMIT),
        name="moe_ffn",
    )(*work, xs, w_gu, b_gu.reshape(N_EXPERTS, 1, -1), w_down, b_down.reshape(N_EXPERTS, 1, -1))


def _combine_kernel(pos_ref, h_ref, gate_ref, p_ref, ys_hbm, g2_ref, b2_ref, wg_ref, bg_ref, wp_ref,
                    g3_ref, b3_ref, o_ref, ybuf, sem):
    tc = h_ref.shape[0]
    i = pl.program_id(0)
    n = pl.num_programs(0)

    def issue(blk, buf):
        base = blk * (TOP_K * tc)

        def body(r, carry):
            for j in range(TOP_K):
                s = pos_ref[base + r * TOP_K + j]
                pltpu.make_async_copy(ys_hbm.at[pl.ds(s, 1), :], ybuf.at[buf, j, pl.ds(r, 1), :],
                                      sem.at[buf]).start()
            return carry

        lax.fori_loop(0, tc, body, 0, unroll=4)

    @pl.when(i == 0)
    def _():
        issue(0, 0)

    @pl.when(i + 1 < n)
    def _():
        issue(i + 1, (i + 1) % 2)

    buf = i % 2
    for j in range(TOP_K):
        pltpu.make_async_copy(ys_hbm.at[pl.ds(0, tc), :], ybuf.at[buf, j], sem.at[buf]).wait()
    gates = gate_ref[...]
    ffn = ybuf[buf, 0] * gates[:, 0:1]
    for j in range(1, TOP_K):
        ffn = ffn + ybuf[buf, j] * gates[:, j:j + 1]
    h2 = _layer_norm(DN_ALPHA * h_ref[...] + ffn, g2_ref[...], b2_ref[...])
    gate = _sigmoid(jnp.dot(h2.astype(BF16), wg_ref[...], preferred_element_type=F32) + bg_ref[...])
    ple = gate * jnp.dot(p_ref[...].astype(BF16), wp_ref[...], preferred_element_type=F32)
    o_ref[...] = _layer_norm(DN_ALPHA * h2 + ple, g3_ref[...], b3_ref[...])


def _combine(pos, h1, gates, p2d, ys, g2, b2, wg_b, bg, wp_b, g3, b3):
    n = h1.shape[0]
    tc = TC_COMB
    row = lambda w: pl.BlockSpec((tc, w), lambda i, pos: (i, 0))
    full = lambda r, c: pl.BlockSpec((r, c), lambda i, pos: (0, 0))
    grid_spec = pltpu.PrefetchScalarGridSpec(
        num_scalar_prefetch=1,
        grid=(n // tc,),
        in_specs=[row(D_MODEL), row(LANES), row(PLE_DIM), pl.BlockSpec(memory_space=pl.ANY),
                  full(1, D_MODEL), full(1, D_MODEL), full(D_MODEL, D_MODEL), full(1, D_MODEL),
                  full(PLE_DIM, D_MODEL), full(1, D_MODEL), full(1, D_MODEL)],
        out_specs=row(D_MODEL),
        scratch_shapes=[pltpu.VMEM((2, TOP_K, tc, D_MODEL), F32),
                        pltpu.SemaphoreType.DMA((2,))],
    )
    return pl.pallas_call(
        _combine_kernel,
        grid_spec=grid_spec,
        out_shape=jax.ShapeDtypeStruct((n, D_MODEL), F32),
        compiler_params=pltpu.CompilerParams(dimension_semantics=("arbitrary",),
                                             vmem_limit_bytes=VMEM_LIMIT),
        name="combine_ln2_ple_ln3",
    )(pos, h1, gates, p2d, ys, g2.reshape(1, -1), b2.reshape(1, -1), wg_b, bg.reshape(1, -1), wp_b,
      g3.reshape(1, -1), b3.reshape(1, -1))


def _routing(idx, rank, counts, tm):
    n_tok = idx.shape[0]
    m = n_tok * TOP_K
    n_tiles = m // tm
    n_items = n_tiles + N_EXPERTS - 1
    grp_end = jnp.cumsum(counts)
    grp_start = grp_end - counts
    experts = jnp.arange(N_EXPERTS, dtype=jnp.int32)
    pos = (jnp.sum(jnp.where(idx[:, :, None] == experts, grp_start, 0), -1) + rank).reshape(-1).astype(jnp.int32)
    t_first = grp_start // tm
    t_last = (grp_end - 1) // tm
    n_e = jnp.where(counts > 0, t_last - t_first + 1, 0)
    w_end = jnp.cumsum(n_e)
    w_start = w_end - n_e
    n_work = w_end[-1]
    w = jnp.arange(n_items, dtype=jnp.int32)
    valid = w < n_work
    e_w = jnp.clip(jnp.searchsorted(w_end, jnp.minimum(w, n_work - 1), side='right'), 0, N_EXPERTS - 1)
    tile_w = t_first[e_w] + (jnp.minimum(w, n_work - 1) - w_start[e_w])
    lo = jnp.clip(grp_start[e_w] - tile_w * tm, 0, tm)
    hi = jnp.clip(grp_end[e_w] - tile_w * tm, 0, tm)
    lo = jnp.where(valid, lo, 0)
    hi = jnp.where(valid, hi, 0)
    i32 = lambda a: a.astype(jnp.int32)
    return pos, (i32(e_w), i32(tile_w), i32(lo), i32(hi), i32(n_work).reshape(1))


def _layer(h, p_i, cos8, sin8, lam_init, w_in, conv_w, a_log, dt_bias, gdn_norm_w,
           lam_q1, lam_k1, lam_q2, lam_k2, diff_norm_w, w_out, ln1_g, ln1_b,
           router_w, router_b, w_gu, b_gu, w_down, b_down, ln2_g, ln2_b,
           ple_w, ple_gate_w, ple_gate_b, ln3_g, ln3_b):
    batch, seq, d = h.shape
    n = batch * seq
    x2d = h.reshape(n, d)
    o_ba = 4 * GDN_W
    o_d = o_ba + 2 * GDN_HEADS
    w_r = jnp.concatenate([w_in[:, :o_ba], w_in[:, o_d:], w_in[:, o_ba:o_d],
                           jnp.zeros((d, LANES - 2 * GDN_HEADS), w_in.dtype)], -1).astype(BF16)
    ones = jnp.ones((n, DIFF_D - ROPE_DIM), F32)
    zeros = jnp.zeros((n, DIFF_D - ROPE_DIM), F32)
    z8 = jnp.zeros_like(sin8)
    ctab = jnp.tile(jnp.concatenate([cos8, cos8, ones], -1), (1, 2))
    satab = jnp.tile(jnp.concatenate([z8, sin8, zeros], -1), (1, 2))
    sbtab = jnp.tile(jnp.concatenate([-sin8, z8, zeros], -1), (1, 2))
    a_proj, ba_proj, dq, dk, dv = _inproj(x2d, w_r, ctab, satab, sbtab)
    o_gdn = _gdn(a_proj, ba_proj, conv_w, a_log, dt_bias, gdn_norm_w, batch, seq)
    o_diff = _diff_attention(dq, dk, dv, lam_q1, lam_k1, lam_q2, lam_k2, diff_norm_w, lam_init, batch, seq)
    rw_pad = jnp.concatenate([router_w, jnp.zeros((d, LANES - N_EXPERTS), F32)], -1)
    rb_pad = jnp.concatenate([router_b, jnp.full((LANES - N_EXPERTS,), -jnp.inf, F32)]).reshape(1, LANES)
    h1, idx, gates, cnt = _outproj(o_gdn, o_diff, x2d, w_out.astype(BF16), ln1_g, ln1_b, rw_pad, rb_pad)
    pos, work = _routing(idx[:, :TOP_K], idx[:, TOP_K:2 * TOP_K], cnt[0, :N_EXPERTS], TM_MOE)
    xs = _dispatch(pos, h1)
    ys = _moe(work, xs, w_gu, b_gu, w_down, b_down)
    out = _combine(pos, h1, gates, p_i.reshape(n, PLE_DIM), ys, ln2_g, ln2_b,
                   ple_gate_w.astype(BF16), ple_gate_b, ple_w.astype(BF16), ln3_g, ln3_b)
    return out.reshape(batch, seq, d)


def kernel(x, p, positions, w_in, conv_w, a_log, dt_bias, gdn_norm_w, lam_q1, lam_k1, lam_q2, lam_k2,
           diff_norm_w, w_out, ln1_g, ln1_b, router_w, router_b, w_gu, b_gu, w_down, b_down, ln2_g, ln2_b,
           ple_w, ple_gate_w, ple_gate_b, ln3_g, ln3_b):
    batch, seq, _ = x.shape
    inv_freq = ROPE_THETA ** (-jnp.arange(0, ROPE_DIM, 2, dtype=F32) / ROPE_DIM)
    ang = (positions.astype(F32)[..., None] * inv_freq).reshape(batch * seq, ROPE_DIM // 2)
    cos8 = jnp.cos(ang)
    sin8 = jnp.sin(ang)
    h = x
    for i in range(w_in.shape[0]):
        lam_init = 0.8 - 0.6 * math.exp(-0.3 * i)
        h = _layer(h, p[i], cos8, sin8, lam_init, w_in[i], conv_w[i], a_log[i], dt_bias[i], gdn_norm_w[i],
                   lam_q1[i], lam_k1[i], lam_q2[i], lam_k2[i], diff_norm_w[i], w_out[i], ln1_g[i], ln1_b[i],
                   router_w[i], router_b[i], w_gu[i], b_gu[i], w_down[i], b_down[i], ln2_g[i], ln2_b[i],
                   ple_w[i], ple_gate_w[i], ple_gate_b[i], ln3_g[i], ln3_b[i])
    return h
```

```python
import functools
import math

import jax
import jax.numpy as jnp
from jax import lax
from jax.experimental import pallas as pl
from jax.experimental.pallas import tpu as pltpu

F32 = jnp.float32
BF16 = jnp.bfloat16
HI = lax.Precision.HIGHEST
LOG2E = 1.4426950408889634

D_MODEL = 1024
PLE_DIM = 256
GDN_HEADS = 4
GDN_DK = 128
GDN_DV = 128
CONV_WIDTH = 4
CHUNK = 64
DIFF_HEADS = 4
DIFF_D = 64
DIFF_DV = 2 * DIFF_D
ROPE_THETA = 500000.0
ROPE_DIM = DIFF_D // 4
N_EXPERTS = 32
TOP_K = 4
D_FF = D_MODEL
SWIGLU_LIMIT = 7.0
SWIGLU_ALPHA = 1.702
DEPTH = 1
DN_ALPHA = (2 * DEPTH) ** 0.25
LN_EPS = 1e-5
RMS_EPS = 1e-6

LANES = 128
GDN_W = GDN_HEADS * GDN_DK
CONV_CH = 3 * GDN_W
A_COLS = 4 * GDN_W
DIFF_W = DIFF_HEADS * DIFF_DV
IN_PAD_W = A_COLS + 3 * DIFF_W + LANES
HSTACK = GDN_HEADS * CHUNK
SUBBLK = 16

VMEM_LIMIT = 56 * 1024 * 1024

TM_PROJ = 512
T_GDN = 256
TQ = 256
TKV = 256
TM_MOE = 256
TC_DISP = 512
TC_COMB = 256


def _layer_norm(y, g, b):
    mu = jnp.mean(y, -1, keepdims=True)
    d = y - mu
    var = jnp.mean(d * d, -1, keepdims=True)
    return d * lax.rsqrt(var + LN_EPS) * g + b


def _sigmoid(x):
    return 1.0 / (1.0 + jnp.exp(-x))


def _inproj_kernel(x_ref, w_ref, c_ref, sa_ref, sb_ref, a_ref, ba_ref, q_ref, k_ref, v_ref):
    xb = x_ref[...].astype(BF16)
    a_ref[...] = jnp.dot(xb, w_ref[:, :A_COLS], preferred_element_type=F32)
    ba_ref[...] = jnp.dot(xb, w_ref[:, A_COLS + 3 * DIFF_W:], preferred_element_type=F32)
    c = c_ref[...]
    sa = sa_ref[...]
    sb = sb_ref[...]

    def rot(t):
        return t * c + pltpu.roll(t, 8, 1) * sa + pltpu.roll(t, LANES - 8, 1) * sb

    for h in range(DIFF_HEADS):
        lo = A_COLS + LANES * h
        q = jnp.dot(xb, w_ref[:, lo:lo + LANES], preferred_element_type=F32)
        q_ref[:, LANES * h:LANES * (h + 1)] = (rot(q) * (DIFF_D ** -0.5 * LOG2E)).astype(BF16)
        lo = A_COLS + DIFF_W + LANES * h
        k = jnp.dot(xb, w_ref[:, lo:lo + LANES], preferred_element_type=F32)
        k_ref[:, LANES * h:LANES * (h + 1)] = rot(k).astype(BF16)
    lo = A_COLS + 2 * DIFF_W
    v_ref[...] = jnp.dot(xb, w_ref[:, lo:lo + DIFF_W], preferred_element_type=F32).astype(BF16)


def _inproj(x2d, w_r, ctab, satab, sbtab):
    n = x2d.shape[0]
    tm = TM_PROJ
    row = lambda w: pl.BlockSpec((tm, w), lambda i: (i, 0))
    return pl.pallas_call(
        _inproj_kernel,
        grid=(n // tm,),
        in_specs=[row(D_MODEL),
                  pl.BlockSpec((D_MODEL, IN_PAD_W), lambda i: (0, 0)),
                  row(LANES), row(LANES), row(LANES)],
        out_specs=[row(A_COLS), row(LANES), row(DIFF_W), row(DIFF_W), row(DIFF_W)],
        out_shape=[jax.ShapeDtypeStruct((n, A_COLS), F32),
                   jax.ShapeDtypeStruct((n, LANES), F32),
                   jax.ShapeDtypeStruct((n, DIFF_W), BF16),
                   jax.ShapeDtypeStruct((n, DIFF_W), BF16),
                   jax.ShapeDtypeStruct((n, DIFF_W), BF16)],
        compiler_params=pltpu.CompilerParams(dimension_semantics=("arbitrary",),
                                             vmem_limit_bytes=VMEM_LIMIT),
        name="inproj",
    )(x2d, w_r, ctab, satab, sbtab)


def _mm(a, b):
    return jnp.dot(a.astype(BF16), b.astype(BF16), preferred_element_type=F32)


def _gdn_kernel(a_ref, ba_ref, cw_ref, aux_ref, nw_ref, o_ref, xe_ref, q_s, k_s, v_s, state_ref):
    t_rows = a_ref.shape[0]
    st = pl.program_id(1)

    @pl.when(st == 0)
    def _():
        xe_ref[0:8, :] = jnp.zeros((8, CONV_CH), F32)
        state_ref[...] = jnp.zeros(state_ref.shape, F32)

    xe_ref[8:8 + t_rows, :] = a_ref[:, :CONV_CH]
    for s in range(CONV_CH // LANES):
        cs = slice(LANES * s, LANES * (s + 1))
        y = jnp.zeros((t_rows, LANES), F32)
        for j in range(CONV_WIDTH):
            off = 8 - (CONV_WIDTH - 1) + j
            y = y + xe_ref[off:off + t_rows, cs] * cw_ref[j:j + 1, cs]
        y = y * _sigmoid(y)
        grp, h = divmod(s, GDN_HEADS)
        hs = slice(LANES * h, LANES * (h + 1))
        if grp == 0:
            q_s[:, hs] = y * lax.rsqrt(jnp.sum(y * y, -1, keepdims=True) + 1e-6) * (GDN_DK ** -0.5)
        elif grp == 1:
            k_s[:, hs] = y * lax.rsqrt(jnp.sum(y * y, -1, keepdims=True) + 1e-6)
        else:
            v_s[:, hs] = y
    xe_ref[0:8, :] = xe_ref[t_rows:t_rows + 8, :]

    ba = ba_ref[...]
    beta_t = _sigmoid(ba)
    gx = ba + aux_ref[1:2, :]
    g_t = -jnp.exp(aux_ref[0:1, :]) * (jnp.maximum(gx, 0.0) + jnp.log(1.0 + jnp.exp(-jnp.abs(gx))))

    ri = lax.broadcasted_iota(jnp.int32, (HSTACK, HSTACK), 0)
    ci = lax.broadcasted_iota(jnp.int32, (HSTACK, HSTACK), 1)
    head_start = ri - (ri & (CHUNK - 1))
    in_head = ci >= head_start
    incl_f = jnp.where(in_head, jnp.where(ci <= ri, 1.0, 0.0), 0.0)
    strict_f = jnp.where(in_head, jnp.where(ci < ri, 1.0, 0.0), 0.0)
    sub_f = jnp.where(ci >= ri - (ri & (SUBBLK - 1)), 1.0, 0.0)
    eye = jnp.where(ri == ci, 1.0, 0.0)
    tri_b = (lax.broadcasted_iota(jnp.int32, (CHUNK, CHUNK), 0)
             >= lax.broadcasted_iota(jnp.int32, (CHUNK, CHUNK), 1)).astype(BF16)
    nt = (((1,), (1,)), ((), ()))
    tn = (((0,), (0,)), ((), ()))
    bdot = functools.partial(jnp.dot, preferred_element_type=F32)

    def stack(fn):
        return jnp.concatenate([fn(h) for h in range(GDN_HEADS)], axis=0)

    chunks = range(t_rows // CHUNK)
    crow = [slice(CHUNK * c, CHUNK * (c + 1)) for c in chunks]

    def cumdecay(g):
        g1 = g.astype(BF16)
        r1 = g - g1.astype(F32)
        g2 = r1.astype(BF16)
        g3 = (r1 - g2.astype(F32)).astype(BF16)
        return bdot(tri_b, g1) + bdot(tri_b, g2) + bdot(tri_b, g3)

    gc_l = [cumdecay(g_t[crow[c]]) for c in chunks]
    gct_l = [gc_l[c].T for c in chunks]
    gcol_l = [stack(lambda h: jnp.broadcast_to(gc_l[c][:, GDN_HEADS + h:GDN_HEADS + h + 1], (CHUNK, LANES)))
              for c in chunks]
    glast_l = [stack(lambda h: jnp.broadcast_to(gc_l[c][CHUNK - 1:CHUNK, GDN_HEADS + h:GDN_HEADS + h + 1],
                                                (CHUNK, LANES))) for c in chunks]
    bcol_l = [stack(lambda h: jnp.broadcast_to(beta_t[crow[c], h:h + 1], (CHUNK, LANES))) for c in chunks]
    grow_l = [jnp.concatenate([gct_l[c][GDN_HEADS + h:GDN_HEADS + h + 1, :] for h in range(GDN_HEADS)], axis=1)
              for c in chunks]
    kk_l = [stack(lambda h: k_s[crow[c], LANES * h:LANES * (h + 1)]) for c in chunks]
    qq_l = [stack(lambda h: q_s[crow[c], LANES * h:LANES * (h + 1)]) for c in chunks]
    vv_l = [stack(lambda h: v_s[crow[c], LANES * h:LANES * (h + 1)]) for c in chunks]
    dec_l = [jnp.exp(jnp.minimum(jnp.concatenate([gcol_l[c], gcol_l[c]], axis=1) - grow_l[c], 0.0)) * incl_f
             for c in chunks]
    kb_l = [kk_l[c] * bcol_l[c] for c in chunks]
    k16_l = [kk_l[c].astype(BF16) for c in chunks]
    amat_l = [lax.dot_general(kb_l[c].astype(BF16), k16_l[c], nt, preferred_element_type=F32)
              * dec_l[c] * strict_f for c in chunks]
    bm_l = [amat_l[c] * sub_f for c in chunks]
    nm_l = [amat_l[c] - bm_l[c] for c in chunks]
    b2_l = [_mm(bm_l[c], bm_l[c]) for c in chunks]
    b4_l = [_mm(b2_l[c], b2_l[c]) for c in chunks]
    b8_l = [_mm(b4_l[c], b4_l[c]) for c in chunks]
    d_l = [_mm(eye - bm_l[c], eye + b2_l[c]) for c in chunks]
    d_l = [_mm(d_l[c], eye + b4_l[c]) for c in chunks]
    dinv_l = [_mm(d_l[c], eye + b8_l[c]) for c in chunks]
    mm_l = [_mm(dinv_l[c], nm_l[c]) for c in chunks]
    m2_l = [_mm(mm_l[c], mm_l[c]) for c in chunks]
    t_l = [_mm(eye - mm_l[c], eye + m2_l[c]) for c in chunks]
    tinv_l = [_mm(t_l[c], dinv_l[c]) for c in chunks]
    eg_l = [jnp.exp(gcol_l[c]) for c in chunks]
    sol_l = [_mm(tinv_l[c], jnp.concatenate([vv_l[c] * bcol_l[c], kb_l[c] * eg_l[c]], axis=1)) for c in chunks]
    qk_l = [lax.dot_general(qq_l[c].astype(BF16), k16_l[c], nt, preferred_element_type=F32) * dec_l[c]
            for c in chunks]
    qd_l = [qq_l[c] * eg_l[c] for c in chunks]
    kd_l = [kk_l[c] * jnp.exp(glast_l[c] - gcol_l[c]) for c in chunks]

    for c in chunks:
        rows = crow[c]
        gc = gc_l[c]
        u = sol_l[c][:, :LANES]
        w = sol_l[c][:, LANES:]
        qk, qd, kd = qk_l[c], qd_l[c], kd_l[c]

        ws, qs = [], []
        for h in range(GDN_HEADS):
            hr = slice(CHUNK * h, CHUNK * (h + 1))
            lhs = jnp.concatenate([w[hr], qd[hr]], axis=0).astype(BF16)
            r = bdot(lhs, state_ref[h].astype(BF16))
            ws.append(r[:CHUNK])
            qs.append(r[CHUNK:])
        vn = u - jnp.concatenate(ws, axis=0)
        vn16 = vn.astype(BF16)
        o = jnp.concatenate(qs, axis=0) + bdot(qk.astype(BF16), vn16)
        kd16 = kd.astype(BF16)
        for h in range(GDN_HEADS):
            hr = slice(CHUNK * h, CHUNK * (h + 1))
            hs = slice(LANES * h, LANES * (h + 1))
            gl = jnp.exp(gc[CHUNK - 1:CHUNK, GDN_HEADS + h:GDN_HEADS + h + 1])
            state_ref[h] = state_ref[h] * gl + lax.dot_general(kd16[hr], vn16[hr], tn,
                                                               preferred_element_type=F32)
            oh = o[hr]
            z = a_ref[rows, CONV_CH + LANES * h:CONV_CH + LANES * (h + 1)]
            oh = oh * lax.rsqrt(jnp.mean(oh * oh, -1, keepdims=True) + RMS_EPS) * nw_ref[...]
            o_ref[rows, hs] = (oh * (z * _sigmoid(z))).astype(o_ref.dtype)


def _gdn(a_proj, ba_proj, conv_w, a_log, dt_bias, norm_w, batch, seq):
    t = T_GDN
    nst = seq // t
    rowblk = lambda w: pl.BlockSpec((t, w), lambda b, s: (b * nst + s, 0))
    aux = jnp.zeros((8, LANES), F32)
    aux = aux.at[0, GDN_HEADS:2 * GDN_HEADS].set(a_log).at[1, GDN_HEADS:2 * GDN_HEADS].set(dt_bias)
    return pl.pallas_call(
        _gdn_kernel,
        grid=(batch, nst),
        in_specs=[rowblk(A_COLS), rowblk(LANES),
                  pl.BlockSpec((CONV_WIDTH, CONV_CH), lambda b, s: (0, 0)),
                  pl.BlockSpec((8, LANES), lambda b, s: (0, 0)),
                  pl.BlockSpec((1, GDN_DV), lambda b, s: (0, 0))],
        out_specs=rowblk(GDN_W),
        out_shape=jax.ShapeDtypeStruct((batch * seq, GDN_W), BF16),
        scratch_shapes=[pltpu.VMEM((t + 8, CONV_CH), F32),
                        pltpu.VMEM((t, GDN_W), F32),
                        pltpu.VMEM((t, GDN_W), F32),
                        pltpu.VMEM((t, GDN_W), F32),
                        pltpu.VMEM((GDN_HEADS, GDN_DK, GDN_DV), F32)],
        compiler_params=pltpu.CompilerParams(dimension_semantics=("arbitrary", "arbitrary"),
                                             vmem_limit_bytes=VMEM_LIMIT),
        name="gdn",
    )(a_proj, ba_proj, conv_w, aux, norm_w.reshape(1, GDN_DV))


def _attn_kernel(q_ref, k_ref, vt_ref, lq1_ref, lk1_ref, lq2_ref, lk2_ref, nw_ref, o_ref, *, lam_init):
    tq = q_ref.shape[0]
    qi = pl.program_id(1)
    lane = lax.broadcasted_iota(jnp.int32, (1, LANES), 1)
    qpos = qi * tq + (lax.broadcasted_iota(jnp.int32, (1, 2 * tq), 1) & (tq - 1))
    nt = (((1,), (1,)), ((), ()))
    q2 = []
    for h in range(DIFF_HEADS):
        q = q_ref[:, LANES * h:LANES * (h + 1)]
        zero = jnp.zeros_like(q)
        q2.append(jnp.concatenate([jnp.where(lane < DIFF_D, q, zero), jnp.where(lane >= DIFF_D, q, zero)], axis=0))

    def make_body(masked):
        def body(kj, carry):
            off = pl.multiple_of(kj * TKV, TKV)
            if masked:
                keep = off + lax.broadcasted_iota(jnp.int32, (TKV, 1), 0) <= qpos
            heads = range(DIFF_HEADS)
            ss = [lax.dot_general(k_ref[pl.ds(off, TKV), LANES * h:LANES * (h + 1)], q2[h], nt,
                                  preferred_element_type=F32) for h in heads]
            if masked:
                ss = [jnp.where(keep, s, -1e30) for s in ss]
            mns = [jnp.maximum(carry[h][0], jnp.max(ss[h], 0, keepdims=True)) for h in heads]
            ps = [jnp.exp2(ss[h] - mns[h]) for h in heads]
            als = [jnp.exp2(carry[h][0] - mns[h]) for h in heads]
            ls = [als[h] * carry[h][1] + jnp.sum(ps[h], 0, keepdims=True) for h in heads]
            pvs = [jnp.dot(vt_ref[LANES * h:LANES * (h + 1), pl.ds(off, TKV)], ps[h].astype(BF16),
                           preferred_element_type=F32) for h in heads]
            return tuple((mns[h], ls[h], als[h] * carry[h][2] + pvs[h]) for h in heads)
        return body

    init = tuple((jnp.full((1, 2 * tq), -1e30, F32), jnp.zeros((1, 2 * tq), F32),
                  jnp.zeros((DIFF_DV, 2 * tq), F32)) for _ in range(DIFF_HEADS))
    carry = lax.fori_loop(0, qi, make_body(False), init)
    carry = make_body(True)(qi, carry)
    lam = (jnp.exp(jnp.sum(lq1_ref[...] * lk1_ref[...], -1, keepdims=True))
           - jnp.exp(jnp.sum(lq2_ref[...] * lk2_ref[...], -1, keepdims=True)) + lam_init)
    for h in range(DIFF_HEADS):
        m, l, acc = carry[h]
        on = acc * (1.0 / l)
        o = (on[:, :tq] - lam * on[:, tq:]).T
        o = o * lax.rsqrt(jnp.mean(o * o, -1, keepdims=True) + RMS_EPS) * nw_ref[...] * (1.0 - lam_init)
        o_ref[:, LANES * h:LANES * (h + 1)] = o.astype(o_ref.dtype)


def _diff_attention(q, k, v, lq1, lk1, lq2, lk2, norm_w, lam_init, batch, seq):
    q3 = q.reshape(batch, seq, DIFF_W)
    k3 = k.reshape(batch, seq, DIFF_W)
    v3 = v.reshape(batch, seq, DIFF_W).transpose(0, 2, 1)
    small = lambda w: pl.BlockSpec((1, w), lambda b, i: (0, 0))
    out = pl.pallas_call(
        functools.partial(_attn_kernel, lam_init=lam_init),
        grid=(batch, seq // TQ),
        in_specs=[pl.BlockSpec((None, TQ, DIFF_W), lambda b, i: (b, i, 0)),
                  pl.BlockSpec((None, seq, DIFF_W), lambda b, i: (b, 0, 0)),
                  pl.BlockSpec((None, DIFF_W, seq), lambda b, i: (b, 0, 0)),
                  small(DIFF_D), small(DIFF_D), small(DIFF_D), small(DIFF_D), small(DIFF_DV)],
        out_specs=pl.BlockSpec((None, TQ, DIFF_W), lambda b, i: (b, i, 0)),
        out_shape=jax.ShapeDtypeStruct((batch, seq, DIFF_W), BF16),
        compiler_params=pltpu.CompilerParams(
            dimension_semantics=("arbitrary", "arbitrary"),
            vmem_limit_bytes=VMEM_LIMIT),
        name="diff_attn",
    )(q3, k3, v3, lq1.reshape(1, -1), lk1.reshape(1, -1), lq2.reshape(1, -1), lk2.reshape(1, -1),
      norm_w.reshape(1, -1))
    return out.reshape(batch * seq, DIFF_W)


def _outproj_kernel(og_ref, od_ref, x_ref, wo_ref, g_ref, b_ref, rw_ref, rb_ref,
                    h_ref, idx_ref, gate_ref, cnt_out_ref, cnt_ref):
    @pl.when(pl.program_id(0) == 0)
    def _():
        cnt_ref[...] = jnp.zeros(cnt_ref.shape, F32)

    mix = (jnp.dot(og_ref[...], wo_ref[:GDN_W, :], preferred_element_type=F32)
           + jnp.dot(od_ref[...], wo_ref[GDN_W:, :], preferred_element_type=F32))
    h = _layer_norm(DN_ALPHA * x_ref[...] + mix, g_ref[...], b_ref[...])
    h_ref[...] = h
    logits = jnp.dot(h, rw_ref[...], precision=HI, preferred_element_type=F32) + rb_ref[...]
    tm = logits.shape[0]
    lane = lax.broadcasted_iota(jnp.int32, (tm, LANES), 1)
    lane_f = lane.astype(F32)
    work = logits
    vals, idxs = [], []
    for _ in range(TOP_K):
        m = jnp.max(work, -1, keepdims=True)
        sel = jnp.min(jnp.where(work == m, lane_f, float(LANES)), -1, keepdims=True)
        vals.append(m)
        idxs.append(sel)
        work = jnp.where(lane_f == sel, -jnp.inf, work)
    exps = [jnp.exp(vv - vals[0]) for vv in vals]
    inv = 1.0 / (exps[0] + exps[1] + exps[2] + exps[3])

    hot = jnp.zeros((tm, LANES), F32)
    for j in range(TOP_K):
        hot = hot + jnp.where(lane_f == idxs[j], 1.0, 0.0)
    ri = lax.broadcasted_iota(jnp.int32, (tm, tm), 0)
    ci = lax.broadcasted_iota(jnp.int32, (tm, tm), 1)
    before = jnp.where(ci < ri, 1.0, 0.0).astype(BF16)
    prefix = jnp.dot(before, hot.astype(BF16), preferred_element_type=F32) + cnt_ref[...]
    cnt_ref[...] = cnt_ref[...] + jnp.sum(hot, 0, keepdims=True)
    cnt_out_ref[...] = cnt_ref[...].astype(jnp.int32)

    idx_out = jnp.zeros((tm, LANES), F32)
    gate_out = jnp.zeros((tm, LANES), F32)
    for j in range(TOP_K):
        rank = jnp.sum(jnp.where(lane_f == idxs[j], prefix, 0.0), -1, keepdims=True)
        idx_out = jnp.where(lane == j, idxs[j], idx_out)
        idx_out = jnp.where(lane == TOP_K + j, rank, idx_out)
        gate_out = jnp.where(lane == j, exps[j] * inv, gate_out)
    idx_ref[...] = idx_out.astype(jnp.int32)
    gate_ref[...] = gate_out


def _outproj(og, od, x2d, w_out_b, ln_g, ln_b, rw_pad, rb_pad):
    n = x2d.shape[0]
    tm = TM_PROJ
    row = lambda w: pl.BlockSpec((tm, w), lambda i: (i, 0))
    full = lambda r, c: pl.BlockSpec((r, c), lambda i: (0, 0))
    return pl.pallas_call(
        _outproj_kernel,
        grid=(n // tm,),
        in_specs=[row(GDN_W), row(DIFF_W), row(D_MODEL), full(GDN_W + DIFF_W, D_MODEL),
                  full(1, D_MODEL), full(1, D_MODEL), full(D_MODEL, LANES), full(1, LANES)],
        out_specs=[row(D_MODEL), row(LANES), row(LANES), full(1, LANES)],
        out_shape=[jax.ShapeDtypeStruct((n, D_MODEL), F32),
                   jax.ShapeDtypeStruct((n, LANES), jnp.int32),
                   jax.ShapeDtypeStruct((n, LANES), F32),
                   jax.ShapeDtypeStruct((1, LANES), jnp.int32)],
        scratch_shapes=[pltpu.VMEM((1, LANES), F32)],
        compiler_params=pltpu.CompilerParams(dimension_semantics=("arbitrary",),
                                             vmem_limit_bytes=VMEM_LIMIT),
        name="outproj_ln1_router",
    )(og, od, x2d, w_out_b, ln_g.reshape(1, -1), ln_b.reshape(1, -1), rw_pad, rb_pad)


def _dispatch_kernel(pos_ref, h_ref, xs_hbm, sem):
    tc = h_ref.shape[0]
    base = pl.program_id(0) * (tc * TOP_K)

    def body(r, carry):
        for j in range(TOP_K):
            s = pos_ref[base + r * TOP_K + j]
            pltpu.make_async_copy(h_ref.at[pl.ds(r, 1), :], xs_hbm.at[pl.ds(s, 1), :], sem).start()
        return carry

    lax.fori_loop(0, tc, body, 0, unroll=4)
    for j in range(TOP_K):
        pltpu.make_async_copy(h_ref, xs_hbm.at[pl.ds(0, tc), :], sem).wait()


def _dispatch(pos, h1):
    n = h1.shape[0]
    tc = TC_DISP
    grid_spec = pltpu.PrefetchScalarGridSpec(
        num_scalar_prefetch=1,
        grid=(n // tc,),
        in_specs=[pl.BlockSpec((tc, D_MODEL), lambda i, pos: (i, 0))],
        out_specs=pl.BlockSpec(memory_space=pl.ANY),
        scratch_shapes=[pltpu.SemaphoreType.DMA],
    )
    return pl.pallas_call(
        _dispatch_kernel,
        grid_spec=grid_spec,
        out_shape=jax.ShapeDtypeStruct((n * TOP_K, D_MODEL), F32),
        compiler_params=pltpu.CompilerParams(dimension_semantics=("arbitrary",),
                                             vmem_limit_bytes=VMEM_LIMIT),
        name="dispatch",
    )(pos, h1)


def _moe_kernel(we_ref, wt_ref, lo_ref, hi_ref, nw_ref, xs_ref, wgu_ref, bgu_ref, wd_ref, bd_ref, ys_ref,
                wgu_b, wd_b):
    tm = ys_ref.shape[0]
    w = pl.program_id(0)

    @pl.when(w < nw_ref[0])
    def _():
        prev = jnp.maximum(w - 1, 0)

        @pl.when((w == 0) | (we_ref[w] != we_ref[prev]))
        def _():
            rows = 128

            def cast(j, carry):
                r0 = pl.multiple_of(j * rows, rows)
                wgu_b[pl.ds(r0, rows), :] = wgu_ref[0, pl.ds(r0, rows), :].astype(BF16)
                wd_b[pl.ds(r0, rows), :] = wd_ref[0, pl.ds(r0, rows), :].astype(BF16)
                return carry

            lax.fori_loop(0, D_MODEL // rows, cast, 0)

        xb = xs_ref[...].astype(BF16)
        hgu = jnp.dot(xb, wgu_b[...], preferred_element_type=F32) + bgu_ref[0]
        gate = jnp.minimum(hgu[:, :D_FF], SWIGLU_LIMIT)
        up = jnp.clip(hgu[:, D_FF:], -SWIGLU_LIMIT, SWIGLU_LIMIT)
        act = (up + 1.0) * gate * _sigmoid(SWIGLU_ALPHA * gate)
        y = jnp.dot(act.astype(BF16), wd_b[...], preferred_element_type=F32) + bd_ref[0]
        rid = lax.broadcasted_iota(jnp.int32, (tm, 1), 0)
        mine = jnp.where(rid >= lo_ref[w], jnp.where(rid < hi_ref[w], 1.0, 0.0), 0.0) > 0.5
        first = (w == 0) | (wt_ref[w] != wt_ref[prev])

        @pl.when(first)
        def _():
            ys_ref[...] = jnp.where(mine, y, 0.0)

        @pl.when(jnp.logical_not(first))
        def _():
            ys_ref[...] = jnp.where(mine, y, ys_ref[...])


def _moe(work, xs, w_gu, b_gu, w_down, b_down):
    tm = TM_MOE
    m = xs.shape[0]
    n_items = m // tm + N_EXPERTS - 1
    widx = lambda f: (lambda w, we, wt, lo, hi, nw: f(w, we, wt))
    grid_spec = pltpu.PrefetchScalarGridSpec(
        num_scalar_prefetch=5,
        grid=(n_items,),
        in_specs=[pl.BlockSpec((tm, D_MODEL), widx(lambda w, we, wt: (wt[w], 0))),
                  pl.BlockSpec((1, D_MODEL, 2 * D_FF), widx(lambda w, we, wt: (we[w], 0, 0))),
                  pl.BlockSpec((1, 1, 2 * D_FF), widx(lambda w, we, wt: (we[w], 0, 0))),
                  pl.BlockSpec((1, D_FF, D_MODEL), widx(lambda w, we, wt: (we[w], 0, 0))),
                  pl.BlockSpec((1, 1, D_MODEL), widx(lambda w, we, wt: (we[w], 0, 0)))],
        out_specs=pl.BlockSpec((tm, D_MODEL), widx(lambda w, we, wt: (wt[w], 0))),
        scratch_shapes=[pltpu.VMEM((D_MODEL, 2 * D_FF), BF16),
                        pltpu.VMEM((D_FF, D_MODEL), BF16)],
    )
    return pl.pallas_call(
        _moe_kernel,
        grid_spec=grid_spec,
        out_shape=jax.ShapeDtypeStruct((m, D_MODEL), F32),
        compiler_params=pltpu.CompilerParams(dimension_semantics=("arbitrary",),
                                             vmem_limit_bytes=VMEM_LIMIT),
        name="moe_ffn",
    )(*work, xs, w_gu, b_gu.reshape(N_EXPERTS, 1, -1), w_down, b_down.reshape(N_EXPERTS, 1, -1))


def _combine_kernel(pos_ref, h_ref, gate_ref, p_ref, ys_hbm, g2_ref, b2_ref, wg_ref, bg_ref, wp_ref,
                    g3_ref, b3_ref, o_ref, ybuf, sem):
    tc = h_ref.shape[0]
    i = pl.program_id(0)
    n = pl.num_programs(0)

    def issue(blk, buf):
        base = blk * (TOP_K * tc)

        def body(r, carry):
            for j in range(TOP_K):
                s = pos_ref[base + r * TOP_K + j]
                pltpu.make_async_copy(ys_hbm.at[pl.ds(s, 1), :], ybuf.at[buf, j, pl.ds(r, 1), :],
                                      sem.at[buf]).start()
            return carry

        lax.fori_loop(0, tc, body, 0, unroll=4)

    @pl.when(i == 0)
    def _():
        issue(0, 0)

    @pl.when(i + 1 < n)
    def _():
        issue(i + 1, (i + 1) % 2)

    buf = i % 2
    for j in range(TOP_K):
        pltpu.make_async_copy(ys_hbm.at[pl.ds(0, tc), :], ybuf.at[buf, j], sem.at[buf]).wait()
    gates = gate_ref[...]
    ffn = ybuf[buf, 0] * gates[:, 0:1]
    for j in range(1, TOP_K):
        ffn = ffn + ybuf[buf, j] * gates[:, j:j + 1]
    h2 = _layer_norm(DN_ALPHA * h_ref[...] + ffn, g2_ref[...], b2_ref[...])
    gate = _sigmoid(jnp.dot(h2.astype(BF16), wg_ref[...], preferred_element_type=F32) + bg_ref[...])
    ple = gate * jnp.dot(p_ref[...].astype(BF16), wp_ref[...], preferred_element_type=F32)
    o_ref[...] = _layer_norm(DN_ALPHA * h2 + ple, g3_ref[...], b3_ref[...])


def _combine(pos, h1, gates, p2d, ys, g2, b2, wg_b, bg, wp_b, g3, b3):
    n = h1.shape[0]
    tc = TC_COMB
    row = lambda w: pl.BlockSpec((tc, w), lambda i, pos: (i, 0))
    full = lambda r, c: pl.BlockSpec((r, c), lambda i, pos: (0, 0))
    grid_spec = pltpu.PrefetchScalarGridSpec(
        num_scalar_prefetch=1,
        grid=(n // tc,),
        in_specs=[row(D_MODEL), row(LANES), row(PLE_DIM), pl.BlockSpec(memory_space=pl.ANY),
                  full(1, D_MODEL), full(1, D_MODEL), full(D_MODEL, D_MODEL), full(1, D_MODEL),
                  full(PLE_DIM, D_MODEL), full(1, D_MODEL), full(1, D_MODEL)],
        out_specs=row(D_MODEL),
        scratch_shapes=[pltpu.VMEM((2, TOP_K, tc, D_MODEL), F32),
                        pltpu.SemaphoreType.DMA((2,))],
    )
    return pl.pallas_call(
        _combine_kernel,
        grid_spec=grid_spec,
        out_shape=jax.ShapeDtypeStruct((n, D_MODEL), F32),
        compiler_params=pltpu.CompilerParams(dimension_semantics=("arbitrary",),
                                             vmem_limit_bytes=VMEM_LIMIT),
        name="combine_ln2_ple_ln3",
    )(pos, h1, gates, p2d, ys, g2.reshape(1, -1), b2.reshape(1, -1), wg_b, bg.reshape(1, -1), wp_b,
      g3.reshape(1, -1), b3.reshape(1, -1))


def _routing(idx, rank, counts, tm):
    n_tok = idx.shape[0]
    m = n_tok * TOP_K
    n_tiles = m // tm
    n_items = n_tiles + N_EXPERTS - 1
    grp_end = jnp.cumsum(counts)
    grp_start = grp_end - counts
    experts = jnp.arange(N_EXPERTS, dtype=jnp.int32)
    pos = (jnp.sum(jnp.where(idx[:, :, None] == experts, grp_start, 0), -1) + rank).reshape(-1).astype(jnp.int32)
    t_first = grp_start // tm
    t_last = (grp_end - 1) // tm
    n_e = jnp.where(counts > 0, t_last - t_first + 1, 0)
    w_end = jnp.cumsum(n_e)
    w_start = w_end - n_e
    n_work = w_end[-1]
    w = jnp.arange(n_items, dtype=jnp.int32)
    valid = w < n_work
    wq = jnp.minimum(w, n_work - 1)
    e_w = jnp.clip(jnp.sum((w_end[None, :] <= wq[:, None]).astype(jnp.int32), -1), 0, N_EXPERTS - 1)
    tile_w = t_first[e_w] + (wq - w_start[e_w])
    lo = jnp.clip(grp_start[e_w] - tile_w * tm, 0, tm)
    hi = jnp.clip(grp_end[e_w] - tile_w * tm, 0, tm)
    lo = jnp.where(valid, lo, 0)
    hi = jnp.where(valid, hi, 0)
    i32 = lambda a: a.astype(jnp.int32)
    return pos, (i32(e_w), i32(tile_w), i32(lo), i32(hi), i32(n_work).reshape(1))


def _layer(h, p_i, cos8, sin8, lam_init, w_in, conv_w, a_log, dt_bias, gdn_norm_w,
           lam_q1, lam_k1, lam_q2, lam_k2, diff_norm_w, w_out, ln1_g, ln1_b,
           router_w, router_b, w_gu, b_gu, w_down, b_down, ln2_g, ln2_b,
           ple_w, ple_gate_w, ple_gate_b, ln3_g, ln3_b):
    batch, seq, d = h.shape
    n = batch * seq
    x2d = h.reshape(n, d)
    o_ba = 4 * GDN_W
    o_d = o_ba + 2 * GDN_HEADS
    w_r = jnp.concatenate([w_in[:, :o_ba], w_in[:, o_d:], w_in[:, o_ba:o_d],
                           jnp.zeros((d, LANES - 2 * GDN_HEADS), w_in.dtype)], -1).astype(BF16)
    ones = jnp.ones((n, DIFF_D - ROPE_DIM), F32)
    zeros = jnp.zeros((n, DIFF_D - ROPE_DIM), F32)
    z8 = jnp.zeros_like(sin8)
    ctab = jnp.tile(jnp.concatenate([cos8, cos8, ones], -1), (1, 2))
    satab = jnp.tile(jnp.concatenate([z8, sin8, zeros], -1), (1, 2))
    sbtab = jnp.tile(jnp.concatenate([-sin8, z8, zeros], -1), (1, 2))
    a_proj, ba_proj, dq, dk, dv = _inproj(x2d, w_r, ctab, satab, sbtab)
    o_gdn = _gdn(a_proj, ba_proj, conv_w, a_log, dt_bias, gdn_norm_w, batch, seq)
    o_diff = _diff_attention(dq, dk, dv, lam_q1, lam_k1, lam_q2, lam_k2, diff_norm_w, lam_init, batch, seq)
    rw_pad = jnp.concatenate([router_w, jnp.zeros((d, LANES - N_EXPERTS), F32)], -1)
    rb_pad = jnp.concatenate([router_b, jnp.full((LANES - N_EXPERTS,), -jnp.inf, F32)]).reshape(1, LANES)
    h1, idx, gates, cnt = _outproj(o_gdn, o_diff, x2d, w_out.astype(BF16), ln1_g, ln1_b, rw_pad, rb_pad)
    pos, work = _routing(idx[:, :TOP_K], idx[:, TOP_K:2 * TOP_K], cnt[0, :N_EXPERTS], TM_MOE)
    xs = _dispatch(pos, h1)
    ys = _moe(work, xs, w_gu, b_gu, w_down, b_down)
    out = _combine(pos, h1, gates, p_i.reshape(n, PLE_DIM), ys, ln2_g, ln2_b,
                   ple_gate_w.astype(BF16), ple_gate_b, ple_w.astype(BF16), ln3_g, ln3_b)
    return out.reshape(batch, seq, d)


def kernel(x, p, positions, w_in, conv_w, a_log, dt_bias, gdn_norm_w, lam_q1, lam_k1, lam_q2, lam_k2,
           diff_norm_w, w_out, ln1_g, ln1_b, router_w, router_b, w_gu, b_gu, w_down, b_down, ln2_g, ln2_b,
           ple_w, ple_gate_w, ple_gate_b, ln3_g, ln3_b):
    batch, seq, _ = x.shape
    inv_freq = ROPE_THETA ** (-jnp.arange(0, ROPE_DIM, 2, dtype=F32) / ROPE_DIM)
    ang = (positions.astype(F32)[..., None] * inv_freq).reshape(batch * seq, ROPE_DIM // 2)
    cos8 = jnp.cos(ang)
    sin8 = jnp.sin(ang)
    h = x
    for i in range(w_in.shape[0]):
        lam_init = 0.8 - 0.6 * math.exp(-0.3 * i)
        h = _layer(h, p[i], cos8, sin8, lam_init, w_in[i], conv_w[i], a_log[i], dt_bias[i], gdn_norm_w[i],
                   lam_q1[i], lam_k1[i], lam_q2[i], lam_k2[i], diff_norm_w[i], w_out[i], ln1_g[i], ln1_b[i],
                   router_w[i], router_b[i], w_gu[i], b_gu[i], w_down[i], b_down[i], ln2_g[i], ln2_b[i],
                   ple_w[i], ple_gate_w[i], ple_gate_b[i], ln3_g[i], ln3_b[i])
    return h
```

```python
import functools
import math

import jax
import jax.numpy as jnp
from jax import lax
from jax.experimental import pallas as pl
from jax.experimental.pallas import tpu as pltpu

F32 = jnp.float32
BF16 = jnp.bfloat16
HI = lax.Precision.HIGHEST
LOG2E = 1.4426950408889634

D_MODEL = 1024
PLE_DIM = 256
GDN_HEADS = 4
GDN_DK = 128
GDN_DV = 128
CONV_WIDTH = 4
CHUNK = 64
DIFF_HEADS = 4
DIFF_D = 64
DIFF_DV = 2 * DIFF_D
ROPE_THETA = 500000.0
ROPE_DIM = DIFF_D // 4
N_EXPERTS = 32
TOP_K = 4
D_FF = D_MODEL
SWIGLU_LIMIT = 7.0
SWIGLU_ALPHA = 1.702
DEPTH = 1
DN_ALPHA = (2 * DEPTH) ** 0.25
LN_EPS = 1e-5
RMS_EPS = 1e-6

LANES = 128
SUBLANES = 8
GDN_W = GDN_HEADS * GDN_DK
CONV_CH = 3 * GDN_W
A_COLS = 4 * GDN_W
DIFF_W = DIFF_HEADS * DIFF_DV
IN_PAD_W = A_COLS + 3 * DIFF_W + LANES
HSTACK = GDN_HEADS * CHUNK
SUBBLK = 16

VMEM_LIMIT = 56 * 1024 * 1024

TM_PROJ = 512
T_GDN = 256
TQ = 256
TKV = 256
TM_MOE = 256
TC_DISP = 512
TC_COMB = 256
COMB_CH = 32
COMB_MAXCH = 72
assert COMB_MAXCH >= (TC_COMB * TOP_K + N_EXPERTS * (SUBLANES - 1 + COMB_CH - 1)) // COMB_CH
assert (COMB_MAXCH * COMB_CH) % LANES == 0


def _layer_norm(y, g, b):
    mu = jnp.mean(y, -1, keepdims=True)
    d = y - mu
    var = jnp.mean(d * d, -1, keepdims=True)
    return d * lax.rsqrt(var + LN_EPS) * g + b


def _sigmoid(x):
    return 1.0 / (1.0 + jnp.exp(-x))


def _inproj_kernel(x_ref, w_ref, c_ref, sa_ref, sb_ref, a_ref, ba_ref, q_ref, k_ref, v_ref):
    xb = x_ref[...].astype(BF16)
    a_ref[...] = jnp.dot(xb, w_ref[:, :A_COLS], preferred_element_type=F32)
    ba_ref[...] = jnp.dot(xb, w_ref[:, A_COLS + 3 * DIFF_W:], preferred_element_type=F32)
    c = c_ref[...]
    sa = sa_ref[...]
    sb = sb_ref[...]

    def rot(t):
        return t * c + pltpu.roll(t, 8, 1) * sa + pltpu.roll(t, LANES - 8, 1) * sb

    for h in range(DIFF_HEADS):
        lo = A_COLS + LANES * h
        q = jnp.dot(xb, w_ref[:, lo:lo + LANES], preferred_element_type=F32)
        q_ref[:, LANES * h:LANES * (h + 1)] = (rot(q) * (DIFF_D ** -0.5 * LOG2E)).astype(BF16)
        lo = A_COLS + DIFF_W + LANES * h
        k = jnp.dot(xb, w_ref[:, lo:lo + LANES], preferred_element_type=F32)
        k_ref[:, LANES * h:LANES * (h + 1)] = rot(k).astype(BF16)
    lo = A_COLS + 2 * DIFF_W
    v_ref[...] = jnp.dot(xb, w_ref[:, lo:lo + DIFF_W], preferred_element_type=F32).astype(BF16)


def _inproj(x2d, w_r, ctab, satab, sbtab):
    n = x2d.shape[0]
    tm = TM_PROJ
    row = lambda w: pl.BlockSpec((tm, w), lambda i: (i, 0))
    return pl.pallas_call(
        _inproj_kernel,
        grid=(n // tm,),
        in_specs=[row(D_MODEL),
                  pl.BlockSpec((D_MODEL, IN_PAD_W), lambda i: (0, 0)),
                  row(LANES), row(LANES), row(LANES)],
        out_specs=[row(A_COLS), row(LANES), row(DIFF_W), row(DIFF_W), row(DIFF_W)],
        out_shape=[jax.ShapeDtypeStruct((n, A_COLS), F32),
                   jax.ShapeDtypeStruct((n, LANES), F32),
                   jax.ShapeDtypeStruct((n, DIFF_W), BF16),
                   jax.ShapeDtypeStruct((n, DIFF_W), BF16),
                   jax.ShapeDtypeStruct((n, DIFF_W), BF16)],
        compiler_params=pltpu.CompilerParams(dimension_semantics=("arbitrary",),
                                             vmem_limit_bytes=VMEM_LIMIT),
        name="inproj",
    )(x2d, w_r, ctab, satab, sbtab)


def _mm(a, b):
    return jnp.dot(a.astype(BF16), b.astype(BF16), preferred_element_type=F32)


def _gdn_kernel(a_ref, ba_ref, cw_ref, aux_ref, nw_ref, o_ref, xe_ref, q_s, k_s, v_s, state_ref):
    t_rows = a_ref.shape[0]
    st = pl.program_id(1)

    @pl.when(st == 0)
    def _():
        xe_ref[0:8, :] = jnp.zeros((8, CONV_CH), F32)
        state_ref[...] = jnp.zeros(state_ref.shape, F32)

    xe_ref[8:8 + t_rows, :] = a_ref[:, :CONV_CH]
    for s in range(CONV_CH // LANES):
        cs = slice(LANES * s, LANES * (s + 1))
        y = jnp.zeros((t_rows, LANES), F32)
        for j in range(CONV_WIDTH):
            off = 8 - (CONV_WIDTH - 1) + j
            y = y + xe_ref[off:off + t_rows, cs] * cw_ref[j:j + 1, cs]
        y = y * _sigmoid(y)
        grp, h = divmod(s, GDN_HEADS)
        hs = slice(LANES * h, LANES * (h + 1))
        if grp == 0:
            q_s[:, hs] = y * lax.rsqrt(jnp.sum(y * y, -1, keepdims=True) + 1e-6) * (GDN_DK ** -0.5)
        elif grp == 1:
            k_s[:, hs] = y * lax.rsqrt(jnp.sum(y * y, -1, keepdims=True) + 1e-6)
        else:
            v_s[:, hs] = y
    xe_ref[0:8, :] = xe_ref[t_rows:t_rows + 8, :]

    ba = ba_ref[...]
    beta_t = _sigmoid(ba)
    gx = ba + aux_ref[1:2, :]
    g_t = -jnp.exp(aux_ref[0:1, :]) * (jnp.maximum(gx, 0.0) + jnp.log(1.0 + jnp.exp(-jnp.abs(gx))))

    ri = lax.broadcasted_iota(jnp.int32, (HSTACK, HSTACK), 0)
    ci = lax.broadcasted_iota(jnp.int32, (HSTACK, HSTACK), 1)
    head_start = ri - (ri & (CHUNK - 1))
    in_head = ci >= head_start
    incl_f = jnp.where(in_head, jnp.where(ci <= ri, 1.0, 0.0), 0.0)
    strict_f = jnp.where(in_head, jnp.where(ci < ri, 1.0, 0.0), 0.0)
    sub_f = jnp.where(ci >= ri - (ri & (SUBBLK - 1)), 1.0, 0.0)
    eye = jnp.where(ri == ci, 1.0, 0.0)
    tri_b = (lax.broadcasted_iota(jnp.int32, (CHUNK, CHUNK), 0)
             >= lax.broadcasted_iota(jnp.int32, (CHUNK, CHUNK), 1)).astype(BF16)
    nt = (((1,), (1,)), ((), ()))
    tn = (((0,), (0,)), ((), ()))
    bdot = functools.partial(jnp.dot, preferred_element_type=F32)

    def stack(fn):
        return jnp.concatenate([fn(h) for h in range(GDN_HEADS)], axis=0)

    chunks = range(t_rows // CHUNK)
    crow = [slice(CHUNK * c, CHUNK * (c + 1)) for c in chunks]

    def cumdecay(g):
        g1 = g.astype(BF16)
        r1 = g - g1.astype(F32)
        g2 = r1.astype(BF16)
        g3 = (r1 - g2.astype(F32)).astype(BF16)
        return bdot(tri_b, g1) + bdot(tri_b, g2) + bdot(tri_b, g3)

    gc_l = [cumdecay(g_t[crow[c]]) for c in chunks]
    gct_l = [gc_l[c].T for c in chunks]
    gcol_l = [stack(lambda h: jnp.broadcast_to(gc_l[c][:, GDN_HEADS + h:GDN_HEADS + h + 1], (CHUNK, LANES)))
              for c in chunks]
    glast_l = [stack(lambda h: jnp.broadcast_to(gc_l[c][CHUNK - 1:CHUNK, GDN_HEADS + h:GDN_HEADS + h + 1],
                                                (CHUNK, LANES))) for c in chunks]
    bcol_l = [stack(lambda h: jnp.broadcast_to(beta_t[crow[c], h:h + 1], (CHUNK, LANES))) for c in chunks]
    grow_l = [jnp.concatenate([gct_l[c][GDN_HEADS + h:GDN_HEADS + h + 1, :] for h in range(GDN_HEADS)], axis=1)
              for c in chunks]
    kk_l = [stack(lambda h: k_s[crow[c], LANES * h:LANES * (h + 1)]) for c in chunks]
    qq_l = [stack(lambda h: q_s[crow[c], LANES * h:LANES * (h + 1)]) for c in chunks]
    vv_l = [stack(lambda h: v_s[crow[c], LANES * h:LANES * (h + 1)]) for c in chunks]
    dec_l = [jnp.exp(jnp.minimum(jnp.concatenate([gcol_l[c], gcol_l[c]], axis=1) - grow_l[c], 0.0)) * incl_f
             for c in chunks]
    kb_l = [kk_l[c] * bcol_l[c] for c in chunks]
    k16_l = [kk_l[c].astype(BF16) for c in chunks]
    amat_l = [lax.dot_general(kb_l[c].astype(BF16), k16_l[c], nt, preferred_element_type=F32)
              * dec_l[c] * strict_f for c in chunks]
    bm_l = [amat_l[c] * sub_f for c in chunks]
    nm_l = [amat_l[c] - bm_l[c] for c in chunks]
    b2_l = [_mm(bm_l[c], bm_l[c]) for c in chunks]
    b4_l = [_mm(b2_l[c], b2_l[c]) for c in chunks]
    b8_l = [_mm(b4_l[c], b4_l[c]) for c in chunks]
    d_l = [_mm(eye - bm_l[c], eye + b2_l[c]) for c in chunks]
    d_l = [_mm(d_l[c], eye + b4_l[c]) for c in chunks]
    dinv_l = [_mm(d_l[c], eye + b8_l[c]) for c in chunks]
    mm_l = [_mm(dinv_l[c], nm_l[c]) for c in chunks]
    m2_l = [_mm(mm_l[c], mm_l[c]) for c in chunks]
    t_l = [_mm(eye - mm_l[c], eye + m2_l[c]) for c in chunks]
    tinv_l = [_mm(t_l[c], dinv_l[c]) for c in chunks]
    eg_l = [jnp.exp(gcol_l[c]) for c in chunks]
    sol_l = [_mm(tinv_l[c], jnp.concatenate([vv_l[c] * bcol_l[c], kb_l[c] * eg_l[c]], axis=1)) for c in chunks]
    qk_l = [lax.dot_general(qq_l[c].astype(BF16), k16_l[c], nt, preferred_element_type=F32) * dec_l[c]
            for c in chunks]
    qd_l = [qq_l[c] * eg_l[c] for c in chunks]
    kd_l = [kk_l[c] * jnp.exp(glast_l[c] - gcol_l[c]) for c in chunks]

    for c in chunks:
        rows = crow[c]
        gc = gc_l[c]
        u = sol_l[c][:, :LANES]
        w = sol_l[c][:, LANES:]
        qk, qd, kd = qk_l[c], qd_l[c], kd_l[c]

        ws, qs = [], []
        for h in range(GDN_HEADS):
            hr = slice(CHUNK * h, CHUNK * (h + 1))
            lhs = jnp.concatenate([w[hr], qd[hr]], axis=0).astype(BF16)
            r = bdot(lhs, state_ref[h].astype(BF16))
            ws.append(r[:CHUNK])
            qs.append(r[CHUNK:])
        vn = u - jnp.concatenate(ws, axis=0)
        vn16 = vn.astype(BF16)
        o = jnp.concatenate(qs, axis=0) + bdot(qk.astype(BF16), vn16)
        kd16 = kd.astype(BF16)
        for h in range(GDN_HEADS):
            hr = slice(CHUNK * h, CHUNK * (h + 1))
            hs = slice(LANES * h, LANES * (h + 1))
            gl = jnp.exp(gc[CHUNK - 1:CHUNK, GDN_HEADS + h:GDN_HEADS + h + 1])
            state_ref[h] = state_ref[h] * gl + lax.dot_general(kd16[hr], vn16[hr], tn,
                                                               preferred_element_type=F32)
            oh = o[hr]
            z = a_ref[rows, CONV_CH + LANES * h:CONV_CH + LANES * (h + 1)]
            oh = oh * lax.rsqrt(jnp.mean(oh * oh, -1, keepdims=True) + RMS_EPS) * nw_ref[...]
            o_ref[rows, hs] = (oh * (z * _sigmoid(z))).astype(o_ref.dtype)


def _gdn(a_proj, ba_proj, conv_w, a_log, dt_bias, norm_w, batch, seq):
    t = T_GDN
    nst = seq // t
    rowblk = lambda w: pl.BlockSpec((t, w), lambda b, s: (b * nst + s, 0))
    aux = jnp.zeros((8, LANES), F32)
    aux = aux.at[0, GDN_HEADS:2 * GDN_HEADS].set(a_log).at[1, GDN_HEADS:2 * GDN_HEADS].set(dt_bias)
    return pl.pallas_call(
        _gdn_kernel,
        grid=(batch, nst),
        in_specs=[rowblk(A_COLS), rowblk(LANES),
                  pl.BlockSpec((CONV_WIDTH, CONV_CH), lambda b, s: (0, 0)),
                  pl.BlockSpec((8, LANES), lambda b, s: (0, 0)),
                  pl.BlockSpec((1, GDN_DV), lambda b, s: (0, 0))],
        out_specs=rowblk(GDN_W),
        out_shape=jax.ShapeDtypeStruct((batch * seq, GDN_W), BF16),
        scratch_shapes=[pltpu.VMEM((t + 8, CONV_CH), F32),
                        pltpu.VMEM((t, GDN_W), F32),
                        pltpu.VMEM((t, GDN_W), F32),
                        pltpu.VMEM((t, GDN_W), F32),
                        pltpu.VMEM((GDN_HEADS, GDN_DK, GDN_DV), F32)],
        compiler_params=pltpu.CompilerParams(dimension_semantics=("arbitrary", "arbitrary"),
                                             vmem_limit_bytes=VMEM_LIMIT),
        name="gdn",
    )(a_proj, ba_proj, conv_w, aux, norm_w.reshape(1, GDN_DV))


def _attn_kernel(q_ref, k_ref, vt_ref, lq1_ref, lk1_ref, lq2_ref, lk2_ref, nw_ref, o_ref, *, lam_init):
    tq = q_ref.shape[0]
    qi = pl.program_id(1)
    lane = lax.broadcasted_iota(jnp.int32, (1, LANES), 1)
    qpos = qi * tq + (lax.broadcasted_iota(jnp.int32, (1, 2 * tq), 1) & (tq - 1))
    nt = (((1,), (1,)), ((), ()))
    q2 = []
    for h in range(DIFF_HEADS):
        q = q_ref[:, LANES * h:LANES * (h + 1)]
        zero = jnp.zeros_like(q)
        q2.append(jnp.concatenate([jnp.where(lane < DIFF_D, q, zero), jnp.where(lane >= DIFF_D, q, zero)], axis=0))

    def make_body(masked):
        def body(kj, carry):
            off = pl.multiple_of(kj * TKV, TKV)
            if masked:
                keep = off + lax.broadcasted_iota(jnp.int32, (TKV, 1), 0) <= qpos
            heads = range(DIFF_HEADS)
            ss = [lax.dot_general(k_ref[pl.ds(off, TKV), LANES * h:LANES * (h + 1)], q2[h], nt,
                                  preferred_element_type=F32) for h in heads]
            if masked:
                ss = [jnp.where(keep, s, -1e30) for s in ss]
            mns = [jnp.maximum(carry[h][0], jnp.max(ss[h], 0, keepdims=True)) for h in heads]
            ps = [jnp.exp2(ss[h] - mns[h]) for h in heads]
            als = [jnp.exp2(carry[h][0] - mns[h]) for h in heads]
            ls = [als[h] * carry[h][1] + jnp.sum(ps[h], 0, keepdims=True) for h in heads]
            pvs = [jnp.dot(vt_ref[LANES * h:LANES * (h + 1), pl.ds(off, TKV)], ps[h].astype(BF16),
                           preferred_element_type=F32) for h in heads]
            return tuple((mns[h], ls[h], als[h] * carry[h][2] + pvs[h]) for h in heads)
        return body

    init = tuple((jnp.full((1, 2 * tq), -1e30, F32), jnp.zeros((1, 2 * tq), F32),
                  jnp.zeros((DIFF_DV, 2 * tq), F32)) for _ in range(DIFF_HEADS))
    carry = lax.fori_loop(0, qi, make_body(False), init)
    carry = make_body(True)(qi, carry)
    lam = (jnp.exp(jnp.sum(lq1_ref[...] * lk1_ref[...], -1, keepdims=True))
           - jnp.exp(jnp.sum(lq2_ref[...] * lk2_ref[...], -1, keepdims=True)) + lam_init)
    for h in range(DIFF_HEADS):
        m, l, acc = carry[h]
        on = acc * (1.0 / l)
        o = (on[:, :tq] - lam * on[:, tq:]).T
        o = o * lax.rsqrt(jnp.mean(o * o, -1, keepdims=True) + RMS_EPS) * nw_ref[...] * (1.0 - lam_init)
        o_ref[:, LANES * h:LANES * (h + 1)] = o.astype(o_ref.dtype)


def _diff_attention(q, k, v, lq1, lk1, lq2, lk2, norm_w, lam_init, batch, seq):
    q3 = q.reshape(batch, seq, DIFF_W)
    k3 = k.reshape(batch, seq, DIFF_W)
    v3 = v.reshape(batch, seq, DIFF_W).transpose(0, 2, 1)
    small = lambda w: pl.BlockSpec((1, w), lambda b, i: (0, 0))
    out = pl.pallas_call(
        functools.partial(_attn_kernel, lam_init=lam_init),
        grid=(batch, seq // TQ),
        in_specs=[pl.BlockSpec((None, TQ, DIFF_W), lambda b, i: (b, i, 0)),
                  pl.BlockSpec((None, seq, DIFF_W), lambda b, i: (b, 0, 0)),
                  pl.BlockSpec((None, DIFF_W, seq), lambda b, i: (b, 0, 0)),
                  small(DIFF_D), small(DIFF_D), small(DIFF_D), small(DIFF_D), small(DIFF_DV)],
        out_specs=pl.BlockSpec((None, TQ, DIFF_W), lambda b, i: (b, i, 0)),
        out_shape=jax.ShapeDtypeStruct((batch, seq, DIFF_W), BF16),
        compiler_params=pltpu.CompilerParams(
            dimension_semantics=("arbitrary", "arbitrary"),
            vmem_limit_bytes=VMEM_LIMIT),
        name="diff_attn",
    )(q3, k3, v3, lq1.reshape(1, -1), lk1.reshape(1, -1), lq2.reshape(1, -1), lk2.reshape(1, -1),
      norm_w.reshape(1, -1))
    return out.reshape(batch * seq, DIFF_W)


def _outproj_kernel(og_ref, od_ref, x_ref, wo_ref, g_ref, b_ref, rw_ref, rb_ref,
                    h_ref, idx_ref, gate_ref, cnt_out_ref, tcnt_ref, cnt_ref):
    @pl.when(pl.program_id(0) == 0)
    def _():
        cnt_ref[...] = jnp.zeros(cnt_ref.shape, F32)

    mix = (jnp.dot(og_ref[...], wo_ref[:GDN_W, :], preferred_element_type=F32)
           + jnp.dot(od_ref[...], wo_ref[GDN_W:, :], preferred_element_type=F32))
    h = _layer_norm(DN_ALPHA * x_ref[...] + mix, g_ref[...], b_ref[...])
    h_ref[...] = h
    logits = jnp.dot(h, rw_ref[...], precision=HI, preferred_element_type=F32) + rb_ref[...]
    tm = logits.shape[0]
    lane = lax.broadcasted_iota(jnp.int32, (tm, LANES), 1)
    lane_f = lane.astype(F32)
    work = logits
    vals, idxs = [], []
    for _ in range(TOP_K):
        m = jnp.max(work, -1, keepdims=True)
        sel = jnp.min(jnp.where(work == m, lane_f, float(LANES)), -1, keepdims=True)
        vals.append(m)
        idxs.append(sel)
        work = jnp.where(lane_f == sel, -jnp.inf, work)
    exps = [jnp.exp(vv - vals[0]) for vv in vals]
    inv = 1.0 / (exps[0] + exps[1] + exps[2] + exps[3])

    hot = jnp.zeros((tm, LANES), F32)
    for j in range(TOP_K):
        hot = hot + jnp.where(lane_f == idxs[j], 1.0, 0.0)
    ri = lax.broadcasted_iota(jnp.int32, (tm, tm), 0)
    ci = lax.broadcasted_iota(jnp.int32, (tm, tm), 1)
    before = jnp.where(ci < ri, 1.0, 0.0).astype(BF16)
    prefix = jnp.dot(before, hot.astype(BF16), preferred_element_type=F32) + cnt_ref[...]
    cnt_ref[...] = cnt_ref[...] + jnp.sum(hot, 0, keepdims=True)
    cnt_out_ref[...] = cnt_ref[...].astype(jnp.int32)
    starts = [prefix[TC_COMB * j:TC_COMB * j + 1, :] for j in range(tm // TC_COMB)]
    pad = jnp.zeros((tcnt_ref.shape[0] - len(starts), LANES), F32)
    tcnt_ref[...] = jnp.concatenate(starts + [pad], axis=0).astype(jnp.int32)

    idx_out = jnp.zeros((tm, LANES), F32)
    gate_out = jnp.zeros((tm, LANES), F32)
    for j in range(TOP_K):
        rank = jnp.sum(jnp.where(lane_f == idxs[j], prefix, 0.0), -1, keepdims=True)
        idx_out = jnp.where(lane == j, idxs[j], idx_out)
        idx_out = jnp.where(lane == TOP_K + j, rank, idx_out)
        gate_out = jnp.where(lane == j, exps[j] * inv, gate_out)
    idx_ref[...] = idx_out.astype(jnp.int32)
    gate_ref[...] = gate_out


def _outproj(og, od, x2d, w_out_b, ln_g, ln_b, rw_pad, rb_pad):
    n = x2d.shape[0]
    tm = TM_PROJ
    row = lambda w: pl.BlockSpec((tm, w), lambda i: (i, 0))
    full = lambda r, c: pl.BlockSpec((r, c), lambda i: (0, 0))
    return pl.pallas_call(
        _outproj_kernel,
        grid=(n // tm,),
        in_specs=[row(GDN_W), row(DIFF_W), row(D_MODEL), full(GDN_W + DIFF_W, D_MODEL),
                  full(1, D_MODEL), full(1, D_MODEL), full(D_MODEL, LANES), full(1, LANES)],
        out_specs=[row(D_MODEL), row(LANES), row(LANES), full(1, LANES),
                   pl.BlockSpec((8, LANES), lambda i: (i, 0))],
        out_shape=[jax.ShapeDtypeStruct((n, D_MODEL), F32),
                   jax.ShapeDtypeStruct((n, LANES), jnp.int32),
                   jax.ShapeDtypeStruct((n, LANES), F32),
                   jax.ShapeDtypeStruct((1, LANES), jnp.int32),
                   jax.ShapeDtypeStruct((n // tm * 8, LANES), jnp.int32)],
        scratch_shapes=[pltpu.VMEM((1, LANES), F32)],
        compiler_params=pltpu.CompilerParams(dimension_semantics=("arbitrary",),
                                             vmem_limit_bytes=VMEM_LIMIT),
        name="outproj_ln1_router",
    )(og, od, x2d, w_out_b, ln_g.reshape(1, -1), ln_b.reshape(1, -1), rw_pad, rb_pad)


def _dispatch_kernel(pos_ref, h_ref, xs_hbm, sem):
    tc = h_ref.shape[0]
    base = pl.program_id(0) * (tc * TOP_K)

    def body(r, carry):
        for j in range(TOP_K):
            s = pos_ref[base + r * TOP_K + j]
            pltpu.make_async_copy(h_ref.at[pl.ds(r, 1), :], xs_hbm.at[pl.ds(s, 1), :], sem).start()
        return carry

    lax.fori_loop(0, tc, body, 0, unroll=4)
    for j in range(TOP_K):
        pltpu.make_async_copy(h_ref, xs_hbm.at[pl.ds(0, tc), :], sem).wait()


def _dispatch(pos, h1):
    n = h1.shape[0]
    tc = TC_DISP
    grid_spec = pltpu.PrefetchScalarGridSpec(
        num_scalar_prefetch=1,
        grid=(n // tc,),
        in_specs=[pl.BlockSpec((tc, D_MODEL), lambda i, pos: (i, 0))],
        out_specs=pl.BlockSpec(memory_space=pl.ANY),
        scratch_shapes=[pltpu.SemaphoreType.DMA],
    )
    return pl.pallas_call(
        _dispatch_kernel,
        grid_spec=grid_spec,
        out_shape=jax.ShapeDtypeStruct((n * TOP_K, D_MODEL), F32),
        compiler_params=pltpu.CompilerParams(dimension_semantics=("arbitrary",),
                                             vmem_limit_bytes=VMEM_LIMIT),
        name="dispatch",
    )(pos, h1)


def _moe_kernel(we_ref, wt_ref, lo_ref, hi_ref, nw_ref, xs_ref, wgu_ref, bgu_ref, wd_ref, bd_ref, ys_ref,
                wgu_b, wd_b):
    tm = ys_ref.shape[0]
    w = pl.program_id(0)

    @pl.when(w < nw_ref[0])
    def _():
        prev = jnp.maximum(w - 1, 0)

        @pl.when((w == 0) | (we_ref[w] != we_ref[prev]))
        def _():
            rows = 128

            def cast(j, carry):
                r0 = pl.multiple_of(j * rows, rows)
                wgu_b[pl.ds(r0, rows), :] = wgu_ref[0, pl.ds(r0, rows), :].astype(BF16)
                wd_b[pl.ds(r0, rows), :] = wd_ref[0, pl.ds(r0, rows), :].astype(BF16)
                return carry

            lax.fori_loop(0, D_MODEL // rows, cast, 0)

        xb = xs_ref[...].astype(BF16)
        hgu = jnp.dot(xb, wgu_b[...], preferred_element_type=F32) + bgu_ref[0]
        gate = jnp.minimum(hgu[:, :D_FF], SWIGLU_LIMIT)
        up = jnp.clip(hgu[:, D_FF:], -SWIGLU_LIMIT, SWIGLU_LIMIT)
        act = (up + 1.0) * gate * _sigmoid(SWIGLU_ALPHA * gate)
        y = jnp.dot(act.astype(BF16), wd_b[...], preferred_element_type=F32) + bd_ref[0]
        rid = lax.broadcasted_iota(jnp.int32, (tm, 1), 0)
        mine = jnp.where(rid >= lo_ref[w], jnp.where(rid < hi_ref[w], 1.0, 0.0), 0.0) > 0.5
        first = (w == 0) | (wt_ref[w] != wt_ref[prev])

        @pl.when(first)
        def _():
            ys_ref[...] = jnp.where(mine, y, 0.0)

        @pl.when(jnp.logical_not(first))
        def _():
            ys_ref[...] = jnp.where(mine, y, ys_ref[...])


def _moe(work, xs, w_gu, b_gu, w_down, b_down):
    tm = TM_MOE
    m = xs.shape[0]
    n_items = m // tm + N_EXPERTS - 1
    widx = lambda f: (lambda w, we, wt, lo, hi, nw: f(w, we, wt))
    grid_spec = pltpu.PrefetchScalarGridSpec(
        num_scalar_prefetch=5,
        grid=(n_items,),
        in_specs=[pl.BlockSpec((tm, D_MODEL), widx(lambda w, we, wt: (wt[w], 0))),
                  pl.BlockSpec((1, D_MODEL, 2 * D_FF), widx(lambda w, we, wt: (we[w], 0, 0))),
                  pl.BlockSpec((1, 1, 2 * D_FF), widx(lambda w, we, wt: (we[w], 0, 0))),
                  pl.BlockSpec((1, D_FF, D_MODEL), widx(lambda w, we, wt: (we[w], 0, 0))),
                  pl.BlockSpec((1, 1, D_MODEL), widx(lambda w, we, wt: (we[w], 0, 0)))],
        out_specs=pl.BlockSpec((tm, D_MODEL), widx(lambda w, we, wt: (wt[w], 0))),
        scratch_shapes=[pltpu.VMEM((D_MODEL, 2 * D_FF), BF16),
                        pltpu.VMEM((D_FF, D_MODEL), BF16)],
    )
    return pl.pallas_call(
        _moe_kernel,
        grid_spec=grid_spec,
        out_shape=jax.ShapeDtypeStruct((m, D_MODEL), F32),
        compiler_params=pltpu.CompilerParams(dimension_semantics=("arbitrary",),
                                             vmem_limit_bytes=VMEM_LIMIT),
        name="moe_ffn",
    )(*work, xs, w_gu, b_gu.reshape(N_EXPERTS, 1, -1), w_down, b_down.reshape(N_EXPERTS, 1, -1))


def _combine_kernel(cs_ref, nch_ref, h_ref, gate_ref, pos_ref, bs_ref, p_ref, ys_hbm, g2_ref, b2_ref, wg_ref,
                    bg_ref, wp_ref, g3_ref, b3_ref, o_ref, ybuf, sem):
    i = pl.program_id(0)
    n = pl.num_programs(0)

    def issue(t, b):
        def body(c, carry):
            dst = pl.multiple_of(c * COMB_CH, COMB_CH)
            src = pl.multiple_of(cs_ref[t * COMB_MAXCH + c], SUBLANES)
            pltpu.make_async_copy(ys_hbm.at[pl.ds(src, COMB_CH), :],
                                  ybuf.at[b, pl.ds(dst, COMB_CH), :], sem.at[b]).start()
            return carry

        lax.fori_loop(0, nch_ref[t], body, 0)

    @pl.when(i == 0)
    def _():
        def zero(j, carry):
            r0 = pl.multiple_of(j * COMB_CH, COMB_CH)
            for b in range(2):
                ybuf[b, pl.ds(r0, COMB_CH), :] = jnp.zeros((COMB_CH, D_MODEL), F32)
            return carry

        lax.fori_loop(0, COMB_MAXCH, zero, 0)
        issue(0, 0)

    @pl.when(i + 1 < n)
    def _():
        issue(i + 1, (i + 1) % 2)

    buf = i % 2

    def wait(c, carry):
        pltpu.make_async_copy(ys_hbm.at[pl.ds(0, COMB_CH), :], ybuf.at[buf, pl.ds(0, COMB_CH), :],
                              sem.at[buf]).wait()
        return carry

    lax.fori_loop(0, nch_ref[i], wait, 0)
    gates = gate_ref[...]
    slots = bs_ref[0]
    sel = jnp.zeros((h_ref.shape[0], COMB_MAXCH * COMB_CH), F32)
    for j in range(TOP_K):
        sel = jnp.where(pos_ref[:, j:j + 1] == slots, gates[:, j:j + 1], sel)
    ffn = jnp.dot(sel.astype(BF16), ybuf[buf].astype(BF16), preferred_element_type=F32)
    h2 = _layer_norm(DN_ALPHA * h_ref[...] + ffn, g2_ref[...], b2_ref[...])
    gate = _sigmoid(jnp.dot(h2.astype(BF16), wg_ref[...], preferred_element_type=F32) + bg_ref[...])
    ple = gate * jnp.dot(p_ref[...].astype(BF16), wp_ref[...], preferred_element_type=F32)
    o_ref[...] = _layer_norm(DN_ALPHA * h2 + ple, g3_ref[...], b3_ref[...])


def _combine(chunks, pos2d, h1, gates, p2d, ys, g2, b2, wg_b, bg, wp_b, g3, b3):
    n = h1.shape[0]
    tc = TC_COMB
    cstart, nch, bufslot = chunks
    row = lambda w: pl.BlockSpec((tc, w), lambda i, cs, nc: (i, 0))
    full = lambda r, c: pl.BlockSpec((r, c), lambda i, cs, nc: (0, 0))
    grid_spec = pltpu.PrefetchScalarGridSpec(
        num_scalar_prefetch=2,
        grid=(n // tc,),
        in_specs=[row(D_MODEL), row(LANES), row(TOP_K),
                  pl.BlockSpec((None, 1, COMB_MAXCH * COMB_CH), lambda i, cs, nc: (i, 0, 0)),
                  row(PLE_DIM), pl.BlockSpec(memory_space=pl.ANY),
                  full(1, D_MODEL), full(1, D_MODEL), full(D_MODEL, D_MODEL), full(1, D_MODEL),
                  full(PLE_DIM, D_MODEL), full(1, D_MODEL), full(1, D_MODEL)],
        out_specs=row(D_MODEL),
        scratch_shapes=[pltpu.VMEM((2, COMB_MAXCH * COMB_CH, D_MODEL), F32),
                        pltpu.SemaphoreType.DMA((2,))],
    )
    return pl.pallas_call(
        _combine_kernel,
        grid_spec=grid_spec,
        out_shape=jax.ShapeDtypeStruct((n, D_MODEL), F32),
        compiler_params=pltpu.CompilerParams(dimension_semantics=("arbitrary",),
                                             vmem_limit_bytes=VMEM_LIMIT),
        name="combine_ln2_ple_ln3",
    )(cstart, nch, h1, gates, pos2d, bufslot, p2d, ys, g2.reshape(1, -1), b2.reshape(1, -1), wg_b,
      bg.reshape(1, -1), wp_b, g3.reshape(1, -1), b3.reshape(1, -1))


def _combine_chunks(cntb, counts, grp_start, m):
    n_tiles = cntb.shape[0]
    nxt = jnp.concatenate([cntb[1:], counts[None, :]], axis=0)
    length = nxt - cntb
    first = grp_start[None, :] + cntb
    base = (first // SUBLANES) * SUBLANES
    nq = jnp.where(length > 0, (first - base + length + COMB_CH - 1) // COMB_CH, 0)
    q_end = jnp.cumsum(nq, axis=1)
    q_start = q_end - nq
    nch = q_end[:, -1]
    c = jnp.arange(COMB_MAXCH, dtype=jnp.int32)
    e_c = jnp.clip(jnp.sum((q_end[:, None, :] <= c[None, :, None]).astype(jnp.int32), -1), 0, N_EXPERTS - 1)
    pick = lambda a: jnp.take_along_axis(a, e_c, axis=1)
    want = pick(base) + COMB_CH * (c[None, :] - pick(q_start))
    start = jnp.minimum(want, m - COMB_CH)
    valid = c[None, :] < nch[:, None]
    cstart = jnp.where(valid, start, 0).astype(jnp.int32).reshape(-1)
    rows = start[:, :, None] + jnp.arange(COMB_CH, dtype=jnp.int32)
    lo = jnp.maximum(want, pick(first))[:, :, None]
    end = (pick(first) + pick(length))[:, :, None]
    own = valid[:, :, None] & (rows >= lo) & (rows < end)
    bufslot = jnp.where(own, rows, -1).astype(jnp.int32).reshape(n_tiles, 1, COMB_MAXCH * COMB_CH)
    return cstart, nch.astype(jnp.int32), bufslot


def _routing(idx, rank, counts, tm):
    n_tok = idx.shape[0]
    m = n_tok * TOP_K
    n_tiles = m // tm
    n_items = n_tiles + N_EXPERTS - 1
    grp_end = jnp.cumsum(counts)
    grp_start = grp_end - counts
    experts = jnp.arange(N_EXPERTS, dtype=jnp.int32)
    pos = (jnp.sum(jnp.where(idx[:, :, None] == experts, grp_start, 0), -1) + rank).reshape(-1).astype(jnp.int32)
    t_first = grp_start // tm
    t_last = (grp_end - 1) // tm
    n_e = jnp.where(counts > 0, t_last - t_first + 1, 0)
    w_end = jnp.cumsum(n_e)
    w_start = w_end - n_e
    n_work = w_end[-1]
    w = jnp.arange(n_items, dtype=jnp.int32)
    valid = w < n_work
    wq = jnp.minimum(w, n_work - 1)
    e_w = jnp.clip(jnp.sum((w_end[None, :] <= wq[:, None]).astype(jnp.int32), -1), 0, N_EXPERTS - 1)
    tile_w = t_first[e_w] + (wq - w_start[e_w])
    lo = jnp.clip(grp_start[e_w] - tile_w * tm, 0, tm)
    hi = jnp.clip(grp_end[e_w] - tile_w * tm, 0, tm)
    lo = jnp.where(valid, lo, 0)
    hi = jnp.where(valid, hi, 0)
    i32 = lambda a: a.astype(jnp.int32)
    return pos, i32(grp_start), (i32(e_w), i32(tile_w), i32(lo), i32(hi), i32(n_work).reshape(1))


def _layer(h, p_i, cos8, sin8, lam_init, w_in, conv_w, a_log, dt_bias, gdn_norm_w,
           lam_q1, lam_k1, lam_q2, lam_k2, diff_norm_w, w_out, ln1_g, ln1_b,
           router_w, router_b, w_gu, b_gu, w_down, b_down, ln2_g, ln2_b,
           ple_w, ple_gate_w, ple_gate_b, ln3_g, ln3_b):
    batch, seq, d = h.shape
    n = batch * seq
    x2d = h.reshape(n, d)
    o_ba = 4 * GDN_W
    o_d = o_ba + 2 * GDN_HEADS
    w_r = jnp.concatenate([w_in[:, :o_ba], w_in[:, o_d:], w_in[:, o_ba:o_d],
                           jnp.zeros((d, LANES - 2 * GDN_HEADS), w_in.dtype)], -1).astype(BF16)
    ones = jnp.ones((n, DIFF_D - ROPE_DIM), F32)
    zeros = jnp.zeros((n, DIFF_D - ROPE_DIM), F32)
    z8 = jnp.zeros_like(sin8)
    ctab = jnp.tile(jnp.concatenate([cos8, cos8, ones], -1), (1, 2))
    satab = jnp.tile(jnp.concatenate([z8, sin8, zeros], -1), (1, 2))
    sbtab = jnp.tile(jnp.concatenate([-sin8, z8, zeros], -1), (1, 2))
    a_proj, ba_proj, dq, dk, dv = _inproj(x2d, w_r, ctab, satab, sbtab)
    o_gdn = _gdn(a_proj, ba_proj, conv_w, a_log, dt_bias, gdn_norm_w, batch, seq)
    o_diff = _diff_attention(dq, dk, dv, lam_q1, lam_k1, lam_q2, lam_k2, diff_norm_w, lam_init, batch, seq)
    rw_pad = jnp.concatenate([router_w, jnp.zeros((d, LANES - N_EXPERTS), F32)], -1)
    rb_pad = jnp.concatenate([router_b, jnp.full((LANES - N_EXPERTS,), -jnp.inf, F32)]).reshape(1, LANES)
    h1, idx, gates, cnt, tcnt = _outproj(o_gdn, o_diff, x2d, w_out.astype(BF16), ln1_g, ln1_b, rw_pad, rb_pad)
    counts = cnt[0, :N_EXPERTS]
    pos, grp_start, work = _routing(idx[:, :TOP_K], idx[:, TOP_K:2 * TOP_K], counts, TM_MOE)
    xs = _dispatch(pos, h1)
    ys = _moe(work, xs, w_gu, b_gu, w_down, b_down)
    per_step = TM_PROJ // TC_COMB
    cntb = tcnt.reshape(n // TM_PROJ, 8, LANES)[:, :per_step, :N_EXPERTS].reshape(n // TC_COMB, N_EXPERTS)
    chunks = _combine_chunks(cntb, counts, grp_start, n * TOP_K)
    out = _combine(chunks, pos.reshape(n, TOP_K), h1, gates, p_i.reshape(n, PLE_DIM), ys, ln2_g, ln2_b,
                   ple_gate_w.astype(BF16), ple_gate_b, ple_w.astype(BF16), ln3_g, ln3_b)
    return out.reshape(batch, seq, d)


def kernel(x, p, positions, w_in, conv_w, a_log, dt_bias, gdn_norm_w, lam_q1, lam_k1, lam_q2, lam_k2,
           diff_norm_w, w_out, ln1_g, ln1_b, router_w, router_b, w_gu, b_gu, w_down, b_down, ln2_g, ln2_b,
           ple_w, ple_gate_w, ple_gate_b, ln3_g, ln3_b):
    batch, seq, _ = x.shape
    inv_freq = ROPE_THETA ** (-jnp.arange(0, ROPE_DIM, 2, dtype=F32) / ROPE_DIM)
    ang = (positions.astype(F32)[..., None] * inv_freq).reshape(batch * seq, ROPE_DIM // 2)
    cos8 = jnp.cos(ang)
    sin8 = jnp.sin(ang)
    h = x
    for i in range(w_in.shape[0]):
        lam_init = 0.8 - 0.6 * math.exp(-0.3 * i)
        h = _layer(h, p[i], cos8, sin8, lam_init, w_in[i], conv_w[i], a_log[i], dt_bias[i], gdn_norm_w[i],
                   lam_q1[i], lam_k1[i], lam_q2[i], lam_k2[i], diff_norm_w[i], w_out[i], ln1_g[i], ln1_b[i],
                   router_w[i], router_b[i], w_gu[i], b_gu[i], w_down[i], b_down[i], ln2_g[i], ln2_b[i],
                   ple_w[i], ple_gate_w[i], ple_gate_b[i], ln3_g[i], ln3_b[i])
    return h
```

```python
import functools
import math

import jax
import jax.numpy as jnp
from jax import lax
from jax.experimental import pallas as pl
from jax.experimental.pallas import tpu as pltpu

F32 = jnp.float32
BF16 = jnp.bfloat16
LOG2E = 1.4426950408889634

D_MODEL = 1024
PLE_DIM = 256
GDN_HEADS = 4
GDN_DK = 128
GDN_DV = 128
CONV_WIDTH = 4
CHUNK = 64
DIFF_HEADS = 4
DIFF_D = 64
DIFF_DV = 2 * DIFF_D
ROPE_THETA = 500000.0
ROPE_DIM = DIFF_D // 4
N_EXPERTS = 32
TOP_K = 4
D_FF = D_MODEL
SWIGLU_LIMIT = 7.0
SWIGLU_ALPHA = 1.702
DEPTH = 1
DN_ALPHA = (2 * DEPTH) ** 0.25
LN_EPS = 1e-5
RMS_EPS = 1e-6

LANES = 128
SUBLANES = 8
GDN_W = GDN_HEADS * GDN_DK
CONV_CH = 3 * GDN_W
A_COLS = 4 * GDN_W
DIFF_W = DIFF_HEADS * DIFF_DV
IN_PAD_W = A_COLS + 3 * DIFF_W + LANES
HSTACK = GDN_HEADS * CHUNK
VT_ROWS = DIFF_DV + 16
SUBBLK = 16

VMEM_LIMIT = 56 * 1024 * 1024

TM_PROJ = 512
T_GDN = 512
TQ = 256
TKV = 256
TM_MOE = 256
TC_DISP = 512
TC_COMB = 256
COMB_CH = 32
COMB_MAXCH = 72
assert COMB_MAXCH >= (TC_COMB * TOP_K + N_EXPERTS * (SUBLANES - 1 + COMB_CH - 1)) // COMB_CH
assert (COMB_MAXCH * COMB_CH) % LANES == 0


def _layer_norm(y, g, b):
    mu = jnp.mean(y, -1, keepdims=True)
    d = y - mu
    var = jnp.mean(d * d, -1, keepdims=True)
    return d * lax.rsqrt(var + LN_EPS) * g + b


def _sigmoid(x):
    return 1.0 / (1.0 + jnp.exp(-x))


def _inproj_kernel(x_ref, w_ref, c_ref, sa_ref, sb_ref, a_ref, ba_ref, q_ref, k_ref, v_ref):
    xb = x_ref[...].astype(BF16)
    a_ref[...] = jnp.dot(xb, w_ref[:, :A_COLS], preferred_element_type=F32)
    ba_ref[...] = jnp.dot(xb, w_ref[:, A_COLS + 3 * DIFF_W:], preferred_element_type=F32)
    c = c_ref[...]
    sa = sa_ref[...]
    sb = sb_ref[...]

    def rot(t):
        return t * c + pltpu.roll(t, 8, 1) * sa + pltpu.roll(t, LANES - 8, 1) * sb

    qk = jnp.dot(xb, w_ref[:, A_COLS:A_COLS + 2 * DIFF_W], preferred_element_type=F32)
    for h in range(DIFF_HEADS):
        q = qk[:, LANES * h:LANES * (h + 1)]
        q_ref[:, LANES * h:LANES * (h + 1)] = (rot(q) * (DIFF_D ** -0.5 * LOG2E)).astype(BF16)
        k = qk[:, DIFF_W + LANES * h:DIFF_W + LANES * (h + 1)]
        k_ref[:, LANES * h:LANES * (h + 1)] = rot(k).astype(BF16)
    lo = A_COLS + 2 * DIFF_W
    v_ref[...] = jnp.dot(xb, w_ref[:, lo:lo + DIFF_W], preferred_element_type=F32).astype(BF16)


def _inproj(x2d, w_r, ctab, satab, sbtab):
    n = x2d.shape[0]
    tm = TM_PROJ
    row = lambda w: pl.BlockSpec((tm, w), lambda i: (i, 0))
    return pl.pallas_call(
        _inproj_kernel,
        grid=(n // tm,),
        in_specs=[row(D_MODEL),
                  pl.BlockSpec((D_MODEL, IN_PAD_W), lambda i: (0, 0)),
                  row(LANES), row(LANES), row(LANES)],
        out_specs=[row(A_COLS), row(LANES), row(DIFF_W), row(DIFF_W), row(DIFF_W)],
        out_shape=[jax.ShapeDtypeStruct((n, A_COLS), F32),
                   jax.ShapeDtypeStruct((n, LANES), F32),
                   jax.ShapeDtypeStruct((n, DIFF_W), BF16),
                   jax.ShapeDtypeStruct((n, DIFF_W), BF16),
                   jax.ShapeDtypeStruct((n, DIFF_W), BF16)],
        compiler_params=pltpu.CompilerParams(dimension_semantics=("arbitrary",),
                                             vmem_limit_bytes=VMEM_LIMIT),
        name="inproj",
    )(x2d, w_r, ctab, satab, sbtab)


def _mm(a, b):
    return jnp.dot(a.astype(BF16), b.astype(BF16), preferred_element_type=F32)


def _gdn_kernel(a_ref, ba_ref, cw_ref, aux_ref, nw_ref, o_ref, xe_ref, q_s, k_s, v_s, state_ref):
    t_rows = a_ref.shape[0]
    st = pl.program_id(1)

    @pl.when(st == 0)
    def _():
        xe_ref[0:8, :] = jnp.zeros((8, CONV_CH), F32)
        state_ref[...] = jnp.zeros(state_ref.shape, F32)

    xe_ref[8:8 + t_rows, :] = a_ref[:, :CONV_CH]
    for s in range(CONV_CH // LANES):
        cs = slice(LANES * s, LANES * (s + 1))
        y = jnp.zeros((t_rows, LANES), F32)
        for j in range(CONV_WIDTH):
            off = 8 - (CONV_WIDTH - 1) + j
            y = y + xe_ref[off:off + t_rows, cs] * cw_ref[j:j + 1, cs]
        y = y * _sigmoid(y)
        grp, h = divmod(s, GDN_HEADS)
        hs = slice(LANES * h, LANES * (h + 1))
        if grp == 0:
            q_s[:, hs] = y * lax.rsqrt(jnp.sum(y * y, -1, keepdims=True) + 1e-6) * (GDN_DK ** -0.5)
        elif grp == 1:
            k_s[:, hs] = y * lax.rsqrt(jnp.sum(y * y, -1, keepdims=True) + 1e-6)
        else:
            v_s[:, hs] = y
    xe_ref[0:8, :] = xe_ref[t_rows:t_rows + 8, :]

    ba = ba_ref[...]
    beta_t = _sigmoid(ba)
    gx = ba + aux_ref[1:2, :]
    g_t = -jnp.exp(aux_ref[0:1, :]) * (jnp.maximum(gx, 0.0) + jnp.log(1.0 + jnp.exp(-jnp.abs(gx))))

    ri = lax.broadcasted_iota(jnp.int32, (HSTACK, HSTACK), 0)
    ci = lax.broadcasted_iota(jnp.int32, (HSTACK, HSTACK), 1)
    head_start = ri - (ri & (CHUNK - 1))
    in_head = ci >= head_start
    incl_f = jnp.where(in_head, jnp.where(ci <= ri, 1.0, 0.0), 0.0)
    strict_f = jnp.where(in_head, jnp.where(ci < ri, 1.0, 0.0), 0.0)
    sub_f = jnp.where(ci >= ri - (ri & (SUBBLK - 1)), 1.0, 0.0)
    eye = jnp.where(ri == ci, 1.0, 0.0)
    tri_b = (lax.broadcasted_iota(jnp.int32, (CHUNK, CHUNK), 0)
             >= lax.broadcasted_iota(jnp.int32, (CHUNK, CHUNK), 1)).astype(BF16)
    nt = (((1,), (1,)), ((), ()))
    tn = (((0,), (0,)), ((), ()))
    bdot = functools.partial(jnp.dot, preferred_element_type=F32)

    def stack(fn):
        return jnp.concatenate([fn(h) for h in range(GDN_HEADS)], axis=0)

    chunks = range(t_rows // CHUNK)
    crow = [slice(CHUNK * c, CHUNK * (c + 1)) for c in chunks]

    def cumdecay(g):
        g1 = g.astype(BF16)
        r1 = g - g1.astype(F32)
        g2 = r1.astype(BF16)
        g3 = (r1 - g2.astype(F32)).astype(BF16)
        return bdot(tri_b, g1) + bdot(tri_b, g2) + bdot(tri_b, g3)

    gc_l = [cumdecay(g_t[crow[c]]) for c in chunks]
    gct_l = [gc_l[c].T for c in chunks]
    gcol_l = [stack(lambda h: jnp.broadcast_to(gc_l[c][:, GDN_HEADS + h:GDN_HEADS + h + 1], (CHUNK, LANES)))
              for c in chunks]
    glast_l = [stack(lambda h: jnp.broadcast_to(gc_l[c][CHUNK - 1:CHUNK, GDN_HEADS + h:GDN_HEADS + h + 1],
                                                (CHUNK, LANES))) for c in chunks]
    bcol_l = [stack(lambda h: jnp.broadcast_to(beta_t[crow[c], h:h + 1], (CHUNK, LANES))) for c in chunks]
    grow_l = [jnp.concatenate([gct_l[c][GDN_HEADS + h:GDN_HEADS + h + 1, :] for h in range(GDN_HEADS)], axis=1)
              for c in chunks]
    kk_l = [stack(lambda h: k_s[crow[c], LANES * h:LANES * (h + 1)]) for c in chunks]
    qq_l = [stack(lambda h: q_s[crow[c], LANES * h:LANES * (h + 1)]) for c in chunks]
    vv_l = [stack(lambda h: v_s[crow[c], LANES * h:LANES * (h + 1)]) for c in chunks]
    dec_l = [jnp.exp(jnp.minimum(jnp.concatenate([gcol_l[c], gcol_l[c]], axis=1) - grow_l[c], 0.0)) * incl_f
             for c in chunks]
    kb_l = [kk_l[c] * bcol_l[c] for c in chunks]
    k16_l = [kk_l[c].astype(BF16) for c in chunks]
    amat_l = [lax.dot_general(kb_l[c].astype(BF16), k16_l[c], nt, preferred_element_type=F32)
              * dec_l[c] * strict_f for c in chunks]
    bm_l = [amat_l[c] * sub_f for c in chunks]
    nm_l = [amat_l[c] - bm_l[c] for c in chunks]
    b2_l = [_mm(bm_l[c], bm_l[c]) for c in chunks]
    b4_l = [_mm(b2_l[c], b2_l[c]) for c in chunks]
    b8_l = [_mm(b4_l[c], b4_l[c]) for c in chunks]
    d_l = [_mm(eye - bm_l[c], eye + b2_l[c]) for c in chunks]
    d_l = [_mm(d_l[c], eye + b4_l[c]) for c in chunks]
    dinv_l = [_mm(d_l[c], eye + b8_l[c]) for c in chunks]
    mm_l = [_mm(dinv_l[c], nm_l[c]) for c in chunks]
    m2_l = [_mm(mm_l[c], mm_l[c]) for c in chunks]
    t_l = [_mm(eye - mm_l[c], eye + m2_l[c]) for c in chunks]
    tinv_l = [_mm(t_l[c], dinv_l[c]) for c in chunks]
    eg_l = [jnp.exp(gcol_l[c]) for c in chunks]
    sol_l = [_mm(tinv_l[c], jnp.concatenate([vv_l[c] * bcol_l[c], kb_l[c] * eg_l[c]], axis=1)) for c in chunks]
    qk_l = [lax.dot_general(qq_l[c].astype(BF16), k16_l[c], nt, preferred_element_type=F32) * dec_l[c]
            for c in chunks]
    qd_l = [qq_l[c] * eg_l[c] for c in chunks]
    kd_l = [kk_l[c] * jnp.exp(glast_l[c] - gcol_l[c]) for c in chunks]

    for c in chunks:
        rows = crow[c]
        gc = gc_l[c]
        u = sol_l[c][:, :LANES]
        w = sol_l[c][:, LANES:]
        qk, qd, kd = qk_l[c], qd_l[c], kd_l[c]

        ws, qs = [], []
        for h in range(GDN_HEADS):
            hr = slice(CHUNK * h, CHUNK * (h + 1))
            lhs = jnp.concatenate([w[hr], qd[hr]], axis=0).astype(BF16)
            r = bdot(lhs, state_ref[h].astype(BF16))
            ws.append(r[:CHUNK])
            qs.append(r[CHUNK:])
        vn = u - jnp.concatenate(ws, axis=0)
        vn16 = vn.astype(BF16)
        o = jnp.concatenate(qs, axis=0) + bdot(qk.astype(BF16), vn16)
        kd16 = kd.astype(BF16)
        for h in range(GDN_HEADS):
            hr = slice(CHUNK * h, CHUNK * (h + 1))
            hs = slice(LANES * h, LANES * (h + 1))
            gl = jnp.exp(gc[CHUNK - 1:CHUNK, GDN_HEADS + h:GDN_HEADS + h + 1])
            state_ref[h] = state_ref[h] * gl + lax.dot_general(kd16[hr], vn16[hr], tn,
                                                               preferred_element_type=F32)
            oh = o[hr]
            z = a_ref[rows, CONV_CH + LANES * h:CONV_CH + LANES * (h + 1)]
            oh = oh * lax.rsqrt(jnp.mean(oh * oh, -1, keepdims=True) + RMS_EPS) * nw_ref[...]
            o_ref[rows, hs] = (oh * (z * _sigmoid(z))).astype(o_ref.dtype)


def _gdn(a_proj, ba_proj, conv_w, a_log, dt_bias, norm_w, batch, seq):
    t = T_GDN
    nst = seq // t
    rowblk = lambda w: pl.BlockSpec((t, w), lambda b, s: (b * nst + s, 0))
    aux = jnp.zeros((8, LANES), F32)
    aux = aux.at[0, GDN_HEADS:2 * GDN_HEADS].set(a_log).at[1, GDN_HEADS:2 * GDN_HEADS].set(dt_bias)
    return pl.pallas_call(
        _gdn_kernel,
        grid=(batch, nst),
        in_specs=[rowblk(A_COLS), rowblk(LANES),
                  pl.BlockSpec((CONV_WIDTH, CONV_CH), lambda b, s: (0, 0)),
                  pl.BlockSpec((8, LANES), lambda b, s: (0, 0)),
                  pl.BlockSpec((1, GDN_DV), lambda b, s: (0, 0))],
        out_specs=rowblk(GDN_W),
        out_shape=jax.ShapeDtypeStruct((batch * seq, GDN_W), BF16),
        scratch_shapes=[pltpu.VMEM((t + 8, CONV_CH), F32),
                        pltpu.VMEM((t, GDN_W), F32),
                        pltpu.VMEM((t, GDN_W), F32),
                        pltpu.VMEM((t, GDN_W), F32),
                        pltpu.VMEM((GDN_HEADS, GDN_DK, GDN_DV), F32)],
        compiler_params=pltpu.CompilerParams(dimension_semantics=("arbitrary", "arbitrary"),
                                             vmem_limit_bytes=VMEM_LIMIT),
        name="gdn",
    )(a_proj, ba_proj, conv_w, aux, norm_w.reshape(1, GDN_DV))


def _attn_kernel(q_ref, k_ref, vt_ref, lq1_ref, lk1_ref, lq2_ref, lk2_ref, nw_ref, o_ref, *, lam_init):
    tq = q_ref.shape[0]
    qi = pl.program_id(1)
    lane = lax.broadcasted_iota(jnp.int32, (1, LANES), 1)
    qpos = qi * tq + (lax.broadcasted_iota(jnp.int32, (1, 2 * tq), 1) & (tq - 1))
    nt = (((1,), (1,)), ((), ()))
    q2 = []
    for h in range(DIFF_HEADS):
        q = q_ref[:, LANES * h:LANES * (h + 1)]
        zero = jnp.zeros_like(q)
        q2.append(jnp.concatenate([jnp.where(lane < DIFF_D, q, zero), jnp.where(lane >= DIFF_D, q, zero)], axis=0))

    heads = range(DIFF_HEADS)

    def scores(kj):
        off = pl.multiple_of(kj * TKV, TKV)
        return tuple(lax.dot_general(k_ref[pl.ds(off, TKV), LANES * h:LANES * (h + 1)], q2[h], nt,
                                     preferred_element_type=F32) for h in heads)

    def update(kj, ss, state, masked):
        off = pl.multiple_of(kj * TKV, TKV)
        if masked:
            keep = off + lax.broadcasted_iota(jnp.int32, (TKV, 1), 0) <= qpos
            ss = [jnp.where(keep, s, -1e30) for s in ss]
        mns = [jnp.maximum(state[h][0], jnp.max(ss[h], 0, keepdims=True)) for h in heads]
        ps = [jnp.exp2((ss[h] - mns[h]).astype(BF16)) for h in heads]
        als = [jnp.exp2(state[h][0] - mns[h]) for h in heads]
        pvs = [jnp.dot(vt_ref[VT_ROWS * h:VT_ROWS * (h + 1), pl.ds(off, TKV)], ps[h],
                       preferred_element_type=F32) for h in heads]
        return tuple((mns[h], als[h] * state[h][1] + pvs[h]) for h in heads)

    def body(kj, state):
        return update(kj, scores(kj), state, False)

    init = tuple((jnp.full((1, 2 * tq), -1e30, F32), jnp.zeros((VT_ROWS, 2 * tq), F32)) for _ in heads)
    state = lax.fori_loop(0, qi, body, init)
    carry = update(qi, scores(qi), state, True)
    lam = (jnp.exp(jnp.sum(lq1_ref[...] * lk1_ref[...], -1, keepdims=True))
           - jnp.exp(jnp.sum(lq2_ref[...] * lk2_ref[...], -1, keepdims=True)) + lam_init)
    for h in range(DIFF_HEADS):
        m, acc = carry[h]
        on = acc[:DIFF_DV] * (1.0 / acc[DIFF_DV:DIFF_DV + 1])
        o = (on[:, :tq] - lam * on[:, tq:]).T
        o = o * lax.rsqrt(jnp.mean(o * o, -1, keepdims=True) + RMS_EPS) * nw_ref[...] * (1.0 - lam_init)
        o_ref[:, LANES * h:LANES * (h + 1)] = o.astype(o_ref.dtype)


def _diff_attention(q, k, v, lq1, lk1, lq2, lk2, norm_w, lam_init, batch, seq):
    q3 = q.reshape(batch, seq, DIFF_W)
    k3 = k.reshape(batch, seq, DIFF_W)
    v4 = v.reshape(batch, seq, DIFF_HEADS, DIFF_DV).transpose(0, 2, 3, 1)
    ones = jnp.ones((batch, DIFF_HEADS, VT_ROWS - DIFF_DV, seq), v.dtype)
    v3 = jnp.concatenate([v4, ones], axis=2).reshape(batch, DIFF_HEADS * VT_ROWS, seq)
    small = lambda w: pl.BlockSpec((1, w), lambda b, i: (0, 0))
    out = pl.pallas_call(
        functools.partial(_attn_kernel, lam_init=lam_init),
        grid=(batch, seq // TQ),
        in_specs=[pl.BlockSpec((None, TQ, DIFF_W), lambda b, i: (b, i, 0)),
                  pl.BlockSpec((None, seq, DIFF_W), lambda b, i: (b, 0, 0)),
                  pl.BlockSpec((None, DIFF_HEADS * VT_ROWS, seq), lambda b, i: (b, 0, 0)),
                  small(DIFF_D), small(DIFF_D), small(DIFF_D), small(DIFF_D), small(DIFF_DV)],
        out_specs=pl.BlockSpec((None, TQ, DIFF_W), lambda b, i: (b, i, 0)),
        out_shape=jax.ShapeDtypeStruct((batch, seq, DIFF_W), BF16),
        compiler_params=pltpu.CompilerParams(
            dimension_semantics=("arbitrary", "arbitrary"),
            vmem_limit_bytes=VMEM_LIMIT),
        name="diff_attn",
    )(q3, k3, v3, lq1.reshape(1, -1), lk1.reshape(1, -1), lq2.reshape(1, -1), lk2.reshape(1, -1),
      norm_w.reshape(1, -1))
    return out.reshape(batch * seq, DIFF_W)


def _outproj_kernel(og_ref, od_ref, x_ref, wo_ref, g_ref, b_ref, rw_ref, rb_ref,
                    h_ref, idx_ref, gate_ref, cnt_out_ref, tcnt_ref, cnt_ref):
    @pl.when(pl.program_id(0) == 0)
    def _():
        cnt_ref[...] = jnp.zeros(cnt_ref.shape, F32)

    mix = (jnp.dot(og_ref[...], wo_ref[:GDN_W, :], preferred_element_type=F32)
           + jnp.dot(od_ref[...], wo_ref[GDN_W:, :], preferred_element_type=F32))
    h = _layer_norm(DN_ALPHA * x_ref[...] + mix, g_ref[...], b_ref[...])
    h_ref[...] = h
    h_hi = h.astype(BF16)
    h_lo = (h - h_hi.astype(F32)).astype(BF16)
    logits = jnp.dot(jnp.concatenate([h_hi, h_hi, h_lo], axis=1), rw_ref[...],
                     preferred_element_type=F32) + rb_ref[...]
    tm = logits.shape[0]
    lane = lax.broadcasted_iota(jnp.int32, (tm, LANES), 1)
    lane_f = lane.astype(F32)
    work = logits
    vals, idxs = [], []
    for _ in range(TOP_K):
        m = jnp.max(work, -1, keepdims=True)
        sel = jnp.min(jnp.where(work == m, lane_f, float(LANES)), -1, keepdims=True)
        vals.append(m)
        idxs.append(sel)
        work = jnp.where(lane_f == sel, -jnp.inf, work)
    exps = [jnp.exp(vv - vals[0]) for vv in vals]
    inv = 1.0 / (exps[0] + exps[1] + exps[2] + exps[3])

    hot = jnp.zeros((tm, LANES), F32)
    for j in range(TOP_K):
        hot = hot + jnp.where(lane_f == idxs[j], 1.0, 0.0)
    ri = lax.broadcasted_iota(jnp.int32, (tm, tm), 0)
    ci = lax.broadcasted_iota(jnp.int32, (tm, tm), 1)
    before = jnp.where(ci < ri, 1.0, 0.0).astype(BF16)
    prefix = jnp.dot(before, hot.astype(BF16), preferred_element_type=F32) + cnt_ref[...]
    cnt_ref[...] = cnt_ref[...] + jnp.sum(hot, 0, keepdims=True)
    cnt_out_ref[...] = cnt_ref[...].astype(jnp.int32)
    starts = [prefix[TC_COMB * j:TC_COMB * j + 1, :] for j in range(tm // TC_COMB)]
    pad = jnp.zeros((tcnt_ref.shape[0] - len(starts), LANES), F32)
    tcnt_ref[...] = jnp.concatenate(starts + [pad], axis=0).astype(jnp.int32)

    idx_out = jnp.zeros((tm, LANES), F32)
    gate_out = jnp.zeros((tm, LANES), F32)
    for j in range(TOP_K):
        rank = jnp.sum(jnp.where(lane_f == idxs[j], prefix, 0.0), -1, keepdims=True)
        idx_out = jnp.where(lane == j, idxs[j], idx_out)
        idx_out = jnp.where(lane == TOP_K + j, rank, idx_out)
        gate_out = jnp.where(lane == j, exps[j] * inv, gate_out)
    idx_ref[...] = idx_out.astype(jnp.int32)
    gate_ref[...] = gate_out


def _outproj(og, od, x2d, w_out_b, ln_g, ln_b, rw_pad, rb_pad):
    n = x2d.shape[0]
    tm = TM_PROJ
    row = lambda w: pl.BlockSpec((tm, w), lambda i: (i, 0))
    full = lambda r, c: pl.BlockSpec((r, c), lambda i: (0, 0))
    return pl.pallas_call(
        _outproj_kernel,
        grid=(n // tm,),
        in_specs=[row(GDN_W), row(DIFF_W), row(D_MODEL), full(GDN_W + DIFF_W, D_MODEL),
                  full(1, D_MODEL), full(1, D_MODEL), full(3 * D_MODEL, LANES), full(1, LANES)],
        out_specs=[row(D_MODEL), row(LANES), row(LANES), full(1, LANES),
                   pl.BlockSpec((8, LANES), lambda i: (i, 0))],
        out_shape=[jax.ShapeDtypeStruct((n, D_MODEL), F32),
                   jax.ShapeDtypeStruct((n, LANES), jnp.int32),
                   jax.ShapeDtypeStruct((n, LANES), F32),
                   jax.ShapeDtypeStruct((1, LANES), jnp.int32),
                   jax.ShapeDtypeStruct((n // tm * 8, LANES), jnp.int32)],
        scratch_shapes=[pltpu.VMEM((1, LANES), F32)],
        compiler_params=pltpu.CompilerParams(dimension_semantics=("arbitrary",),
                                             vmem_limit_bytes=VMEM_LIMIT),
        name="outproj_ln1_router",
    )(og, od, x2d, w_out_b, ln_g.reshape(1, -1), ln_b.reshape(1, -1), rw_pad, rb_pad)


def _dispatch_kernel(pos_ref, h_ref, xs_hbm, sem):
    tc = h_ref.shape[0]
    base = pl.program_id(0) * (tc * TOP_K)

    def body(r, carry):
        for j in range(TOP_K):
            s = pos_ref[base + r * TOP_K + j]
            pltpu.make_async_copy(h_ref.at[pl.ds(r, 1), :], xs_hbm.at[pl.ds(s, 1), :], sem).start()
        return carry

    lax.fori_loop(0, tc, body, 0, unroll=4)
    for j in range(TOP_K):
        pltpu.make_async_copy(h_ref, xs_hbm.at[pl.ds(0, tc), :], sem).wait()


def _dispatch(pos, h1):
    n = h1.shape[0]
    tc = TC_DISP
    grid_spec = pltpu.PrefetchScalarGridSpec(
        num_scalar_prefetch=1,
        grid=(n // tc,),
        in_specs=[pl.BlockSpec((tc, D_MODEL), lambda i, pos: (i, 0))],
        out_specs=pl.BlockSpec(memory_space=pl.ANY),
        scratch_shapes=[pltpu.SemaphoreType.DMA],
    )
    return pl.pallas_call(
        _dispatch_kernel,
        grid_spec=grid_spec,
        out_shape=jax.ShapeDtypeStruct((n * TOP_K, D_MODEL), F32),
        compiler_params=pltpu.CompilerParams(dimension_semantics=("arbitrary",),
                                             vmem_limit_bytes=VMEM_LIMIT),
        name="dispatch",
    )(pos, h1)


def _moe_kernel(we_ref, wt_ref, lo_ref, hi_ref, nw_ref, xs_ref, wgu_ref, bgu_ref, wd_ref, bd_ref, ys_ref,
                wgu_b, wd_b):
    tm = ys_ref.shape[0]
    w = pl.program_id(0)

    @pl.when(w < nw_ref[0])
    def _():
        prev = jnp.maximum(w - 1, 0)

        @pl.when((w == 0) | (we_ref[w] != we_ref[prev]))
        def _():
            rows = 128

            def cast(j, carry):
                r0 = pl.multiple_of(j * rows, rows)
                wgu_b[pl.ds(r0, rows), :] = wgu_ref[0, pl.ds(r0, rows), :].astype(BF16)
                wd_b[pl.ds(r0, rows), :] = wd_ref[0, pl.ds(r0, rows), :].astype(BF16)
                return carry

            lax.fori_loop(0, D_MODEL // rows, cast, 0)

        xb = xs_ref[...].astype(BF16)
        hgu = jnp.dot(xb, wgu_b[...], preferred_element_type=F32) + bgu_ref[0]
        gate = jnp.minimum(hgu[:, :D_FF], SWIGLU_LIMIT)
        up = jnp.clip(hgu[:, D_FF:], -SWIGLU_LIMIT, SWIGLU_LIMIT)
        act = (up + 1.0) * gate * _sigmoid(SWIGLU_ALPHA * gate)
        y = jnp.dot(act.astype(BF16), wd_b[...], preferred_element_type=F32) + bd_ref[0]
        rid = lax.broadcasted_iota(jnp.int32, (tm, 1), 0)
        mine = jnp.where(rid >= lo_ref[w], jnp.where(rid < hi_ref[w], 1.0, 0.0), 0.0) > 0.5
        first = (w == 0) | (wt_ref[w] != wt_ref[prev])

        @pl.when(first)
        def _():
            ys_ref[...] = jnp.where(mine, y, 0.0)

        @pl.when(jnp.logical_not(first))
        def _():
            ys_ref[...] = jnp.where(mine, y, ys_ref[...])


def _moe(work, xs, w_gu, b_gu, w_down, b_down):
    tm = TM_MOE
    m = xs.shape[0]
    n_items = m // tm + N_EXPERTS - 1
    widx = lambda f: (lambda w, we, wt, lo, hi, nw: f(w, we, wt))
    grid_spec = pltpu.PrefetchScalarGridSpec(
        num_scalar_prefetch=5,
        grid=(n_items,),
        in_specs=[pl.BlockSpec((tm, D_MODEL), widx(lambda w, we, wt: (wt[w], 0))),
                  pl.BlockSpec((1, D_MODEL, 2 * D_FF), widx(lambda w, we, wt: (we[w], 0, 0))),
                  pl.BlockSpec((1, 1, 2 * D_FF), widx(lambda w, we, wt: (we[w], 0, 0))),
                  pl.BlockSpec((1, D_FF, D_MODEL), widx(lambda w, we, wt: (we[w], 0, 0))),
                  pl.BlockSpec((1, 1, D_MODEL), widx(lambda w, we, wt: (we[w], 0, 0)))],
        out_specs=pl.BlockSpec((tm, D_MODEL), widx(lambda w, we, wt: (wt[w], 0))),
        scratch_shapes=[pltpu.VMEM((D_MODEL, 2 * D_FF), BF16),
                        pltpu.VMEM((D_FF, D_MODEL), BF16)],
    )
    return pl.pallas_call(
        _moe_kernel,
        grid_spec=grid_spec,
        out_shape=jax.ShapeDtypeStruct((m, D_MODEL), F32),
        compiler_params=pltpu.CompilerParams(dimension_semantics=("arbitrary",),
                                             vmem_limit_bytes=VMEM_LIMIT),
        name="moe_ffn",
    )(*work, xs, w_gu, b_gu.reshape(N_EXPERTS, 1, -1), w_down, b_down.reshape(N_EXPERTS, 1, -1))


def _combine_kernel(cs_ref, nch_ref, h_ref, gate_ref, pos_ref, bs_ref, p_ref, ys_hbm, g2_ref, b2_ref, wg_ref,
                    bg_ref, wp_ref, g3_ref, b3_ref, o_ref, ybuf, sem):
    i = pl.program_id(0)
    n = pl.num_programs(0)

    def issue(t, b):
        def body(c, carry):
            dst = pl.multiple_of(c * COMB_CH, COMB_CH)
            src = pl.multiple_of(cs_ref[t * COMB_MAXCH + c], SUBLANES)
            pltpu.make_async_copy(ys_hbm.at[pl.ds(src, COMB_CH), :],
                                  ybuf.at[b, pl.ds(dst, COMB_CH), :], sem.at[b]).start()
            return carry

        lax.fori_loop(0, nch_ref[t], body, 0)

    @pl.when(i == 0)
    def _():
        def zero(j, carry):
            r0 = pl.multiple_of(j * COMB_CH, COMB_CH)
            for b in range(2):
                ybuf[b, pl.ds(r0, COMB_CH), :] = jnp.zeros((COMB_CH, D_MODEL), F32)
            return carry

        lax.fori_loop(0, COMB_MAXCH, zero, 0)
        issue(0, 0)

    @pl.when(i + 1 < n)
    def _():
        issue(i + 1, (i + 1) % 2)

    buf = i % 2

    def wait(c, carry):
        pltpu.make_async_copy(ys_hbm.at[pl.ds(0, COMB_CH), :], ybuf.at[buf, pl.ds(0, COMB_CH), :],
                              sem.at[buf]).wait()
        return carry

    lax.fori_loop(0, nch_ref[i], wait, 0)
    gates = gate_ref[...]
    slots = bs_ref[0]
    sel = jnp.zeros((h_ref.shape[0], COMB_MAXCH * COMB_CH), F32)
    for j in range(TOP_K):
        sel = jnp.where(pos_ref[:, j:j + 1] == slots, gates[:, j:j + 1], sel)
    ffn = jnp.dot(sel.astype(BF16), ybuf[buf].astype(BF16), preferred_element_type=F32)
    h2 = _layer_norm(DN_ALPHA * h_ref[...] + ffn, g2_ref[...], b2_ref[...])
    gate = _sigmoid(jnp.dot(h2.astype(BF16), wg_ref[...], preferred_element_type=F32) + bg_ref[...])
    ple = gate * jnp.dot(p_ref[...].astype(BF16), wp_ref[...], preferred_element_type=F32)
    o_ref[...] = _layer_norm(DN_ALPHA * h2 + ple, g3_ref[...], b3_ref[...])


def _combine(chunks, pos2d, h1, gates, p2d, ys, g2, b2, wg_b, bg, wp_b, g3, b3):
    n = h1.shape[0]
    tc = TC_COMB
    cstart, nch, bufslot = chunks
    row = lambda w: pl.BlockSpec((tc, w), lambda i, cs, nc: (i, 0))
    full = lambda r, c: pl.BlockSpec((r, c), lambda i, cs, nc: (0, 0))
    grid_spec = pltpu.PrefetchScalarGridSpec(
        num_scalar_prefetch=2,
        grid=(n // tc,),
        in_specs=[row(D_MODEL), row(LANES), row(TOP_K),
                  pl.BlockSpec((None, 1, COMB_MAXCH * COMB_CH), lambda i, cs, nc: (i, 0, 0)),
                  row(PLE_DIM), pl.BlockSpec(memory_space=pl.ANY),
                  full(1, D_MODEL), full(1, D_MODEL), full(D_MODEL, D_MODEL), full(1, D_MODEL),
                  full(PLE_DIM, D_MODEL), full(1, D_MODEL), full(1, D_MODEL)],
        out_specs=row(D_MODEL),
        scratch_shapes=[pltpu.VMEM((2, COMB_MAXCH * COMB_CH, D_MODEL), F32),
                        pltpu.SemaphoreType.DMA((2,))],
    )
    return pl.pallas_call(
        _combine_kernel,
        grid_spec=grid_spec,
        out_shape=jax.ShapeDtypeStruct((n, D_MODEL), F32),
        compiler_params=pltpu.CompilerParams(dimension_semantics=("arbitrary",),
                                             vmem_limit_bytes=VMEM_LIMIT),
        name="combine_ln2_ple_ln3",
    )(cstart, nch, h1, gates, pos2d, bufslot, p2d, ys, g2.reshape(1, -1), b2.reshape(1, -1), wg_b,
      bg.reshape(1, -1), wp_b, g3.reshape(1, -1), b3.reshape(1, -1))


def _combine_chunks(cntb, counts, grp_start, m):
    n_tiles = cntb.shape[0]
    nxt = jnp.concatenate([cntb[1:], counts[None, :]], axis=0)
    length = nxt - cntb
    first = grp_start[None, :] + cntb
    base = (first // SUBLANES) * SUBLANES
    nq = jnp.where(length > 0, (first - base + length + COMB_CH - 1) // COMB_CH, 0)
    q_end = jnp.cumsum(nq, axis=1)
    q_start = q_end - nq
    nch = q_end[:, -1]
    c = jnp.arange(COMB_MAXCH, dtype=jnp.int32)
    e_c = jnp.clip(jnp.sum((q_end[:, None, :] <= c[None, :, None]).astype(jnp.int32), -1), 0, N_EXPERTS - 1)
    onehot = e_c[:, :, None] == jnp.arange(N_EXPERTS, dtype=jnp.int32)
    pick = lambda a: jnp.sum(jnp.where(onehot, a[:, None, :], 0), -1)
    want = pick(base) + COMB_CH * (c[None, :] - pick(q_start))
    start = jnp.minimum(want, m - COMB_CH)
    valid = c[None, :] < nch[:, None]
    cstart = jnp.where(valid, start, 0).astype(jnp.int32).reshape(-1)
    rows = start[:, :, None] + jnp.arange(COMB_CH, dtype=jnp.int32)
    lo = jnp.maximum(want, pick(first))[:, :, None]
    end = (pick(first) + pick(length))[:, :, None]
    own = valid[:, :, None] & (rows >= lo) & (rows < end)
    bufslot = jnp.where(own, rows, -1).astype(jnp.int32).reshape(n_tiles, 1, COMB_MAXCH * COMB_CH)
    return cstart, nch.astype(jnp.int32), bufslot


def _routing(idx, rank, counts, tm):
    n_tok = idx.shape[0]
    m = n_tok * TOP_K
    n_tiles = m // tm
    n_items = n_tiles + N_EXPERTS - 1
    grp_end = jnp.cumsum(counts)
    grp_start = grp_end - counts
    experts = jnp.arange(N_EXPERTS, dtype=jnp.int32)
    pos = (jnp.sum(jnp.where(idx[:, :, None] == experts, grp_start, 0), -1) + rank).reshape(-1).astype(jnp.int32)
    t_first = grp_start // tm
    t_last = (grp_end - 1) // tm
    n_e = jnp.where(counts > 0, t_last - t_first + 1, 0)
    w_end = jnp.cumsum(n_e)
    w_start = w_end - n_e
    n_work = w_end[-1]
    w = jnp.arange(n_items, dtype=jnp.int32)
    valid = w < n_work
    wq = jnp.minimum(w, n_work - 1)
    e_w = jnp.clip(jnp.sum((w_end[None, :] <= wq[:, None]).astype(jnp.int32), -1), 0, N_EXPERTS - 1)
    onehot = e_w[:, None] == experts
    pick = lambda a: jnp.sum(jnp.where(onehot, a[None, :], 0), -1)
    tile_w = pick(t_first) + (wq - pick(w_start))
    lo = jnp.clip(pick(grp_start) - tile_w * tm, 0, tm)
    hi = jnp.clip(pick(grp_end) - tile_w * tm, 0, tm)
    lo = jnp.where(valid, lo, 0)
    hi = jnp.where(valid, hi, 0)
    i32 = lambda a: a.astype(jnp.int32)
    return pos, i32(grp_start), (i32(e_w), i32(tile_w), i32(lo), i32(hi), i32(n_work).reshape(1))


def _layer(h, p_i, cos8, sin8, lam_init, w_in, conv_w, a_log, dt_bias, gdn_norm_w,
           lam_q1, lam_k1, lam_q2, lam_k2, diff_norm_w, w_out, ln1_g, ln1_b,
           router_w, router_b, w_gu, b_gu, w_down, b_down, ln2_g, ln2_b,
           ple_w, ple_gate_w, ple_gate_b, ln3_g, ln3_b):
    batch, seq, d = h.shape
    n = batch * seq
    x2d = h.reshape(n, d)
    o_ba = 4 * GDN_W
    o_d = o_ba + 2 * GDN_HEADS
    w_r = jnp.concatenate([w_in[:, :o_ba], w_in[:, o_d:], w_in[:, o_ba:o_d],
                           jnp.zeros((d, LANES - 2 * GDN_HEADS), w_in.dtype)], -1).astype(BF16)
    ones = jnp.ones((n, DIFF_D - ROPE_DIM), F32)
    zeros = jnp.zeros((n, DIFF_D - ROPE_DIM), F32)
    z8 = jnp.zeros_like(sin8)
    ctab = jnp.tile(jnp.concatenate([cos8, cos8, ones], -1), (1, 2))
    satab = jnp.tile(jnp.concatenate([z8, sin8, zeros], -1), (1, 2))
    sbtab = jnp.tile(jnp.concatenate([-sin8, z8, zeros], -1), (1, 2))
    a_proj, ba_proj, dq, dk, dv = _inproj(x2d, w_r, ctab, satab, sbtab)
    o_gdn = _gdn(a_proj, ba_proj, conv_w, a_log, dt_bias, gdn_norm_w, batch, seq)
    o_diff = _diff_attention(dq, dk, dv, lam_q1, lam_k1, lam_q2, lam_k2, diff_norm_w, lam_init, batch, seq)
    rw_pad = jnp.concatenate([router_w, jnp.zeros((d, LANES - N_EXPERTS), F32)], -1)
    rw_hi = rw_pad.astype(BF16)
    rw_lo = (rw_pad - rw_hi.astype(F32)).astype(BF16)
    rw_pad = jnp.concatenate([rw_hi, rw_lo, rw_hi], axis=0)
    rb_pad = jnp.concatenate([router_b, jnp.full((LANES - N_EXPERTS,), -jnp.inf, F32)]).reshape(1, LANES)
    h1, idx, gates, cnt, tcnt = _outproj(o_gdn, o_diff, x2d, w_out.astype(BF16), ln1_g, ln1_b, rw_pad, rb_pad)
    counts = cnt[0, :N_EXPERTS]
    pos, grp_start, work = _routing(idx[:, :TOP_K], idx[:, TOP_K:2 * TOP_K], counts, TM_MOE)
    xs = _dispatch(pos, h1)
    ys = _moe(work, xs, w_gu, b_gu, w_down, b_down)
    per_step = TM_PROJ // TC_COMB
    cntb = tcnt.reshape(n // TM_PROJ, 8, LANES)[:, :per_step, :N_EXPERTS].reshape(n // TC_COMB, N_EXPERTS)
    chunks = _combine_chunks(cntb, counts, grp_start, n * TOP_K)
    out = _combine(chunks, pos.reshape(n, TOP_K), h1, gates, p_i.reshape(n, PLE_DIM), ys, ln2_g, ln2_b,
                   ple_gate_w.astype(BF16), ple_gate_b, ple_w.astype(BF16), ln3_g, ln3_b)
    return out.reshape(batch, seq, d)


def kernel(x, p, positions, w_in, conv_w, a_log, dt_bias, gdn_norm_w, lam_q1, lam_k1, lam_q2, lam_k2,
           diff_norm_w, w_out, ln1_g, ln1_b, router_w, router_b, w_gu, b_gu, w_down, b_down, ln2_g, ln2_b,
           ple_w, ple_gate_w, ple_gate_b, ln3_g, ln3_b):
    batch, seq, _ = x.shape
    inv_freq = ROPE_THETA ** (-jnp.arange(0, ROPE_DIM, 2, dtype=F32) / ROPE_DIM)
    ang = (positions.astype(F32)[..., None] * inv_freq).reshape(batch * seq, ROPE_DIM // 2)
    cos8 = jnp.cos(ang)
    sin8 = jnp.sin(ang)
    h = x
    for i in range(w_in.shape[0]):
        lam_init = 0.8 - 0.6 * math.exp(-0.3 * i)
        h = _layer(h, p[i], cos8, sin8, lam_init, w_in[i], conv_w[i], a_log[i], dt_bias[i], gdn_norm_w[i],
                   lam_q1[i], lam_k1[i], lam_q2[i], lam_k2[i], diff_norm_w[i], w_out[i], ln1_g[i], ln1_b[i],
                   router_w[i], router_b[i], w_gu[i], b_gu[i], w_down[i], b_down[i], ln2_g[i], ln2_b[i],
                   ple_w[i], ple_gate_w[i], ple_gate_b[i], ln3_g[i], ln3_b[i])
    return h
```

```python
import functools
import math

import jax
import jax.numpy as jnp
from jax import lax
from jax.experimental import pallas as pl
from jax.experimental.pallas import tpu as pltpu

F32 = jnp.float32
BF16 = jnp.bfloat16
LOG2E = 1.4426950408889634

D_MODEL = 1024
PLE_DIM = 256
GDN_HEADS = 4
GDN_DK = 128
GDN_DV = 128
CONV_WIDTH = 4
CHUNK = 64
DIFF_HEADS = 4
DIFF_D = 64
DIFF_DV = 2 * DIFF_D
ROPE_THETA = 500000.0
ROPE_DIM = DIFF_D // 4
N_EXPERTS = 32
TOP_K = 4
D_FF = D_MODEL
SWIGLU_LIMIT = 7.0
SWIGLU_ALPHA = 1.702
DEPTH = 1
DN_ALPHA = (2 * DEPTH) ** 0.25
LN_EPS = 1e-5
RMS_EPS = 1e-6

LANES = 128
SUBLANES = 8
GDN_W = GDN_HEADS * GDN_DK
CONV_CH = 3 * GDN_W
A_COLS = 4 * GDN_W
DIFF_W = DIFF_HEADS * DIFF_DV
IN_PAD_W = A_COLS + 3 * DIFF_W + LANES
HSTACK = GDN_HEADS * CHUNK
VT_ROWS = DIFF_DV + 16
SUBBLK = 16

VMEM_LIMIT = 56 * 1024 * 1024

TM_PROJ = 512
T_GDN = 512
TQ = 512
TKV = 512
TM_MOE = 256
TC_DISP = 512
TC_COMB = 256
COMB_CH = 32
COMB_MAXCH = 72
assert COMB_MAXCH >= (TC_COMB * TOP_K + N_EXPERTS * (SUBLANES - 1 + COMB_CH - 1)) // COMB_CH
assert (COMB_MAXCH * COMB_CH) % LANES == 0


def _layer_norm(y, g, b):
    mu = jnp.mean(y, -1, keepdims=True)
    d = y - mu
    var = jnp.mean(d * d, -1, keepdims=True)
    return d * lax.rsqrt(var + LN_EPS) * g + b


def _sigmoid(x):
    return 1.0 / (1.0 + jnp.exp(-x))


def _inproj_kernel(x_ref, w_ref, c_ref, sa_ref, sb_ref, a_ref, ba_ref, q_ref, k_ref, v_ref):
    xb = x_ref[...].astype(BF16)
    a_ref[...] = jnp.dot(xb, w_ref[:, :A_COLS], preferred_element_type=F32)
    ba_ref[...] = jnp.dot(xb, w_ref[:, A_COLS + 3 * DIFF_W:], preferred_element_type=F32)
    c = c_ref[...]
    sa = sa_ref[...]
    sb = sb_ref[...]

    def rot(t):
        return t * c + pltpu.roll(t, 8, 1) * sa + pltpu.roll(t, LANES - 8, 1) * sb

    qk = jnp.dot(xb, w_ref[:, A_COLS:A_COLS + 2 * DIFF_W], preferred_element_type=F32)
    for h in range(DIFF_HEADS):
        q = qk[:, LANES * h:LANES * (h + 1)]
        q_ref[:, LANES * h:LANES * (h + 1)] = (rot(q) * (DIFF_D ** -0.5 * LOG2E)).astype(BF16)
        k = qk[:, DIFF_W + LANES * h:DIFF_W + LANES * (h + 1)]
        k_ref[:, LANES * h:LANES * (h + 1)] = rot(k).astype(BF16)
    lo = A_COLS + 2 * DIFF_W
    v_ref[...] = jnp.dot(xb, w_ref[:, lo:lo + DIFF_W], preferred_element_type=F32).astype(BF16)


def _inproj(x2d, w_r, ctab, satab, sbtab):
    n = x2d.shape[0]
    tm = TM_PROJ
    row = lambda w: pl.BlockSpec((tm, w), lambda i: (i, 0))
    return pl.pallas_call(
        _inproj_kernel,
        grid=(n // tm,),
        in_specs=[row(D_MODEL),
                  pl.BlockSpec((D_MODEL, IN_PAD_W), lambda i: (0, 0)),
                  row(LANES), row(LANES), row(LANES)],
        out_specs=[row(A_COLS), row(LANES), row(DIFF_W), row(DIFF_W), row(DIFF_W)],
        out_shape=[jax.ShapeDtypeStruct((n, A_COLS), F32),
                   jax.ShapeDtypeStruct((n, LANES), F32),
                   jax.ShapeDtypeStruct((n, DIFF_W), BF16),
                   jax.ShapeDtypeStruct((n, DIFF_W), BF16),
                   jax.ShapeDtypeStruct((n, DIFF_W), BF16)],
        compiler_params=pltpu.CompilerParams(dimension_semantics=("arbitrary",),
                                             vmem_limit_bytes=VMEM_LIMIT),
        name="inproj",
    )(x2d, w_r, ctab, satab, sbtab)


def _mm(a, b):
    return jnp.dot(a.astype(BF16), b.astype(BF16), preferred_element_type=F32)


def _gdn_kernel(a_ref, ba_ref, cw_ref, aux_ref, nw_ref, o_ref, xe_ref, q_s, k_s, v_s, state_ref):
    t_rows = a_ref.shape[0]
    st = pl.program_id(1)

    @pl.when(st == 0)
    def _():
        xe_ref[0:8, :] = jnp.zeros((8, CONV_CH), F32)
        state_ref[...] = jnp.zeros(state_ref.shape, F32)

    xe_ref[8:8 + t_rows, :] = a_ref[:, :CONV_CH]
    for s in range(CONV_CH // LANES):
        cs = slice(LANES * s, LANES * (s + 1))
        y = jnp.zeros((t_rows, LANES), F32)
        for j in range(CONV_WIDTH):
            off = 8 - (CONV_WIDTH - 1) + j
            y = y + xe_ref[off:off + t_rows, cs] * cw_ref[j:j + 1, cs]
        y = y * _sigmoid(y)
        grp, h = divmod(s, GDN_HEADS)
        hs = slice(LANES * h, LANES * (h + 1))
        if grp == 0:
            q_s[:, hs] = y * lax.rsqrt(jnp.sum(y * y, -1, keepdims=True) + 1e-6) * (GDN_DK ** -0.5)
        elif grp == 1:
            k_s[:, hs] = y * lax.rsqrt(jnp.sum(y * y, -1, keepdims=True) + 1e-6)
        else:
            v_s[:, hs] = y
    xe_ref[0:8, :] = xe_ref[t_rows:t_rows + 8, :]

    ba = ba_ref[...]
    beta_t = _sigmoid(ba)
    gx = ba + aux_ref[1:2, :]
    g_t = -jnp.exp(aux_ref[0:1, :]) * (jnp.maximum(gx, 0.0) + jnp.log(1.0 + jnp.exp(-jnp.abs(gx))))

    ri = lax.broadcasted_iota(jnp.int32, (HSTACK, HSTACK), 0)
    ci = lax.broadcasted_iota(jnp.int32, (HSTACK, HSTACK), 1)
    head_start = ri - (ri & (CHUNK - 1))
    in_head = ci >= head_start
    incl_f = jnp.where(in_head, jnp.where(ci <= ri, 1.0, 0.0), 0.0)
    strict_f = jnp.where(in_head, jnp.where(ci < ri, 1.0, 0.0), 0.0)
    sub_f = jnp.where(ci >= ri - (ri & (SUBBLK - 1)), 1.0, 0.0)
    eye = jnp.where(ri == ci, 1.0, 0.0)
    tri_b = (lax.broadcasted_iota(jnp.int32, (CHUNK, CHUNK), 0)
             >= lax.broadcasted_iota(jnp.int32, (CHUNK, CHUNK), 1)).astype(BF16)
    nt = (((1,), (1,)), ((), ()))
    tn = (((0,), (0,)), ((), ()))
    bdot = functools.partial(jnp.dot, preferred_element_type=F32)

    def stack(fn):
        return jnp.concatenate([fn(h) for h in range(GDN_HEADS)], axis=0)

    chunks = range(t_rows // CHUNK)
    crow = [slice(CHUNK * c, CHUNK * (c + 1)) for c in chunks]

    def cumdecay(g):
        g1 = g.astype(BF16)
        r1 = g - g1.astype(F32)
        g2 = r1.astype(BF16)
        g3 = (r1 - g2.astype(F32)).astype(BF16)
        return bdot(tri_b, g1) + bdot(tri_b, g2) + bdot(tri_b, g3)

    gc_l = [cumdecay(g_t[crow[c]]) for c in chunks]
    gct_l = [gc_l[c].T for c in chunks]
    gcol_l = [stack(lambda h: jnp.broadcast_to(gc_l[c][:, GDN_HEADS + h:GDN_HEADS + h + 1], (CHUNK, LANES)))
              for c in chunks]
    glast_l = [stack(lambda h: jnp.broadcast_to(gc_l[c][CHUNK - 1:CHUNK, GDN_HEADS + h:GDN_HEADS + h + 1],
                                                (CHUNK, LANES))) for c in chunks]
    bcol_l = [stack(lambda h: jnp.broadcast_to(beta_t[crow[c], h:h + 1], (CHUNK, LANES))) for c in chunks]
    grow_l = [jnp.concatenate([gct_l[c][GDN_HEADS + h:GDN_HEADS + h + 1, :] for h in range(GDN_HEADS)], axis=1)
              for c in chunks]
    kk_l = [stack(lambda h: k_s[crow[c], LANES * h:LANES * (h + 1)]) for c in chunks]
    qq_l = [stack(lambda h: q_s[crow[c], LANES * h:LANES * (h + 1)]) for c in chunks]
    vv_l = [stack(lambda h: v_s[crow[c], LANES * h:LANES * (h + 1)]) for c in chunks]
    dec_l = [jnp.exp(jnp.minimum(jnp.concatenate([gcol_l[c], gcol_l[c]], axis=1) - grow_l[c], 0.0)) * incl_f
             for c in chunks]
    kb_l = [kk_l[c] * bcol_l[c] for c in chunks]
    k16_l = [kk_l[c].astype(BF16) for c in chunks]
    amat_l = [lax.dot_general(kb_l[c].astype(BF16), k16_l[c], nt, preferred_element_type=F32)
              * dec_l[c] * strict_f for c in chunks]
    bm_l = [amat_l[c] * sub_f for c in chunks]
    nm_l = [amat_l[c] - bm_l[c] for c in chunks]
    b2_l = [_mm(bm_l[c], bm_l[c]) for c in chunks]
    b4_l = [_mm(b2_l[c], b2_l[c]) for c in chunks]
    b8_l = [_mm(b4_l[c], b4_l[c]) for c in chunks]
    d_l = [_mm(eye - bm_l[c], eye + b2_l[c]) for c in chunks]
    d_l = [_mm(d_l[c], eye + b4_l[c]) for c in chunks]
    dinv_l = [_mm(d_l[c], eye + b8_l[c]) for c in chunks]
    mm_l = [_mm(dinv_l[c], nm_l[c]) for c in chunks]
    m2_l = [_mm(mm_l[c], mm_l[c]) for c in chunks]
    t_l = [_mm(eye - mm_l[c], eye + m2_l[c]) for c in chunks]
    tinv_l = [_mm(t_l[c], dinv_l[c]) for c in chunks]
    eg_l = [jnp.exp(gcol_l[c]) for c in chunks]
    sol_l = [_mm(tinv_l[c], jnp.concatenate([vv_l[c] * bcol_l[c], kb_l[c] * eg_l[c]], axis=1)) for c in chunks]
    qk_l = [lax.dot_general(qq_l[c].astype(BF16), k16_l[c], nt, preferred_element_type=F32) * dec_l[c]
            for c in chunks]
    qd_l = [qq_l[c] * eg_l[c] for c in chunks]
    kd_l = [kk_l[c] * jnp.exp(glast_l[c] - gcol_l[c]) for c in chunks]

    for c in chunks:
        rows = crow[c]
        gc = gc_l[c]
        u = sol_l[c][:, :LANES]
        w = sol_l[c][:, LANES:]
        qk, qd, kd = qk_l[c], qd_l[c], kd_l[c]

        ws, qs = [], []
        for h in range(GDN_HEADS):
            hr = slice(CHUNK * h, CHUNK * (h + 1))
            lhs = jnp.concatenate([w[hr], qd[hr]], axis=0).astype(BF16)
            r = bdot(lhs, state_ref[h].astype(BF16))
            ws.append(r[:CHUNK])
            qs.append(r[CHUNK:])
        vn = u - jnp.concatenate(ws, axis=0)
        vn16 = vn.astype(BF16)
        o = jnp.concatenate(qs, axis=0) + bdot(qk.astype(BF16), vn16)
        kd16 = kd.astype(BF16)
        for h in range(GDN_HEADS):
            hr = slice(CHUNK * h, CHUNK * (h + 1))
            hs = slice(LANES * h, LANES * (h + 1))
            gl = jnp.exp(gc[CHUNK - 1:CHUNK, GDN_HEADS + h:GDN_HEADS + h + 1])
            state_ref[h] = state_ref[h] * gl + lax.dot_general(kd16[hr], vn16[hr], tn,
                                                               preferred_element_type=F32)
            oh = o[hr]
            z = a_ref[rows, CONV_CH + LANES * h:CONV_CH + LANES * (h + 1)]
            oh = oh * lax.rsqrt(jnp.mean(oh * oh, -1, keepdims=True) + RMS_EPS) * nw_ref[...]
            o_ref[rows, hs] = (oh * (z * _sigmoid(z))).astype(o_ref.dtype)


def _gdn(a_proj, ba_proj, conv_w, a_log, dt_bias, norm_w, batch, seq):
    t = T_GDN
    nst = seq // t
    rowblk = lambda w: pl.BlockSpec((t, w), lambda b, s: (b * nst + s, 0))
    aux = jnp.zeros((8, LANES), F32)
    aux = aux.at[0, GDN_HEADS:2 * GDN_HEADS].set(a_log).at[1, GDN_HEADS:2 * GDN_HEADS].set(dt_bias)
    return pl.pallas_call(
        _gdn_kernel,
        grid=(batch, nst),
        in_specs=[rowblk(A_COLS), rowblk(LANES),
                  pl.BlockSpec((CONV_WIDTH, CONV_CH), lambda b, s: (0, 0)),
                  pl.BlockSpec((8, LANES), lambda b, s: (0, 0)),
                  pl.BlockSpec((1, GDN_DV), lambda b, s: (0, 0))],
        out_specs=rowblk(GDN_W),
        out_shape=jax.ShapeDtypeStruct((batch * seq, GDN_W), BF16),
        scratch_shapes=[pltpu.VMEM((t + 8, CONV_CH), F32),
                        pltpu.VMEM((t, GDN_W), F32),
                        pltpu.VMEM((t, GDN_W), F32),
                        pltpu.VMEM((t, GDN_W), F32),
                        pltpu.VMEM((GDN_HEADS, GDN_DK, GDN_DV), F32)],
        compiler_params=pltpu.CompilerParams(dimension_semantics=("arbitrary", "arbitrary"),
                                             vmem_limit_bytes=VMEM_LIMIT),
        name="gdn",
    )(a_proj, ba_proj, conv_w, aux, norm_w.reshape(1, GDN_DV))


def _attn_kernel(q_ref, k_ref, vt_ref, lq1_ref, lk1_ref, lq2_ref, lk2_ref, nw_ref, o_ref, *, lam_init):
    tq = q_ref.shape[0]
    qi = pl.program_id(1)
    lane = lax.broadcasted_iota(jnp.int32, (1, LANES), 1)
    qpos = qi * tq + (lax.broadcasted_iota(jnp.int32, (1, 2 * tq), 1) & (tq - 1))
    nt = (((1,), (1,)), ((), ()))
    q2 = []
    for h in range(DIFF_HEADS):
        q = q_ref[:, LANES * h:LANES * (h + 1)]
        zero = jnp.zeros_like(q)
        q2.append(jnp.concatenate([jnp.where(lane < DIFF_D, q, zero), jnp.where(lane >= DIFF_D, q, zero)], axis=0))

    heads = range(DIFF_HEADS)

    def scores(kj):
        off = pl.multiple_of(kj * TKV, TKV)
        return tuple(lax.dot_general(k_ref[pl.ds(off, TKV), LANES * h:LANES * (h + 1)], q2[h], nt,
                                     preferred_element_type=F32) for h in heads)

    def update(kj, ss, state, masked):
        off = pl.multiple_of(kj * TKV, TKV)
        if masked:
            keep = off + lax.broadcasted_iota(jnp.int32, (TKV, 1), 0) <= qpos
            ss = [jnp.where(keep, s, -1e30) for s in ss]
        mns = [jnp.maximum(state[h][0], jnp.max(ss[h], 0, keepdims=True)) for h in heads]
        ps = [jnp.exp2((ss[h] - mns[h]).astype(BF16)) for h in heads]
        als = [jnp.exp2(state[h][0] - mns[h]) for h in heads]
        pvs = [jnp.dot(vt_ref[VT_ROWS * h:VT_ROWS * (h + 1), pl.ds(off, TKV)], ps[h],
                       preferred_element_type=F32) for h in heads]
        return tuple((mns[h], als[h] * state[h][1] + pvs[h]) for h in heads)

    def body(kj, state):
        return update(kj, scores(kj), state, False)

    init = tuple((jnp.full((1, 2 * tq), -1e30, F32), jnp.zeros((VT_ROWS, 2 * tq), F32)) for _ in heads)
    state = lax.fori_loop(0, qi, body, init)
    carry = update(qi, scores(qi), state, True)
    lam = (jnp.exp(jnp.sum(lq1_ref[...] * lk1_ref[...], -1, keepdims=True))
           - jnp.exp(jnp.sum(lq2_ref[...] * lk2_ref[...], -1, keepdims=True)) + lam_init)
    for h in range(DIFF_HEADS):
        m, acc = carry[h]
        on = acc[:DIFF_DV] * (1.0 / acc[DIFF_DV:DIFF_DV + 1])
        o = (on[:, :tq] - lam * on[:, tq:]).T
        o = o * lax.rsqrt(jnp.mean(o * o, -1, keepdims=True) + RMS_EPS) * nw_ref[...] * (1.0 - lam_init)
        o_ref[:, LANES * h:LANES * (h + 1)] = o.astype(o_ref.dtype)


def _diff_attention(q, k, v, lq1, lk1, lq2, lk2, norm_w, lam_init, batch, seq):
    q3 = q.reshape(batch, seq, DIFF_W)
    k3 = k.reshape(batch, seq, DIFF_W)
    v4 = v.reshape(batch, seq, DIFF_HEADS, DIFF_DV).transpose(0, 2, 3, 1)
    ones = jnp.ones((batch, DIFF_HEADS, VT_ROWS - DIFF_DV, seq), v.dtype)
    v3 = jnp.concatenate([v4, ones], axis=2).reshape(batch, DIFF_HEADS * VT_ROWS, seq)
    small = lambda w: pl.BlockSpec((1, w), lambda b, i: (0, 0))
    out = pl.pallas_call(
        functools.partial(_attn_kernel, lam_init=lam_init),
        grid=(batch, seq // TQ),
        in_specs=[pl.BlockSpec((None, TQ, DIFF_W), lambda b, i: (b, i, 0)),
                  pl.BlockSpec((None, seq, DIFF_W), lambda b, i: (b, 0, 0)),
                  pl.BlockSpec((None, DIFF_HEADS * VT_ROWS, seq), lambda b, i: (b, 0, 0)),
                  small(DIFF_D), small(DIFF_D), small(DIFF_D), small(DIFF_D), small(DIFF_DV)],
        out_specs=pl.BlockSpec((None, TQ, DIFF_W), lambda b, i: (b, i, 0)),
        out_shape=jax.ShapeDtypeStruct((batch, seq, DIFF_W), BF16),
        compiler_params=pltpu.CompilerParams(
            dimension_semantics=("arbitrary", "arbitrary"),
            vmem_limit_bytes=VMEM_LIMIT),
        name="diff_attn",
    )(q3, k3, v3, lq1.reshape(1, -1), lk1.reshape(1, -1), lq2.reshape(1, -1), lk2.reshape(1, -1),
      norm_w.reshape(1, -1))
    return out.reshape(batch * seq, DIFF_W)


def _outproj_kernel(og_ref, od_ref, x_ref, wo_ref, g_ref, b_ref, rw_ref, rb_ref,
                    h_ref, idx_ref, gate_ref, cnt_out_ref, tcnt_ref, cnt_ref):
    @pl.when(pl.program_id(0) == 0)
    def _():
        cnt_ref[...] = jnp.zeros(cnt_ref.shape, F32)

    mix = (jnp.dot(og_ref[...], wo_ref[:GDN_W, :], preferred_element_type=F32)
           + jnp.dot(od_ref[...], wo_ref[GDN_W:, :], preferred_element_type=F32))
    h = _layer_norm(DN_ALPHA * x_ref[...] + mix, g_ref[...], b_ref[...])
    h_ref[...] = h
    h_hi = h.astype(BF16)
    h_lo = (h - h_hi.astype(F32)).astype(BF16)
    logits = jnp.dot(jnp.concatenate([h_hi, h_hi, h_lo], axis=1), rw_ref[...],
                     preferred_element_type=F32) + rb_ref[...]
    tm = logits.shape[0]
    lane = lax.broadcasted_iota(jnp.int32, (tm, LANES), 1)
    lane_f = lane.astype(F32)
    work = logits
    vals, idxs = [], []
    for _ in range(TOP_K):
        m = jnp.max(work, -1, keepdims=True)
        sel = jnp.min(jnp.where(work == m, lane_f, float(LANES)), -1, keepdims=True)
        vals.append(m)
        idxs.append(sel)
        work = jnp.where(lane_f == sel, -jnp.inf, work)
    exps = [jnp.exp(vv - vals[0]) for vv in vals]
    inv = 1.0 / (exps[0] + exps[1] + exps[2] + exps[3])

    hot = jnp.zeros((tm, LANES), F32)
    for j in range(TOP_K):
        hot = hot + jnp.where(lane_f == idxs[j], 1.0, 0.0)
    ri = lax.broadcasted_iota(jnp.int32, (tm, tm), 0)
    ci = lax.broadcasted_iota(jnp.int32, (tm, tm), 1)
    before = jnp.where(ci < ri, 1.0, 0.0).astype(BF16)
    prefix = jnp.dot(before, hot.astype(BF16), preferred_element_type=F32) + cnt_ref[...]
    cnt_ref[...] = cnt_ref[...] + jnp.sum(hot, 0, keepdims=True)
    cnt_out_ref[...] = cnt_ref[...].astype(jnp.int32)
    starts = [prefix[TC_COMB * j:TC_COMB * j + 1, :] for j in range(tm // TC_COMB)]
    pad = jnp.zeros((tcnt_ref.shape[0] - len(starts), LANES), F32)
    tcnt_ref[...] = jnp.concatenate(starts + [pad], axis=0).astype(jnp.int32)

    idx_out = jnp.zeros((tm, LANES), F32)
    gate_out = jnp.zeros((tm, LANES), F32)
    for j in range(TOP_K):
        rank = jnp.sum(jnp.where(lane_f == idxs[j], prefix, 0.0), -1, keepdims=True)
        idx_out = jnp.where(lane == j, idxs[j], idx_out)
        idx_out = jnp.where(lane == TOP_K + j, rank, idx_out)
        gate_out = jnp.where(lane == j, exps[j] * inv, gate_out)
    idx_ref[...] = idx_out.astype(jnp.int32)
    gate_ref[...] = gate_out


def _outproj(og, od, x2d, w_out_b, ln_g, ln_b, rw_pad, rb_pad):
    n = x2d.shape[0]
    tm = TM_PROJ
    row = lambda w: pl.BlockSpec((tm, w), lambda i: (i, 0))
    full = lambda r, c: pl.BlockSpec((r, c), lambda i: (0, 0))
    return pl.pallas_call(
        _outproj_kernel,
        grid=(n // tm,),
        in_specs=[row(GDN_W), row(DIFF_W), row(D_MODEL), full(GDN_W + DIFF_W, D_MODEL),
                  full(1, D_MODEL), full(1, D_MODEL), full(3 * D_MODEL, LANES), full(1, LANES)],
        out_specs=[row(D_MODEL), row(LANES), row(LANES), full(1, LANES),
                   pl.BlockSpec((8, LANES), lambda i: (i, 0))],
        out_shape=[jax.ShapeDtypeStruct((n, D_MODEL), F32),
                   jax.ShapeDtypeStruct((n, LANES), jnp.int32),
                   jax.ShapeDtypeStruct((n, LANES), F32),
                   jax.ShapeDtypeStruct((1, LANES), jnp.int32),
                   jax.ShapeDtypeStruct((n // tm * 8, LANES), jnp.int32)],
        scratch_shapes=[pltpu.VMEM((1, LANES), F32)],
        compiler_params=pltpu.CompilerParams(dimension_semantics=("arbitrary",),
                                             vmem_limit_bytes=VMEM_LIMIT),
        name="outproj_ln1_router",
    )(og, od, x2d, w_out_b, ln_g.reshape(1, -1), ln_b.reshape(1, -1), rw_pad, rb_pad)


def _dispatch_kernel(pos_ref, h_ref, xs_hbm, sem):
    tc = h_ref.shape[0]
    base = pl.program_id(0) * (tc * TOP_K)

    def body(r, carry):
        for j in range(TOP_K):
            s = pos_ref[base + r * TOP_K + j]
            pltpu.make_async_copy(h_ref.at[pl.ds(r, 1), :], xs_hbm.at[pl.ds(s, 1), :], sem).start()
        return carry

    lax.fori_loop(0, tc, body, 0, unroll=4)
    for j in range(TOP_K):
        pltpu.make_async_copy(h_ref, xs_hbm.at[pl.ds(0, tc), :], sem).wait()


def _dispatch(pos, h1):
    n = h1.shape[0]
    tc = TC_DISP
    grid_spec = pltpu.PrefetchScalarGridSpec(
        num_scalar_prefetch=1,
        grid=(n // tc,),
        in_specs=[pl.BlockSpec((tc, D_MODEL), lambda i, pos: (i, 0))],
        out_specs=pl.BlockSpec(memory_space=pl.ANY),
        scratch_shapes=[pltpu.SemaphoreType.DMA],
    )
    return pl.pallas_call(
        _dispatch_kernel,
        grid_spec=grid_spec,
        out_shape=jax.ShapeDtypeStruct((n * TOP_K, D_MODEL), F32),
        compiler_params=pltpu.CompilerParams(dimension_semantics=("arbitrary",),
                                             vmem_limit_bytes=VMEM_LIMIT),
        name="dispatch",
    )(pos, h1)


def _moe_kernel(we_ref, wt_ref, lo_ref, hi_ref, nw_ref, xs_ref, wgu_ref, bgu_ref, wd_ref, bd_ref, ys_ref,
                wgu_b, wd_b, acc_ref):
    tm = ys_ref.shape[0]
    w = pl.program_id(0)

    @pl.when(w == 0)
    def _():
        acc_ref[...] = jnp.zeros(acc_ref.shape, F32)

    @pl.when(w < nw_ref[0])
    def _():
        prev = jnp.maximum(w - 1, 0)

        @pl.when((w == 0) | (we_ref[w] != we_ref[prev]))
        def _():
            rows = 128

            def cast(j, carry):
                r0 = pl.multiple_of(j * rows, rows)
                wgu_b[pl.ds(r0, rows), :] = wgu_ref[0, pl.ds(r0, rows), :].astype(BF16)
                wd_b[pl.ds(r0, rows), :] = wd_ref[0, pl.ds(r0, rows), :].astype(BF16)
                return carry

            lax.fori_loop(0, D_MODEL // rows, cast, 0)

        xb = xs_ref[...].astype(BF16)
        hgu = jnp.dot(xb, wgu_b[...], preferred_element_type=F32) + bgu_ref[0]
        gate = jnp.minimum(hgu[:, :D_FF], SWIGLU_LIMIT)
        up = jnp.clip(hgu[:, D_FF:], -SWIGLU_LIMIT, SWIGLU_LIMIT)
        act = (up + 1.0) * gate * _sigmoid(SWIGLU_ALPHA * gate)
        y = jnp.dot(act.astype(BF16), wd_b[...], preferred_element_type=F32) + bd_ref[0]
        rid = lax.broadcasted_iota(jnp.int32, (tm, 1), 0)
        mine = jnp.where(rid >= lo_ref[w], jnp.where(rid < hi_ref[w], 1.0, 0.0), 0.0) > 0.5
        keep = jnp.where((w == 0) | (wt_ref[w] != wt_ref[prev]), 0.0, 1.0)
        merged = jnp.where(mine, y, acc_ref[...] * keep)
        acc_ref[...] = merged
        ys_ref[...] = merged


def _moe(work, xs, w_gu, b_gu, w_down, b_down):
    tm = TM_MOE
    m = xs.shape[0]
    n_items = m // tm + N_EXPERTS - 1
    widx = lambda f: (lambda w, we, wt, lo, hi, nw: f(w, we, wt))
    grid_spec = pltpu.PrefetchScalarGridSpec(
        num_scalar_prefetch=5,
        grid=(n_items,),
        in_specs=[pl.BlockSpec((tm, D_MODEL), widx(lambda w, we, wt: (wt[w], 0))),
                  pl.BlockSpec((1, D_MODEL, 2 * D_FF), widx(lambda w, we, wt: (we[w], 0, 0))),
                  pl.BlockSpec((1, 1, 2 * D_FF), widx(lambda w, we, wt: (we[w], 0, 0))),
                  pl.BlockSpec((1, D_FF, D_MODEL), widx(lambda w, we, wt: (we[w], 0, 0))),
                  pl.BlockSpec((1, 1, D_MODEL), widx(lambda w, we, wt: (we[w], 0, 0)))],
        out_specs=pl.BlockSpec((tm, D_MODEL), widx(lambda w, we, wt: (wt[w], 0))),
        scratch_shapes=[pltpu.VMEM((D_MODEL, 2 * D_FF), BF16),
                        pltpu.VMEM((D_FF, D_MODEL), BF16),
                        pltpu.VMEM((tm, D_MODEL), F32)],
    )
    return pl.pallas_call(
        _moe_kernel,
        grid_spec=grid_spec,
        out_shape=jax.ShapeDtypeStruct((m, D_MODEL), F32),
        compiler_params=pltpu.CompilerParams(dimension_semantics=("arbitrary",),
                                             vmem_limit_bytes=VMEM_LIMIT),
        name="moe_ffn",
    )(*work, xs, w_gu, b_gu.reshape(N_EXPERTS, 1, -1), w_down, b_down.reshape(N_EXPERTS, 1, -1))


def _combine_kernel(cs_ref, nch_ref, h_ref, gate_ref, pos_ref, bs_ref, p_ref, ys_hbm, g2_ref, b2_ref, wg_ref,
                    bg_ref, wp_ref, g3_ref, b3_ref, o_ref, ybuf, sem):
    i = pl.program_id(0)
    n = pl.num_programs(0)

    def issue(t, b):
        def body(c, carry):
            dst = pl.multiple_of(c * COMB_CH, COMB_CH)
            src = pl.multiple_of(cs_ref[t * COMB_MAXCH + c], SUBLANES)
            pltpu.make_async_copy(ys_hbm.at[pl.ds(src, COMB_CH), :],
                                  ybuf.at[b, pl.ds(dst, COMB_CH), :], sem.at[b]).start()
            return carry

        lax.fori_loop(0, nch_ref[t], body, 0)

    @pl.when(i == 0)
    def _():
        def zero(j, carry):
            r0 = pl.multiple_of(j * COMB_CH, COMB_CH)
            for b in range(2):
                ybuf[b, pl.ds(r0, COMB_CH), :] = jnp.zeros((COMB_CH, D_MODEL), F32)
            return carry

        lax.fori_loop(0, COMB_MAXCH, zero, 0)
        issue(0, 0)

    @pl.when(i + 1 < n)
    def _():
        issue(i + 1, (i + 1) % 2)

    buf = i % 2

    def wait(c, carry):
        pltpu.make_async_copy(ys_hbm.at[pl.ds(0, COMB_CH), :], ybuf.at[buf, pl.ds(0, COMB_CH), :],
                              sem.at[buf]).wait()
        return carry

    lax.fori_loop(0, nch_ref[i], wait, 0)
    gates = gate_ref[...]
    slots = bs_ref[0]
    sel = jnp.zeros((h_ref.shape[0], COMB_MAXCH * COMB_CH), F32)
    for j in range(TOP_K):
        sel = jnp.where(pos_ref[:, j:j + 1] == slots, gates[:, j:j + 1], sel)
    ffn = jnp.dot(sel.astype(BF16), ybuf[buf].astype(BF16), preferred_element_type=F32)
    h2 = _layer_norm(DN_ALPHA * h_ref[...] + ffn, g2_ref[...], b2_ref[...])
    gate = _sigmoid(jnp.dot(h2.astype(BF16), wg_ref[...], preferred_element_type=F32) + bg_ref[...])
    ple = gate * jnp.dot(p_ref[...].astype(BF16), wp_ref[...], preferred_element_type=F32)
    o_ref[...] = _layer_norm(DN_ALPHA * h2 + ple, g3_ref[...], b3_ref[...])


def _combine(chunks, pos2d, h1, gates, p2d, ys, g2, b2, wg_b, bg, wp_b, g3, b3):
    n = h1.shape[0]
    tc = TC_COMB
    cstart, nch, bufslot = chunks
    row = lambda w: pl.BlockSpec((tc, w), lambda i, cs, nc: (i, 0))
    full = lambda r, c: pl.BlockSpec((r, c), lambda i, cs, nc: (0, 0))
    grid_spec = pltpu.PrefetchScalarGridSpec(
        num_scalar_prefetch=2,
        grid=(n // tc,),
        in_specs=[row(D_MODEL), row(LANES), row(TOP_K),
                  pl.BlockSpec((None, 1, COMB_MAXCH * COMB_CH), lambda i, cs, nc: (i, 0, 0)),
                  row(PLE_DIM), pl.BlockSpec(memory_space=pl.ANY),
                  full(1, D_MODEL), full(1, D_MODEL), full(D_MODEL, D_MODEL), full(1, D_MODEL),
                  full(PLE_DIM, D_MODEL), full(1, D_MODEL), full(1, D_MODEL)],
        out_specs=row(D_MODEL),
        scratch_shapes=[pltpu.VMEM((2, COMB_MAXCH * COMB_CH, D_MODEL), F32),
                        pltpu.SemaphoreType.DMA((2,))],
    )
    return pl.pallas_call(
        _combine_kernel,
        grid_spec=grid_spec,
        out_shape=jax.ShapeDtypeStruct((n, D_MODEL), F32),
        compiler_params=pltpu.CompilerParams(dimension_semantics=("arbitrary",),
                                             vmem_limit_bytes=VMEM_LIMIT),
        name="combine_ln2_ple_ln3",
    )(cstart, nch, h1, gates, pos2d, bufslot, p2d, ys, g2.reshape(1, -1), b2.reshape(1, -1), wg_b,
      bg.reshape(1, -1), wp_b, g3.reshape(1, -1), b3.reshape(1, -1))


def _combine_chunks(cntb, counts, grp_start, m):
    n_tiles = cntb.shape[0]
    nxt = jnp.concatenate([cntb[1:], counts[None, :]], axis=0)
    length = nxt - cntb
    first = grp_start[None, :] + cntb
    base = (first // SUBLANES) * SUBLANES
    nq = jnp.where(length > 0, (first - base + length + COMB_CH - 1) // COMB_CH, 0)
    q_end = jnp.cumsum(nq, axis=1)
    q_start = q_end - nq
    nch = q_end[:, -1]
    c = jnp.arange(COMB_MAXCH, dtype=jnp.int32)
    e_c = jnp.clip(jnp.sum((q_end[:, None, :] <= c[None, :, None]).astype(jnp.int32), -1), 0, N_EXPERTS - 1)
    onehot = e_c[:, :, None] == jnp.arange(N_EXPERTS, dtype=jnp.int32)
    pick = lambda a: jnp.sum(jnp.where(onehot, a[:, None, :], 0), -1)
    want = pick(base) + COMB_CH * (c[None, :] - pick(q_start))
    start = jnp.minimum(want, m - COMB_CH)
    valid = c[None, :] < nch[:, None]
    cstart = jnp.where(valid, start, 0).astype(jnp.int32).reshape(-1)
    rows = start[:, :, None] + jnp.arange(COMB_CH, dtype=jnp.int32)
    lo = jnp.maximum(want, pick(first))[:, :, None]
    end = (pick(first) + pick(length))[:, :, None]
    own = valid[:, :, None] & (rows >= lo) & (rows < end)
    bufslot = jnp.where(own, rows, -1).astype(jnp.int32).reshape(n_tiles, 1, COMB_MAXCH * COMB_CH)
    return cstart, nch.astype(jnp.int32), bufslot


def _routing(idx, rank, counts, tm):
    n_tok = idx.shape[0]
    m = n_tok * TOP_K
    n_tiles = m // tm
    n_items = n_tiles + N_EXPERTS - 1
    grp_end = jnp.cumsum(counts)
    grp_start = grp_end - counts
    experts = jnp.arange(N_EXPERTS, dtype=jnp.int32)
    pos = (jnp.sum(jnp.where(idx[:, :, None] == experts, grp_start, 0), -1) + rank).reshape(-1).astype(jnp.int32)
    t_first = grp_start // tm
    t_last = (grp_end - 1) // tm
    n_e = jnp.where(counts > 0, t_last - t_first + 1, 0)
    w_end = jnp.cumsum(n_e)
    w_start = w_end - n_e
    n_work = w_end[-1]
    w = jnp.arange(n_items, dtype=jnp.int32)
    valid = w < n_work
    wq = jnp.minimum(w, n_work - 1)
    e_w = jnp.clip(jnp.sum((w_end[None, :] <= wq[:, None]).astype(jnp.int32), -1), 0, N_EXPERTS - 1)
    onehot = e_w[:, None] == experts
    pick = lambda a: jnp.sum(jnp.where(onehot, a[None, :], 0), -1)
    tile_w = pick(t_first) + (wq - pick(w_start))
    lo = jnp.clip(pick(grp_start) - tile_w * tm, 0, tm)
    hi = jnp.clip(pick(grp_end) - tile_w * tm, 0, tm)
    lo = jnp.where(valid, lo, 0)
    hi = jnp.where(valid, hi, 0)
    i32 = lambda a: a.astype(jnp.int32)
    return pos, i32(grp_start), (i32(e_w), i32(tile_w), i32(lo), i32(hi), i32(n_work).reshape(1))


def _layer(h, p_i, cos8, sin8, lam_init, w_in, conv_w, a_log, dt_bias, gdn_norm_w,
           lam_q1, lam_k1, lam_q2, lam_k2, diff_norm_w, w_out, ln1_g, ln1_b,
           router_w, router_b, w_gu, b_gu, w_down, b_down, ln2_g, ln2_b,
           ple_w, ple_gate_w, ple_gate_b, ln3_g, ln3_b):
    batch, seq, d = h.shape
    n = batch * seq
    x2d = h.reshape(n, d)
    o_ba = 4 * GDN_W
    o_d = o_ba + 2 * GDN_HEADS
    w_r = jnp.concatenate([w_in[:, :o_ba], w_in[:, o_d:], w_in[:, o_ba:o_d],
                           jnp.zeros((d, LANES - 2 * GDN_HEADS), w_in.dtype)], -1).astype(BF16)
    ones = jnp.ones((n, DIFF_D - ROPE_DIM), F32)
    zeros = jnp.zeros((n, DIFF_D - ROPE_DIM), F32)
    z8 = jnp.zeros_like(sin8)
    ctab = jnp.tile(jnp.concatenate([cos8, cos8, ones], -1), (1, 2))
    satab = jnp.tile(jnp.concatenate([z8, sin8, zeros], -1), (1, 2))
    sbtab = jnp.tile(jnp.concatenate([-sin8, z8, zeros], -1), (1, 2))
    a_proj, ba_proj, dq, dk, dv = _inproj(x2d, w_r, ctab, satab, sbtab)
    o_gdn = _gdn(a_proj, ba_proj, conv_w, a_log, dt_bias, gdn_norm_w, batch, seq)
    o_diff = _diff_attention(dq, dk, dv, lam_q1, lam_k1, lam_q2, lam_k2, diff_norm_w, lam_init, batch, seq)
    rw_pad = jnp.concatenate([router_w, jnp.zeros((d, LANES - N_EXPERTS), F32)], -1)
    rw_hi = rw_pad.astype(BF16)
    rw_lo = (rw_pad - rw_hi.astype(F32)).astype(BF16)
    rw_pad = jnp.concatenate([rw_hi, rw_lo, rw_hi], axis=0)
    rb_pad = jnp.concatenate([router_b, jnp.full((LANES - N_EXPERTS,), -jnp.inf, F32)]).reshape(1, LANES)
    h1, idx, gates, cnt, tcnt = _outproj(o_gdn, o_diff, x2d, w_out.astype(BF16), ln1_g, ln1_b, rw_pad, rb_pad)
    counts = cnt[0, :N_EXPERTS]
    pos, grp_start, work = _routing(idx[:, :TOP_K], idx[:, TOP_K:2 * TOP_K], counts, TM_MOE)
    xs = _dispatch(pos, h1)
    ys = _moe(work, xs, w_gu, b_gu, w_down, b_down)
    per_step = TM_PROJ // TC_COMB
    cntb = tcnt.reshape(n // TM_PROJ, 8, LANES)[:, :per_step, :N_EXPERTS].reshape(n // TC_COMB, N_EXPERTS)
    chunks = _combine_chunks(cntb, counts, grp_start, n * TOP_K)
    out = _combine(chunks, pos.reshape(n, TOP_K), h1, gates, p_i.reshape(n, PLE_DIM), ys, ln2_g, ln2_b,
                   ple_gate_w.astype(BF16), ple_gate_b, ple_w.astype(BF16), ln3_g, ln3_b)
    return out.reshape(batch, seq, d)


def kernel(x, p, positions, w_in, conv_w, a_log, dt_bias, gdn_norm_w, lam_q1, lam_k1, lam_q2, lam_k2,
           diff_norm_w, w_out, ln1_g, ln1_b, router_w, router_b, w_gu, b_gu, w_down, b_down, ln2_g, ln2_b,
           ple_w, ple_gate_w, ple_gate_b, ln3_g, ln3_b):
    batch, seq, _ = x.shape
    inv_freq = ROPE_THETA ** (-jnp.arange(0, ROPE_DIM, 2, dtype=F32) / ROPE_DIM)
    ang = (positions.astype(F32)[..., None] * inv_freq).reshape(batch * seq, ROPE_DIM // 2)
    cos8 = jnp.cos(ang)
    sin8 = jnp.sin(ang)
    h = x
    for i in range(w_in.shape[0]):
        lam_init = 0.8 - 0.6 * math.exp(-0.3 * i)
        h = _layer(h, p[i], cos8, sin8, lam_init, w_in[i], conv_w[i], a_log[i], dt_bias[i], gdn_norm_w[i],
                   lam_q1[i], lam_k1[i], lam_q2[i], lam_k2[i], diff_norm_w[i], w_out[i], ln1_g[i], ln1_b[i],
                   router_w[i], router_b[i], w_gu[i], b_gu[i], w_down[i], b_down[i], ln2_g[i], ln2_b[i],
                   ple_w[i], ple_gate_w[i], ple_gate_b[i], ln3_g[i], ln3_b[i])
    return h
```

```python
import functools
import math

import jax
import jax.numpy as jnp
import numpy as np
from jax import lax
from jax.experimental import pallas as pl
from jax.experimental.pallas import tpu as pltpu

F32 = jnp.float32
BF16 = jnp.bfloat16
LOG2E = 1.4426950408889634

D_MODEL = 1024
PLE_DIM = 256
GDN_HEADS = 4
GDN_DK = 128
GDN_DV = 128
CONV_WIDTH = 4
CHUNK = 64
DIFF_HEADS = 4
DIFF_D = 64
DIFF_DV = 2 * DIFF_D
ROPE_THETA = 500000.0
ROPE_DIM = DIFF_D // 4
N_EXPERTS = 32
TOP_K = 4
D_FF = D_MODEL
SWIGLU_LIMIT = 7.0
SWIGLU_ALPHA = 1.702
DEPTH = 1
DN_ALPHA = (2 * DEPTH) ** 0.25
LN_EPS = 1e-5
RMS_EPS = 1e-6

LANES = 128
SUBLANES = 8
GDN_W = GDN_HEADS * GDN_DK
CONV_CH = 3 * GDN_W
A_COLS = 4 * GDN_W
DIFF_W = DIFF_HEADS * DIFF_DV
IN_PAD_W = A_COLS + 3 * DIFF_W + LANES
HSTACK = GDN_HEADS * CHUNK
VT_ROWS = DIFF_DV + 16
SUBBLK = 16

VMEM_LIMIT = 56 * 1024 * 1024

TM_PROJ = 512
T_GDN = 512
TQ = 512
TKV = 512
TM_MOE = 256
TC_DISP = 512
TC_COMB = 256
COMB_CH = 32
COMB_MAXCH = 72
assert COMB_MAXCH >= (TC_COMB * TOP_K + N_EXPERTS * (SUBLANES - 1 + COMB_CH - 1)) // COMB_CH
assert (COMB_MAXCH * COMB_CH) % LANES == 0


def _layer_norm(y, g, b):
    mu = jnp.mean(y, -1, keepdims=True)
    d = y - mu
    var = jnp.mean(d * d, -1, keepdims=True)
    return d * lax.rsqrt(var + LN_EPS) * g + b


def _sigmoid(x):
    return 1.0 / (1.0 + jnp.exp(-x))


def _inproj_kernel(x_ref, w_ref, cs_ref, e_ref, a_ref, ba_ref, q_ref, k_ref, vt_ref):
    xb = x_ref[...].astype(BF16)
    a_ref[...] = jnp.dot(xb, w_ref[:, :A_COLS], preferred_element_type=F32)
    ba_ref[...] = jnp.dot(xb, w_ref[:, A_COLS + 3 * DIFF_W:], preferred_element_type=F32)
    cs = cs_ref[...]
    p1 = cs.astype(BF16)
    r1 = cs - p1.astype(F32)
    p2 = r1.astype(BF16)
    p3 = (r1 - p2.astype(F32)).astype(BF16)
    tabs = (jnp.dot(p1, e_ref[...], preferred_element_type=F32) + jnp.dot(p2, e_ref[...], preferred_element_type=F32)
            + jnp.dot(p3, e_ref[...], preferred_element_type=F32))
    lane = lax.broadcasted_iota(jnp.int32, (1, LANES), 1)
    c = tabs[:, :LANES] + jnp.where((lane & (DIFF_D - 1)) >= ROPE_DIM, 1.0, 0.0)
    sa = tabs[:, LANES:2 * LANES]
    sb = tabs[:, 2 * LANES:]

    def rot(t):
        return t * c + pltpu.roll(t, 8, 1) * sa + pltpu.roll(t, LANES - 8, 1) * sb

    qk = jnp.dot(xb, w_ref[:, A_COLS:A_COLS + 2 * DIFF_W], preferred_element_type=F32)
    for h in range(DIFF_HEADS):
        q = qk[:, LANES * h:LANES * (h + 1)]
        q_ref[:, LANES * h:LANES * (h + 1)] = (rot(q) * (DIFF_D ** -0.5 * LOG2E)).astype(BF16)
        k = qk[:, DIFF_W + LANES * h:DIFF_W + LANES * (h + 1)]
        k_ref[:, LANES * h:LANES * (h + 1)] = rot(k).astype(BF16)
    lo = A_COLS + 2 * DIFF_W
    v = jnp.dot(xb, w_ref[:, lo:lo + DIFF_W], preferred_element_type=F32)
    ones = jnp.ones((VT_ROWS - DIFF_DV, v.shape[0]), BF16)
    for h in range(DIFF_HEADS):
        vt_ref[VT_ROWS * h:VT_ROWS * h + DIFF_DV, :] = v[:, DIFF_DV * h:DIFF_DV * (h + 1)].T.astype(BF16)
        vt_ref[VT_ROWS * h + DIFF_DV:VT_ROWS * (h + 1), :] = ones


def _rope_expand():
    half = ROPE_DIM // 2
    e = np.zeros((ROPE_DIM, 3 * LANES), np.float32)
    for lane in range(LANES):
        j = lane % DIFF_D
        if j < ROPE_DIM:
            e[j % half, lane] = 1.0
            if j >= half:
                e[half + j % half, LANES + lane] = 1.0
            else:
                e[half + j % half, 2 * LANES + lane] = -1.0
    return jnp.asarray(e, BF16)


def _inproj(x2d, w_r, cs16, batch, seq):
    n = x2d.shape[0]
    tm = TM_PROJ
    per_b = seq // tm
    row = lambda w: pl.BlockSpec((tm, w), lambda i: (i, 0))
    return pl.pallas_call(
        _inproj_kernel,
        grid=(n // tm,),
        in_specs=[row(D_MODEL),
                  pl.BlockSpec((D_MODEL, IN_PAD_W), lambda i: (0, 0)),
                  row(ROPE_DIM),
                  pl.BlockSpec((ROPE_DIM, 3 * LANES), lambda i: (0, 0))],
        out_specs=[row(A_COLS), row(LANES), row(DIFF_W), row(DIFF_W),
                   pl.BlockSpec((None, DIFF_HEADS * VT_ROWS, tm), lambda i: (i // per_b, 0, i % per_b))],
        out_shape=[jax.ShapeDtypeStruct((n, A_COLS), F32),
                   jax.ShapeDtypeStruct((n, LANES), F32),
                   jax.ShapeDtypeStruct((n, DIFF_W), BF16),
                   jax.ShapeDtypeStruct((n, DIFF_W), BF16),
                   jax.ShapeDtypeStruct((batch, DIFF_HEADS * VT_ROWS, seq), BF16)],
        compiler_params=pltpu.CompilerParams(dimension_semantics=("arbitrary",),
                                             vmem_limit_bytes=VMEM_LIMIT),
        name="inproj",
    )(x2d, w_r, cs16, _rope_expand())


def _mm(a, b):
    return jnp.dot(a.astype(BF16), b.astype(BF16), preferred_element_type=F32)


def _mm16(a, b):
    return jnp.dot(a, b, preferred_element_type=F32).astype(BF16)


def _gdn_kernel(a_ref, ba_ref, cw_ref, aux_ref, nw_ref, o_ref, xe_ref, q_s, k_s, v_s, state_ref):
    t_rows = a_ref.shape[0]
    st = pl.program_id(1)

    @pl.when(st == 0)
    def _():
        xe_ref[0:8, :] = jnp.zeros((8, CONV_CH), F32)
        state_ref[...] = jnp.zeros(state_ref.shape, F32)

    xe_ref[8:8 + t_rows, :] = a_ref[:, :CONV_CH]
    for s in range(CONV_CH // LANES):
        cs = slice(LANES * s, LANES * (s + 1))
        y = jnp.zeros((t_rows, LANES), F32)
        for j in range(CONV_WIDTH):
            off = 8 - (CONV_WIDTH - 1) + j
            y = y + xe_ref[off:off + t_rows, cs] * cw_ref[j:j + 1, cs]
        y = y * _sigmoid(y)
        grp, h = divmod(s, GDN_HEADS)
        hs = slice(LANES * h, LANES * (h + 1))
        if grp == 0:
            q_s[:, hs] = y * lax.rsqrt(jnp.sum(y * y, -1, keepdims=True) + 1e-6) * (GDN_DK ** -0.5)
        elif grp == 1:
            k_s[:, hs] = y * lax.rsqrt(jnp.sum(y * y, -1, keepdims=True) + 1e-6)
        else:
            v_s[:, hs] = y
    xe_ref[0:8, :] = xe_ref[t_rows:t_rows + 8, :]

    ba = ba_ref[...]
    beta_t = _sigmoid(ba)
    gx = ba + aux_ref[1:2, :]
    g_t = -jnp.exp(aux_ref[0:1, :]) * (jnp.maximum(gx, 0.0) + jnp.log(1.0 + jnp.exp(-jnp.abs(gx))))

    ri = lax.broadcasted_iota(jnp.int32, (HSTACK, HSTACK), 0)
    ci = lax.broadcasted_iota(jnp.int32, (HSTACK, HSTACK), 1)
    head_start = ri - (ri & (CHUNK - 1))
    in_head = ci >= head_start
    incl_f = jnp.where(in_head, jnp.where(ci <= ri, 1.0, 0.0), 0.0)
    strict_f = jnp.where(in_head, jnp.where(ci < ri, 1.0, 0.0), 0.0)
    sub_f = jnp.where(ci >= ri - (ri & (SUBBLK - 1)), 1.0, 0.0)
    eye = jnp.where(ri == ci, 1.0, 0.0)
    tri_b = (lax.broadcasted_iota(jnp.int32, (CHUNK, CHUNK), 0)
             >= lax.broadcasted_iota(jnp.int32, (CHUNK, CHUNK), 1)).astype(BF16)
    nt = (((1,), (1,)), ((), ()))
    tn = (((0,), (0,)), ((), ()))
    bdot = functools.partial(jnp.dot, preferred_element_type=F32)

    def stack(fn):
        return jnp.concatenate([fn(h) for h in range(GDN_HEADS)], axis=0)

    chunks = range(t_rows // CHUNK)
    crow = [slice(CHUNK * c, CHUNK * (c + 1)) for c in chunks]

    def cumdecay(g):
        g1 = g.astype(BF16)
        r1 = g - g1.astype(F32)
        g2 = r1.astype(BF16)
        g3 = (r1 - g2.astype(F32)).astype(BF16)
        return bdot(tri_b, g1) + bdot(tri_b, g2) + bdot(tri_b, g3)

    gc_l = [cumdecay(g_t[crow[c]]) for c in chunks]
    gct_l = [gc_l[c].T for c in chunks]
    gcol_l = [stack(lambda h: jnp.broadcast_to(gc_l[c][:, GDN_HEADS + h:GDN_HEADS + h + 1], (CHUNK, LANES)))
              for c in chunks]
    glast_l = [stack(lambda h: jnp.broadcast_to(gc_l[c][CHUNK - 1:CHUNK, GDN_HEADS + h:GDN_HEADS + h + 1],
                                                (CHUNK, LANES))) for c in chunks]
    bcol_l = [stack(lambda h: jnp.broadcast_to(beta_t[crow[c], h:h + 1], (CHUNK, LANES))) for c in chunks]
    grow_l = [jnp.concatenate([gct_l[c][GDN_HEADS + h:GDN_HEADS + h + 1, :] for h in range(GDN_HEADS)], axis=1)
              for c in chunks]
    kk_l = [stack(lambda h: k_s[crow[c], LANES * h:LANES * (h + 1)]) for c in chunks]
    qq_l = [stack(lambda h: q_s[crow[c], LANES * h:LANES * (h + 1)]) for c in chunks]
    vv_l = [stack(lambda h: v_s[crow[c], LANES * h:LANES * (h + 1)]) for c in chunks]
    dec_l = [jnp.exp(jnp.minimum(jnp.concatenate([gcol_l[c], gcol_l[c]], axis=1) - grow_l[c], 0.0)) * incl_f
             for c in chunks]
    kb_l = [kk_l[c] * bcol_l[c] for c in chunks]
    k16_l = [kk_l[c].astype(BF16) for c in chunks]
    amat_l = [lax.dot_general(kb_l[c].astype(BF16), k16_l[c], nt, preferred_element_type=F32)
              * dec_l[c] * strict_f for c in chunks]
    eye16 = eye.astype(BF16)
    f_l = [amat_l[c] * sub_f for c in chunks]
    bm_l = [f_l[c].astype(BF16) for c in chunks]
    nm_l = [(amat_l[c] - f_l[c]).astype(BF16) for c in chunks]
    b2_l = [_mm16(bm_l[c], bm_l[c]) for c in chunks]
    b4_l = [_mm16(b2_l[c], b2_l[c]) for c in chunks]
    b8_l = [_mm16(b4_l[c], b4_l[c]) for c in chunks]
    d_l = [_mm16(eye16 - bm_l[c], eye16 + b2_l[c]) for c in chunks]
    d_l = [_mm16(d_l[c], eye16 + b4_l[c]) for c in chunks]
    dinv_l = [_mm16(d_l[c], eye16 + b8_l[c]) for c in chunks]
    mm_l = [_mm16(dinv_l[c], nm_l[c]) for c in chunks]
    m2_l = [_mm16(mm_l[c], mm_l[c]) for c in chunks]
    t_l = [_mm16(eye16 - mm_l[c], eye16 + m2_l[c]) for c in chunks]
    tinv_l = [_mm16(t_l[c], dinv_l[c]) for c in chunks]
    eg_l = [jnp.exp(gcol_l[c]) for c in chunks]
    sol_l = [_mm(tinv_l[c], jnp.concatenate([vv_l[c] * bcol_l[c], kb_l[c] * eg_l[c]], axis=1)) for c in chunks]
    qk_l = [lax.dot_general(qq_l[c].astype(BF16), k16_l[c], nt, preferred_element_type=F32) * dec_l[c]
            for c in chunks]
    qd_l = [qq_l[c] * eg_l[c] for c in chunks]
    kd_l = [kk_l[c] * jnp.exp(glast_l[c] - gcol_l[c]) for c in chunks]

    for c in chunks:
        rows = crow[c]
        gc = gc_l[c]
        u = sol_l[c][:, :LANES]
        w = sol_l[c][:, LANES:]
        qk, qd, kd = qk_l[c], qd_l[c], kd_l[c]

        ws, qs = [], []
        for h in range(GDN_HEADS):
            hr = slice(CHUNK * h, CHUNK * (h + 1))
            lhs = jnp.concatenate([w[hr], qd[hr]], axis=0).astype(BF16)
            r = bdot(lhs, state_ref[h].astype(BF16))
            ws.append(r[:CHUNK])
            qs.append(r[CHUNK:])
        vn = u - jnp.concatenate(ws, axis=0)
        vn16 = vn.astype(BF16)
        o = jnp.concatenate(qs, axis=0) + bdot(qk.astype(BF16), vn16)
        kd16 = kd.astype(BF16)
        for h in range(GDN_HEADS):
            hr = slice(CHUNK * h, CHUNK * (h + 1))
            hs = slice(LANES * h, LANES * (h + 1))
            gl = jnp.exp(gc[CHUNK - 1:CHUNK, GDN_HEADS + h:GDN_HEADS + h + 1])
            state_ref[h] = state_ref[h] * gl + lax.dot_general(kd16[hr], vn16[hr], tn,
                                                               preferred_element_type=F32)
            oh = o[hr]
            z = a_ref[rows, CONV_CH + LANES * h:CONV_CH + LANES * (h + 1)]
            oh = oh * lax.rsqrt(jnp.mean(oh * oh, -1, keepdims=True) + RMS_EPS) * nw_ref[...]
            o_ref[rows, hs] = (oh * (z * _sigmoid(z))).astype(o_ref.dtype)


def _gdn(a_proj, ba_proj, conv_w, a_log, dt_bias, norm_w, batch, seq):
    t = T_GDN
    nst = seq // t
    rowblk = lambda w: pl.BlockSpec((t, w), lambda b, s: (b * nst + s, 0))
    aux = jnp.zeros((8, LANES), F32)
    aux = aux.at[0, GDN_HEADS:2 * GDN_HEADS].set(a_log).at[1, GDN_HEADS:2 * GDN_HEADS].set(dt_bias)
    return pl.pallas_call(
        _gdn_kernel,
        grid=(batch, nst),
        in_specs=[rowblk(A_COLS), rowblk(LANES),
                  pl.BlockSpec((CONV_WIDTH, CONV_CH), lambda b, s: (0, 0)),
                  pl.BlockSpec((8, LANES), lambda b, s: (0, 0)),
                  pl.BlockSpec((1, GDN_DV), lambda b, s: (0, 0))],
        out_specs=rowblk(GDN_W),
        out_shape=jax.ShapeDtypeStruct((batch * seq, GDN_W), BF16),
        scratch_shapes=[pltpu.VMEM((t + 8, CONV_CH), F32),
                        pltpu.VMEM((t, GDN_W), F32),
                        pltpu.VMEM((t, GDN_W), F32),
                        pltpu.VMEM((t, GDN_W), F32),
                        pltpu.VMEM((GDN_HEADS, GDN_DK, GDN_DV), F32)],
        compiler_params=pltpu.CompilerParams(dimension_semantics=("arbitrary", "arbitrary"),
                                             vmem_limit_bytes=VMEM_LIMIT),
        name="gdn",
    )(a_proj, ba_proj, conv_w, aux, norm_w.reshape(1, GDN_DV))


def _attn_kernel(q_ref, k_ref, vt_ref, lq1_ref, lk1_ref, lq2_ref, lk2_ref, nw_ref, o_ref, *, lam_init):
    tq = q_ref.shape[0]
    qi = pl.program_id(1)
    lane = lax.broadcasted_iota(jnp.int32, (1, LANES), 1)
    qpos = qi * tq + (lax.broadcasted_iota(jnp.int32, (1, 2 * tq), 1) & (tq - 1))
    nt = (((1,), (1,)), ((), ()))
    q2 = []
    for h in range(DIFF_HEADS):
        q = q_ref[:, LANES * h:LANES * (h + 1)]
        zero = jnp.zeros_like(q)
        q2.append(jnp.concatenate([jnp.where(lane < DIFF_D, q, zero), jnp.where(lane >= DIFF_D, q, zero)], axis=0))

    heads = range(DIFF_HEADS)

    def scores(kj):
        off = pl.multiple_of(kj * TKV, TKV)
        return tuple(lax.dot_general(k_ref[pl.ds(off, TKV), LANES * h:LANES * (h + 1)], q2[h], nt,
                                     preferred_element_type=F32) for h in heads)

    def update(kj, ss, state, masked):
        off = pl.multiple_of(kj * TKV, TKV)
        if masked:
            keep = off + lax.broadcasted_iota(jnp.int32, (TKV, 1), 0) <= qpos
            ss = [jnp.where(keep, s, -1e30) for s in ss]
        mns = [jnp.maximum(state[h][0], jnp.max(ss[h], 0, keepdims=True)) for h in heads]
        ps = [jnp.exp2((ss[h] - mns[h]).astype(BF16)) for h in heads]
        als = [jnp.exp2(state[h][0] - mns[h]) for h in heads]
        pvs = [jnp.dot(vt_ref[VT_ROWS * h:VT_ROWS * (h + 1), pl.ds(off, TKV)], ps[h],
                       preferred_element_type=F32) for h in heads]
        return tuple((mns[h], als[h] * state[h][1] + pvs[h]) for h in heads)

    def body(kj, state):
        return update(kj, scores(kj), state, False)

    init = tuple((jnp.full((1, 2 * tq), -1e30, F32), jnp.zeros((VT_ROWS, 2 * tq), F32)) for _ in heads)
    state = lax.fori_loop(0, qi, body, init)
    carry = update(qi, scores(qi), state, True)
    lam = (jnp.exp(jnp.sum(lq1_ref[...] * lk1_ref[...], -1, keepdims=True))
           - jnp.exp(jnp.sum(lq2_ref[...] * lk2_ref[...], -1, keepdims=True)) + lam_init)
    for h in range(DIFF_HEADS):
        m, acc = carry[h]
        on = acc[:DIFF_DV] * (1.0 / acc[DIFF_DV:DIFF_DV + 1])
        o = (on[:, :tq] - lam * on[:, tq:]).T
        o = o * lax.rsqrt(jnp.mean(o * o, -1, keepdims=True) + RMS_EPS) * nw_ref[...] * (1.0 - lam_init)
        o_ref[:, LANES * h:LANES * (h + 1)] = o.astype(o_ref.dtype)


def _diff_attention(q, k, v3, lq1, lk1, lq2, lk2, norm_w, lam_init, batch, seq):
    q3 = q.reshape(batch, seq, DIFF_W)
    k3 = k.reshape(batch, seq, DIFF_W)
    small = lambda w: pl.BlockSpec((1, w), lambda b, i: (0, 0))
    out = pl.pallas_call(
        functools.partial(_attn_kernel, lam_init=lam_init),
        grid=(batch, seq // TQ),
        in_specs=[pl.BlockSpec((None, TQ, DIFF_W), lambda b, i: (b, i, 0)),
                  pl.BlockSpec((None, seq, DIFF_W), lambda b, i: (b, 0, 0)),
                  pl.BlockSpec((None, DIFF_HEADS * VT_ROWS, seq), lambda b, i: (b, 0, 0)),
                  small(DIFF_D), small(DIFF_D), small(DIFF_D), small(DIFF_D), small(DIFF_DV)],
        out_specs=pl.BlockSpec((None, TQ, DIFF_W), lambda b, i: (b, i, 0)),
        out_shape=jax.ShapeDtypeStruct((batch, seq, DIFF_W), BF16),
        compiler_params=pltpu.CompilerParams(
            dimension_semantics=("arbitrary", "arbitrary"),
            vmem_limit_bytes=VMEM_LIMIT),
        name="diff_attn",
    )(q3, k3, v3, lq1.reshape(1, -1), lk1.reshape(1, -1), lq2.reshape(1, -1), lk2.reshape(1, -1),
      norm_w.reshape(1, -1))
    return out.reshape(batch * seq, DIFF_W)


def _outproj_kernel(og_ref, od_ref, x_ref, wo_ref, g_ref, b_ref, rw_ref, rb_ref,
                    h_ref, idx_ref, gate_ref, cnt_out_ref, tcnt_ref, cnt_ref):
    @pl.when(pl.program_id(0) == 0)
    def _():
        cnt_ref[...] = jnp.zeros(cnt_ref.shape, F32)

    mix = (jnp.dot(og_ref[...], wo_ref[:GDN_W, :], preferred_element_type=F32)
           + jnp.dot(od_ref[...], wo_ref[GDN_W:, :], preferred_element_type=F32))
    h = _layer_norm(DN_ALPHA * x_ref[...] + mix, g_ref[...], b_ref[...])
    h_ref[...] = h
    h_hi = h.astype(BF16)
    h_lo = (h - h_hi.astype(F32)).astype(BF16)
    logits = jnp.dot(jnp.concatenate([h_hi, h_hi, h_lo], axis=1), rw_ref[...],
                     preferred_element_type=F32) + rb_ref[...]
    tm = logits.shape[0]
    lane = lax.broadcasted_iota(jnp.int32, (tm, LANES), 1)
    lane_f = lane.astype(F32)
    work = logits
    vals, idxs = [], []
    for _ in range(TOP_K):
        m = jnp.max(work, -1, keepdims=True)
        sel = jnp.min(jnp.where(work == m, lane_f, float(LANES)), -1, keepdims=True)
        vals.append(m)
        idxs.append(sel)
        work = jnp.where(lane_f == sel, -jnp.inf, work)
    exps = [jnp.exp(vv - vals[0]) for vv in vals]
    inv = 1.0 / (exps[0] + exps[1] + exps[2] + exps[3])

    hot = jnp.zeros((tm, LANES), F32)
    for j in range(TOP_K):
        hot = hot + jnp.where(lane_f == idxs[j], 1.0, 0.0)
    ri = lax.broadcasted_iota(jnp.int32, (tm, tm), 0)
    ci = lax.broadcasted_iota(jnp.int32, (tm, tm), 1)
    before = jnp.where(ci < ri, 1.0, 0.0).astype(BF16)
    prefix = jnp.dot(before, hot.astype(BF16), preferred_element_type=F32) + cnt_ref[...]
    cnt_ref[...] = cnt_ref[...] + jnp.sum(hot, 0, keepdims=True)
    cnt_out_ref[...] = cnt_ref[...].astype(jnp.int32)
    starts = [prefix[TC_COMB * j:TC_COMB * j + 1, :] for j in range(tm // TC_COMB)]
    pad = jnp.zeros((tcnt_ref.shape[0] - len(starts), LANES), F32)
    tcnt_ref[...] = jnp.concatenate(starts + [pad], axis=0).astype(jnp.int32)

    idx_out = jnp.zeros((tm, LANES), F32)
    gate_out = jnp.zeros((tm, LANES), F32)
    for j in range(TOP_K):
        rank = jnp.sum(jnp.where(lane_f == idxs[j], prefix, 0.0), -1, keepdims=True)
        idx_out = jnp.where(lane == j, idxs[j], idx_out)
        idx_out = jnp.where(lane == TOP_K + j, rank, idx_out)
        gate_out = jnp.where(lane == j, exps[j] * inv, gate_out)
    idx_ref[...] = idx_out.astype(jnp.int32)
    gate_ref[...] = gate_out


def _outproj(og, od, x2d, w_out_b, ln_g, ln_b, rw_pad, rb_pad):
    n = x2d.shape[0]
    tm = TM_PROJ
    row = lambda w: pl.BlockSpec((tm, w), lambda i: (i, 0))
    full = lambda r, c: pl.BlockSpec((r, c), lambda i: (0, 0))
    return pl.pallas_call(
        _outproj_kernel,
        grid=(n // tm,),
        in_specs=[row(GDN_W), row(DIFF_W), row(D_MODEL), full(GDN_W + DIFF_W, D_MODEL),
                  full(1, D_MODEL), full(1, D_MODEL), full(3 * D_MODEL, LANES), full(1, LANES)],
        out_specs=[row(D_MODEL), row(LANES), row(LANES), full(1, LANES),
                   pl.BlockSpec((8, LANES), lambda i: (i, 0))],
        out_shape=[jax.ShapeDtypeStruct((n, D_MODEL), F32),
                   jax.ShapeDtypeStruct((n, LANES), jnp.int32),
                   jax.ShapeDtypeStruct((n, LANES), F32),
                   jax.ShapeDtypeStruct((1, LANES), jnp.int32),
                   jax.ShapeDtypeStruct((n // tm * 8, LANES), jnp.int32)],
        scratch_shapes=[pltpu.VMEM((1, LANES), F32)],
        compiler_params=pltpu.CompilerParams(dimension_semantics=("arbitrary",),
                                             vmem_limit_bytes=VMEM_LIMIT),
        name="outproj_ln1_router",
    )(og, od, x2d, w_out_b, ln_g.reshape(1, -1), ln_b.reshape(1, -1), rw_pad, rb_pad)


def _dispatch_kernel(pos_ref, h_ref, xs_hbm, sem):
    tc = h_ref.shape[0]
    base = pl.program_id(0) * (tc * TOP_K)

    def body(r, carry):
        for j in range(TOP_K):
            s = pos_ref[base + r * TOP_K + j]
            pltpu.make_async_copy(h_ref.at[pl.ds(r, 1), :], xs_hbm.at[pl.ds(s, 1), :], sem).start()
        return carry

    lax.fori_loop(0, tc, body, 0, unroll=4)
    for j in range(TOP_K):
        pltpu.make_async_copy(h_ref, xs_hbm.at[pl.ds(0, tc), :], sem).wait()


def _dispatch(pos, h1):
    n = h1.shape[0]
    tc = TC_DISP
    grid_spec = pltpu.PrefetchScalarGridSpec(
        num_scalar_prefetch=1,
        grid=(n // tc,),
        in_specs=[pl.BlockSpec((tc, D_MODEL), lambda i, pos: (i, 0))],
        out_specs=pl.BlockSpec(memory_space=pl.ANY),
        scratch_shapes=[pltpu.SemaphoreType.DMA],
    )
    return pl.pallas_call(
        _dispatch_kernel,
        grid_spec=grid_spec,
        out_shape=jax.ShapeDtypeStruct((n * TOP_K, D_MODEL), F32),
        compiler_params=pltpu.CompilerParams(dimension_semantics=("arbitrary",),
                                             vmem_limit_bytes=VMEM_LIMIT),
        name="dispatch",
    )(pos, h1)


def _moe_kernel(we_ref, wt_ref, lo_ref, hi_ref, nw_ref, slot_ref, nxt_ref, xs_ref, wgu_hbm, bgu_ref, wd_hbm,
                bd_ref, ys_ref, wgu_f, wd_f, wgu_b, wd_b, acc_ref, sem):
    tm = ys_ref.shape[0]
    w = pl.program_id(0)

    def weights_copy(e, s):
        return (pltpu.make_async_copy(wgu_hbm.at[e], wgu_f.at[s], sem.at[s, 0]),
                pltpu.make_async_copy(wd_hbm.at[e], wd_f.at[s], sem.at[s, 1]))

    @pl.when(w == 0)
    def _():
        acc_ref[...] = jnp.zeros(acc_ref.shape, F32)
        for cp in weights_copy(we_ref[0], slot_ref[0]):
            cp.start()

    @pl.when(w < nw_ref[0])
    def _():
        prev = jnp.maximum(w - 1, 0)

        @pl.when((w == 0) | (we_ref[w] != we_ref[prev]))
        def _():
            s = slot_ref[w]
            for cp in weights_copy(we_ref[w], s):
                cp.wait()

            @pl.when(nxt_ref[w] >= 0)
            def _():
                for cp in weights_copy(nxt_ref[w], 1 - s):
                    cp.start()

            rows = 128

            def cast(j, carry):
                r0 = pl.multiple_of(j * rows, rows)
                wgu_b[pl.ds(r0, rows), :] = wgu_f[s, pl.ds(r0, rows), :].astype(BF16)
                wd_b[pl.ds(r0, rows), :] = wd_f[s, pl.ds(r0, rows), :].astype(BF16)
                return carry

            lax.fori_loop(0, D_MODEL // rows, cast, 0)

        xb = xs_ref[...].astype(BF16)
        hgu = jnp.dot(xb, wgu_b[...], preferred_element_type=F32) + bgu_ref[0]
        gate = jnp.minimum(hgu[:, :D_FF], SWIGLU_LIMIT)
        up = jnp.clip(hgu[:, D_FF:], -SWIGLU_LIMIT, SWIGLU_LIMIT)
        act = (up + 1.0) * gate * _sigmoid(SWIGLU_ALPHA * gate)
        y = jnp.dot(act.astype(BF16), wd_b[...], preferred_element_type=F32) + bd_ref[0]
        rid = lax.broadcasted_iota(jnp.int32, (tm, 1), 0)
        mine = jnp.where(rid >= lo_ref[w], jnp.where(rid < hi_ref[w], 1.0, 0.0), 0.0) > 0.5
        keep = jnp.where((w == 0) | (wt_ref[w] != wt_ref[prev]), 0.0, 1.0)
        merged = jnp.where(mine, y, acc_ref[...] * keep)
        acc_ref[...] = merged
        ys_ref[...] = merged


def _moe(work, xs, w_gu, b_gu, w_down, b_down):
    tm = TM_MOE
    m = xs.shape[0]
    n_items = m // tm + N_EXPERTS - 1
    widx = lambda f: (lambda w, we, wt, lo, hi, nw, slot, nxt: f(w, we, wt))
    grid_spec = pltpu.PrefetchScalarGridSpec(
        num_scalar_prefetch=7,
        grid=(n_items,),
        in_specs=[pl.BlockSpec((tm, D_MODEL), widx(lambda w, we, wt: (wt[w], 0))),
                  pl.BlockSpec(memory_space=pl.ANY),
                  pl.BlockSpec((1, 1, 2 * D_FF), widx(lambda w, we, wt: (we[w], 0, 0))),
                  pl.BlockSpec(memory_space=pl.ANY),
                  pl.BlockSpec((1, 1, D_MODEL), widx(lambda w, we, wt: (we[w], 0, 0)))],
        out_specs=pl.BlockSpec((tm, D_MODEL), widx(lambda w, we, wt: (wt[w], 0))),
        scratch_shapes=[pltpu.VMEM((2, D_MODEL, 2 * D_FF), F32),
                        pltpu.VMEM((2, D_FF, D_MODEL), F32),
                        pltpu.VMEM((D_MODEL, 2 * D_FF), BF16),
                        pltpu.VMEM((D_FF, D_MODEL), BF16),
                        pltpu.VMEM((tm, D_MODEL), F32),
                        pltpu.SemaphoreType.DMA((2, 2))],
    )
    return pl.pallas_call(
        _moe_kernel,
        grid_spec=grid_spec,
        out_shape=jax.ShapeDtypeStruct((m, D_MODEL), F32),
        compiler_params=pltpu.CompilerParams(dimension_semantics=("arbitrary",),
                                             vmem_limit_bytes=VMEM_LIMIT),
        name="moe_ffn",
    )(*work, xs, w_gu, b_gu.reshape(N_EXPERTS, 1, -1), w_down, b_down.reshape(N_EXPERTS, 1, -1))


def _combine_kernel(cs_ref, nch_ref, h_ref, gate_ref, pos_ref, bs_ref, p_ref, ys_hbm, g2_ref, b2_ref, wg_ref,
                    bg_ref, wp_ref, g3_ref, b3_ref, o_ref, ybuf, sem):
    i = pl.program_id(0)
    n = pl.num_programs(0)

    def issue(t, b):
        def body(c, carry):
            dst = pl.multiple_of(c * COMB_CH, COMB_CH)
            src = pl.multiple_of(cs_ref[t * COMB_MAXCH + c], SUBLANES)
            pltpu.make_async_copy(ys_hbm.at[pl.ds(src, COMB_CH), :],
                                  ybuf.at[b, pl.ds(dst, COMB_CH), :], sem.at[b]).start()
            return carry

        lax.fori_loop(0, nch_ref[t], body, 0)

    @pl.when(i == 0)
    def _():
        def zero(j, carry):
            r0 = pl.multiple_of(j * COMB_CH, COMB_CH)
            for b in range(2):
                ybuf[b, pl.ds(r0, COMB_CH), :] = jnp.zeros((COMB_CH, D_MODEL), F32)
            return carry

        lax.fori_loop(0, COMB_MAXCH, zero, 0)
        issue(0, 0)

    @pl.when(i + 1 < n)
    def _():
        issue(i + 1, (i + 1) % 2)

    buf = i % 2

    def wait(c, carry):
        pltpu.make_async_copy(ys_hbm.at[pl.ds(0, COMB_CH), :], ybuf.at[buf, pl.ds(0, COMB_CH), :],
                              sem.at[buf]).wait()
        return carry

    lax.fori_loop(0, nch_ref[i], wait, 0)
    gates = gate_ref[...]
    slots = bs_ref[0]
    sel = jnp.zeros((h_ref.shape[0], COMB_MAXCH * COMB_CH), F32)
    for j in range(TOP_K):
        sel = jnp.where(pos_ref[:, j:j + 1] == slots, gates[:, j:j + 1], sel)
    ffn = jnp.dot(sel.astype(BF16), ybuf[buf].astype(BF16), preferred_element_type=F32)
    h2 = _layer_norm(DN_ALPHA * h_ref[...] + ffn, g2_ref[...], b2_ref[...])
    gate = _sigmoid(jnp.dot(h2.astype(BF16), wg_ref[...], preferred_element_type=F32) + bg_ref[...])
    ple = gate * jnp.dot(p_ref[...].astype(BF16), wp_ref[...], preferred_element_type=F32)
    o_ref[...] = _layer_norm(DN_ALPHA * h2 + ple, g3_ref[...], b3_ref[...])


def _combine(chunks, pos2d, h1, gates, p2d, ys, g2, b2, wg_b, bg, wp_b, g3, b3):
    n = h1.shape[0]
    tc = TC_COMB
    cstart, nch, bufslot = chunks
    row = lambda w: pl.BlockSpec((tc, w), lambda i, cs, nc: (i, 0))
    full = lambda r, c: pl.BlockSpec((r, c), lambda i, cs, nc: (0, 0))
    grid_spec = pltpu.PrefetchScalarGridSpec(
        num_scalar_prefetch=2,
        grid=(n // tc,),
        in_specs=[row(D_MODEL), row(LANES), row(TOP_K),
                  pl.BlockSpec((None, 1, COMB_MAXCH * COMB_CH), lambda i, cs, nc: (i, 0, 0)),
                  row(PLE_DIM), pl.BlockSpec(memory_space=pl.ANY),
                  full(1, D_MODEL), full(1, D_MODEL), full(D_MODEL, D_MODEL), full(1, D_MODEL),
                  full(PLE_DIM, D_MODEL), full(1, D_MODEL), full(1, D_MODEL)],
        out_specs=row(D_MODEL),
        scratch_shapes=[pltpu.VMEM((2, COMB_MAXCH * COMB_CH, D_MODEL), F32),
                        pltpu.SemaphoreType.DMA((2,))],
    )
    return pl.pallas_call(
        _combine_kernel,
        grid_spec=grid_spec,
        out_shape=jax.ShapeDtypeStruct((n, D_MODEL), F32),
        compiler_params=pltpu.CompilerParams(dimension_semantics=("arbitrary",),
                                             vmem_limit_bytes=VMEM_LIMIT),
        name="combine_ln2_ple_ln3",
    )(cstart, nch, h1, gates, pos2d, bufslot, p2d, ys, g2.reshape(1, -1), b2.reshape(1, -1), wg_b,
      bg.reshape(1, -1), wp_b, g3.reshape(1, -1), b3.reshape(1, -1))


def _combine_chunks(cntb, counts, grp_start, m):
    n_tiles = cntb.shape[0]
    nxt = jnp.concatenate([cntb[1:], counts[None, :]], axis=0)
    length = nxt - cntb
    first = grp_start[None, :] + cntb
    base = (first // SUBLANES) * SUBLANES
    nq = jnp.where(length > 0, (first - base + length + COMB_CH - 1) // COMB_CH, 0)
    q_end = jnp.cumsum(nq, axis=1)
    q_start = q_end - nq
    nch = q_end[:, -1]
    c = jnp.arange(COMB_MAXCH, dtype=jnp.int32)
    e_c = jnp.clip(jnp.sum((q_end[:, None, :] <= c[None, :, None]).astype(jnp.int32), -1), 0, N_EXPERTS - 1)
    onehot = e_c[:, :, None] == jnp.arange(N_EXPERTS, dtype=jnp.int32)
    pick = lambda a: jnp.sum(jnp.where(onehot, a[:, None, :], 0), -1)
    want = pick(base) + COMB_CH * (c[None, :] - pick(q_start))
    start = jnp.minimum(want, m - COMB_CH)
    valid = c[None, :] < nch[:, None]
    cstart = jnp.where(valid, start, 0).astype(jnp.int32).reshape(-1)
    rows = start[:, :, None] + jnp.arange(COMB_CH, dtype=jnp.int32)
    lo = jnp.maximum(want, pick(first))[:, :, None]
    end = (pick(first) + pick(length))[:, :, None]
    own = valid[:, :, None] & (rows >= lo) & (rows < end)
    bufslot = jnp.where(own, rows, -1).astype(jnp.int32).reshape(n_tiles, 1, COMB_MAXCH * COMB_CH)
    return cstart, nch.astype(jnp.int32), bufslot


def _routing(idx, rank, counts, tm):
    n_tok = idx.shape[0]
    m = n_tok * TOP_K
    n_tiles = m // tm
    n_items = n_tiles + N_EXPERTS - 1
    grp_end = jnp.cumsum(counts)
    grp_start = grp_end - counts
    experts = jnp.arange(N_EXPERTS, dtype=jnp.int32)
    pos = (jnp.sum(jnp.where(idx[:, :, None] == experts, grp_start, 0), -1) + rank).reshape(-1).astype(jnp.int32)
    t_first = grp_start // tm
    t_last = (grp_end - 1) // tm
    n_e = jnp.where(counts > 0, t_last - t_first + 1, 0)
    w_end = jnp.cumsum(n_e)
    w_start = w_end - n_e
    n_work = w_end[-1]
    w = jnp.arange(n_items, dtype=jnp.int32)
    valid = w < n_work
    wq = jnp.minimum(w, n_work - 1)
    e_w = jnp.clip(jnp.sum((w_end[None, :] <= wq[:, None]).astype(jnp.int32), -1), 0, N_EXPERTS - 1)
    onehot = e_w[:, None] == experts
    pick = lambda a: jnp.sum(jnp.where(onehot, a[None, :], 0), -1)
    tile_w = pick(t_first) + (wq - pick(w_start))
    lo = jnp.clip(pick(grp_start) - tile_w * tm, 0, tm)
    hi = jnp.clip(pick(grp_end) - tile_w * tm, 0, tm)
    lo = jnp.where(valid, lo, 0)
    hi = jnp.where(valid, hi, 0)
    nonempty = counts > 0
    slot_e = (jnp.cumsum(nonempty.astype(jnp.int32)) - 1) % 2
    later = nonempty[None, :] & (experts[None, :] > experts[:, None])
    next_e = jnp.min(jnp.where(later, experts[None, :], N_EXPERTS), axis=1)
    next_e = jnp.where(next_e < N_EXPERTS, next_e, -1)
    i32 = lambda a: a.astype(jnp.int32)
    return pos, i32(grp_start), (i32(e_w), i32(tile_w), i32(lo), i32(hi), i32(n_work).reshape(1),
                                 i32(pick(slot_e)), i32(pick(next_e)))


def _layer(h, p_i, cos8, sin8, lam_init, w_in, conv_w, a_log, dt_bias, gdn_norm_w,
           lam_q1, lam_k1, lam_q2, lam_k2, diff_norm_w, w_out, ln1_g, ln1_b,
           router_w, router_b, w_gu, b_gu, w_down, b_down, ln2_g, ln2_b,
           ple_w, ple_gate_w, ple_gate_b, ln3_g, ln3_b):
    batch, seq, d = h.shape
    n = batch * seq
    x2d = h.reshape(n, d)
    o_ba = 4 * GDN_W
    o_d = o_ba + 2 * GDN_HEADS
    w_r = jnp.concatenate([w_in[:, :o_ba], w_in[:, o_d:], w_in[:, o_ba:o_d],
                           jnp.zeros((d, LANES - 2 * GDN_HEADS), w_in.dtype)], -1).astype(BF16)
    a_proj, ba_proj, dq, dk, dv = _inproj(x2d, w_r, jnp.concatenate([cos8, sin8], -1), batch, seq)
    o_gdn = _gdn(a_proj, ba_proj, conv_w, a_log, dt_bias, gdn_norm_w, batch, seq)
    o_diff = _diff_attention(dq, dk, dv, lam_q1, lam_k1, lam_q2, lam_k2, diff_norm_w, lam_init, batch, seq)
    rw_pad = jnp.concatenate([router_w, jnp.zeros((d, LANES - N_EXPERTS), F32)], -1)
    rw_hi = rw_pad.astype(BF16)
    rw_lo = (rw_pad - rw_hi.astype(F32)).astype(BF16)
    rw_pad = jnp.concatenate([rw_hi, rw_lo, rw_hi], axis=0)
    rb_pad = jnp.concatenate([router_b, jnp.full((LANES - N_EXPERTS,), -jnp.inf, F32)]).reshape(1, LANES)
    h1, idx, gates, cnt, tcnt = _outproj(o_gdn, o_diff, x2d, w_out.astype(BF16), ln1_g, ln1_b, rw_pad, rb_pad)
    counts = cnt[0, :N_EXPERTS]
    pos, grp_start, work = _routing(idx[:, :TOP_K], idx[:, TOP_K:2 * TOP_K], counts, TM_MOE)
    xs = _dispatch(pos, h1)
    ys = _moe(work, xs, w_gu, b_gu, w_down, b_down)
    per_step = TM_PROJ // TC_COMB
    cntb = tcnt.reshape(n // TM_PROJ, 8, LANES)[:, :per_step, :N_EXPERTS].reshape(n // TC_COMB, N_EXPERTS)
    chunks = _combine_chunks(cntb, counts, grp_start, n * TOP_K)
    out = _combine(chunks, pos.reshape(n, TOP_K), h1, gates, p_i.reshape(n, PLE_DIM), ys, ln2_g, ln2_b,
                   ple_gate_w.astype(BF16), ple_gate_b, ple_w.astype(BF16), ln3_g, ln3_b)
    return out.reshape(batch, seq, d)


def kernel(x, p, positions, w_in, conv_w, a_log, dt_bias, gdn_norm_w, lam_q1, lam_k1, lam_q2, lam_k2,
           diff_norm_w, w_out, ln1_g, ln1_b, router_w, router_b, w_gu, b_gu, w_down, b_down, ln2_g, ln2_b,
           ple_w, ple_gate_w, ple_gate_b, ln3_g, ln3_b):
    batch, seq, _ = x.shape
    inv_freq = ROPE_THETA ** (-jnp.arange(0, ROPE_DIM, 2, dtype=F32) / ROPE_DIM)
    ang = (positions.astype(F32)[..., None] * inv_freq).reshape(batch * seq, ROPE_DIM // 2)
    cos8 = jnp.cos(ang)
    sin8 = jnp.sin(ang)
    h = x
    for i in range(w_in.shape[0]):
        lam_init = 0.8 - 0.6 * math.exp(-0.3 * i)
        h = _layer(h, p[i], cos8, sin8, lam_init, w_in[i], conv_w[i], a_log[i], dt_bias[i], gdn_norm_w[i],
                   lam_q1[i], lam_k1[i], lam_q2[i], lam_k2[i], diff_norm_w[i], w_out[i], ln1_g[i], ln1_b[i],
                   router_w[i], router_b[i], w_gu[i], b_gu[i], w_down[i], b_down[i], ln2_g[i], ln2_b[i],
                   ple_w[i], ple_gate_w[i], ple_gate_b[i], ln3_g[i], ln3_b[i])
    return h
```

```python
import functools
import math

import jax
import jax.numpy as jnp
import numpy as np
from jax import lax
from jax.experimental import pallas as pl
from jax.experimental.pallas import tpu as pltpu

F32 = jnp.float32
BF16 = jnp.bfloat16
LOG2E = 1.4426950408889634

D_MODEL = 1024
PLE_DIM = 256
GDN_HEADS = 4
GDN_DK = 128
GDN_DV = 128
CONV_WIDTH = 4
CHUNK = 64
DIFF_HEADS = 4
DIFF_D = 64
DIFF_DV = 2 * DIFF_D
ROPE_THETA = 500000.0
ROPE_DIM = DIFF_D // 4
N_EXPERTS = 32
TOP_K = 4
D_FF = D_MODEL
SWIGLU_LIMIT = 7.0
SWIGLU_ALPHA = 1.702
DEPTH = 1
DN_ALPHA = (2 * DEPTH) ** 0.25
LN_EPS = 1e-5
RMS_EPS = 1e-6

LANES = 128
SUBLANES = 8
GDN_W = GDN_HEADS * GDN_DK
CONV_CH = 3 * GDN_W
A_COLS = 4 * GDN_W
DIFF_W = DIFF_HEADS * DIFF_DV
IN_PAD_W = A_COLS + 3 * DIFF_W + LANES
HSTACK = GDN_HEADS * CHUNK
VT_ROWS = DIFF_DV + 16
SUBBLK = 16

VMEM_LIMIT = 56 * 1024 * 1024

TM_PROJ = 512
T_GDN = 512
TQ = 512
TKV = 512
TM_MOE = 256
TC_DISP = 512
TC_COMB = 256
COMB_CH = 32
COMB_MAXCH = 72
assert COMB_MAXCH >= (TC_COMB * TOP_K + N_EXPERTS * (SUBLANES - 1 + COMB_CH - 1)) // COMB_CH
assert (COMB_MAXCH * COMB_CH) % LANES == 0


def _layer_norm(y, g, b):
    mu = jnp.mean(y, -1, keepdims=True)
    d = y - mu
    var = jnp.mean(d * d, -1, keepdims=True)
    return d * lax.rsqrt(var + LN_EPS) * g + b


def _sigmoid(x):
    return 1.0 / (1.0 + jnp.exp(-x))


def _inproj_kernel(x_ref, w_ref, cs_ref, e_ref, a_ref, ba_ref, q_ref, k_ref, vt_ref):
    xb = x_ref[...].astype(BF16)
    a_ref[...] = jnp.dot(xb, w_ref[:, :A_COLS], preferred_element_type=F32)
    ba_ref[...] = jnp.dot(xb, w_ref[:, A_COLS + 3 * DIFF_W:], preferred_element_type=F32)
    cs = cs_ref[...]
    p1 = cs.astype(BF16)
    r1 = cs - p1.astype(F32)
    p2 = r1.astype(BF16)
    p3 = (r1 - p2.astype(F32)).astype(BF16)
    tabs = (jnp.dot(p1, e_ref[...], preferred_element_type=F32) + jnp.dot(p2, e_ref[...], preferred_element_type=F32)
            + jnp.dot(p3, e_ref[...], preferred_element_type=F32))
    lane = lax.broadcasted_iota(jnp.int32, (1, LANES), 1)
    c = tabs[:, :LANES] + jnp.where((lane & (DIFF_D - 1)) >= ROPE_DIM, 1.0, 0.0)
    sa = tabs[:, LANES:2 * LANES]
    sb = tabs[:, 2 * LANES:]

    def rot(t):
        return t * c + pltpu.roll(t, 8, 1) * sa + pltpu.roll(t, LANES - 8, 1) * sb

    qk = jnp.dot(xb, w_ref[:, A_COLS:A_COLS + 2 * DIFF_W], preferred_element_type=F32)
    for h in range(DIFF_HEADS):
        q = qk[:, LANES * h:LANES * (h + 1)]
        q_ref[:, LANES * h:LANES * (h + 1)] = (rot(q) * (DIFF_D ** -0.5 * LOG2E)).astype(BF16)
        k = qk[:, DIFF_W + LANES * h:DIFF_W + LANES * (h + 1)]
        k_ref[:, LANES * h:LANES * (h + 1)] = rot(k).astype(BF16)
    lo = A_COLS + 2 * DIFF_W
    v = jnp.dot(xb, w_ref[:, lo:lo + DIFF_W], preferred_element_type=F32)
    ones = jnp.ones((VT_ROWS - DIFF_DV, v.shape[0]), BF16)
    for h in range(DIFF_HEADS):
        vt_ref[VT_ROWS * h:VT_ROWS * h + DIFF_DV, :] = v[:, DIFF_DV * h:DIFF_DV * (h + 1)].T.astype(BF16)
        vt_ref[VT_ROWS * h + DIFF_DV:VT_ROWS * (h + 1), :] = ones


def _rope_expand():
    half = ROPE_DIM // 2
    e = np.zeros((ROPE_DIM, 3 * LANES), np.float32)
    for lane in range(LANES):
        j = lane % DIFF_D
        if j < ROPE_DIM:
            e[j % half, lane] = 1.0
            if j >= half:
                e[half + j % half, LANES + lane] = 1.0
            else:
                e[half + j % half, 2 * LANES + lane] = -1.0
    return jnp.asarray(e, BF16)


def _inproj(x2d, w_r, cs16, batch, seq):
    n = x2d.shape[0]
    tm = TM_PROJ
    per_b = seq // tm
    row = lambda w: pl.BlockSpec((tm, w), lambda i: (i, 0))
    return pl.pallas_call(
        _inproj_kernel,
        grid=(n // tm,),
        in_specs=[row(D_MODEL),
                  pl.BlockSpec((D_MODEL, IN_PAD_W), lambda i: (0, 0)),
                  row(ROPE_DIM),
                  pl.BlockSpec((ROPE_DIM, 3 * LANES), lambda i: (0, 0))],
        out_specs=[row(A_COLS), row(LANES), row(DIFF_W), row(DIFF_W),
                   pl.BlockSpec((None, DIFF_HEADS * VT_ROWS, tm), lambda i: (i // per_b, 0, i % per_b))],
        out_shape=[jax.ShapeDtypeStruct((n, A_COLS), F32),
                   jax.ShapeDtypeStruct((n, LANES), F32),
                   jax.ShapeDtypeStruct((n, DIFF_W), BF16),
                   jax.ShapeDtypeStruct((n, DIFF_W), BF16),
                   jax.ShapeDtypeStruct((batch, DIFF_HEADS * VT_ROWS, seq), BF16)],
        compiler_params=pltpu.CompilerParams(dimension_semantics=("arbitrary",),
                                             vmem_limit_bytes=VMEM_LIMIT),
        name="inproj",
    )(x2d, w_r, cs16, _rope_expand())


def _mm(a, b):
    return jnp.dot(a.astype(BF16), b.astype(BF16), preferred_element_type=F32)


def _mm16(a, b):
    return jnp.dot(a, b, preferred_element_type=F32).astype(BF16)


def _gdn_kernel(a_ref, ba_ref, cw_ref, aux_ref, nw_ref, o_ref, xe_ref, q_s, k_s, v_s, state_ref):
    t_rows = a_ref.shape[0]
    st = pl.program_id(1)

    @pl.when(st == 0)
    def _():
        xe_ref[0:8, :] = jnp.zeros((8, CONV_CH), F32)
        state_ref[...] = jnp.zeros(state_ref.shape, F32)

    xe_ref[8:8 + t_rows, :] = a_ref[:, :CONV_CH]
    for s in range(CONV_CH // LANES):
        cs = slice(LANES * s, LANES * (s + 1))
        y = jnp.zeros((t_rows, LANES), F32)
        for j in range(CONV_WIDTH):
            off = 8 - (CONV_WIDTH - 1) + j
            y = y + xe_ref[off:off + t_rows, cs] * cw_ref[j:j + 1, cs]
        y = y * _sigmoid(y)
        grp, h = divmod(s, GDN_HEADS)
        hs = slice(LANES * h, LANES * (h + 1))
        if grp == 0:
            q_s[:, hs] = y * lax.rsqrt(jnp.sum(y * y, -1, keepdims=True) + 1e-6) * (GDN_DK ** -0.5)
        elif grp == 1:
            k_s[:, hs] = y * lax.rsqrt(jnp.sum(y * y, -1, keepdims=True) + 1e-6)
        else:
            v_s[:, hs] = y
    xe_ref[0:8, :] = xe_ref[t_rows:t_rows + 8, :]

    ba = ba_ref[...]
    beta_t = _sigmoid(ba)
    gx = ba + aux_ref[1:2, :]
    g_t = -jnp.exp(aux_ref[0:1, :]) * (jnp.maximum(gx, 0.0) + jnp.log(1.0 + jnp.exp(-jnp.abs(gx))))

    ri = lax.broadcasted_iota(jnp.int32, (HSTACK, HSTACK), 0)
    ci = lax.broadcasted_iota(jnp.int32, (HSTACK, HSTACK), 1)
    head_start = ri - (ri & (CHUNK - 1))
    in_head = ci >= head_start
    incl_f = jnp.where(in_head, jnp.where(ci <= ri, 1.0, 0.0), 0.0)
    strict_f = jnp.where(in_head, jnp.where(ci < ri, 1.0, 0.0), 0.0)
    sub_f = jnp.where(ci >= ri - (ri & (SUBBLK - 1)), 1.0, 0.0)
    eye = jnp.where(ri == ci, 1.0, 0.0)
    tri_b = (lax.broadcasted_iota(jnp.int32, (CHUNK, CHUNK), 0)
             >= lax.broadcasted_iota(jnp.int32, (CHUNK, CHUNK), 1)).astype(BF16)
    nt = (((1,), (1,)), ((), ()))
    tn = (((0,), (0,)), ((), ()))
    bdot = functools.partial(jnp.dot, preferred_element_type=F32)

    def stack(fn):
        return jnp.concatenate([fn(h) for h in range(GDN_HEADS)], axis=0)

    chunks = range(t_rows // CHUNK)
    crow = [slice(CHUNK * c, CHUNK * (c + 1)) for c in chunks]

    def cumdecay(g):
        g1 = g.astype(BF16)
        r1 = g - g1.astype(F32)
        g2 = r1.astype(BF16)
        g3 = (r1 - g2.astype(F32)).astype(BF16)
        return bdot(tri_b, g1) + bdot(tri_b, g2) + bdot(tri_b, g3)

    gc_l = [cumdecay(g_t[crow[c]]) for c in chunks]
    gct_l = [gc_l[c].T for c in chunks]
    gcol_l = [stack(lambda h: jnp.broadcast_to(gc_l[c][:, GDN_HEADS + h:GDN_HEADS + h + 1], (CHUNK, LANES)))
              for c in chunks]
    glast_l = [stack(lambda h: jnp.broadcast_to(gc_l[c][CHUNK - 1:CHUNK, GDN_HEADS + h:GDN_HEADS + h + 1],
                                                (CHUNK, LANES))) for c in chunks]
    bcol_l = [stack(lambda h: jnp.broadcast_to(beta_t[crow[c], h:h + 1], (CHUNK, LANES))) for c in chunks]
    grow_l = [jnp.concatenate([gct_l[c][GDN_HEADS + h:GDN_HEADS + h + 1, :] for h in range(GDN_HEADS)], axis=1)
              for c in chunks]
    kk_l = [stack(lambda h: k_s[crow[c], LANES * h:LANES * (h + 1)]) for c in chunks]
    qq_l = [stack(lambda h: q_s[crow[c], LANES * h:LANES * (h + 1)]) for c in chunks]
    vv_l = [stack(lambda h: v_s[crow[c], LANES * h:LANES * (h + 1)]) for c in chunks]
    dec_l = [jnp.exp(jnp.minimum(jnp.concatenate([gcol_l[c], gcol_l[c]], axis=1) - grow_l[c], 0.0)) * incl_f
             for c in chunks]
    kb_l = [kk_l[c] * bcol_l[c] for c in chunks]
    k16_l = [kk_l[c].astype(BF16) for c in chunks]
    amat_l = [lax.dot_general(kb_l[c].astype(BF16), k16_l[c], nt, preferred_element_type=F32)
              * dec_l[c] * strict_f for c in chunks]
    eye16 = eye.astype(BF16)
    f_l = [amat_l[c] * sub_f for c in chunks]
    bm_l = [f_l[c].astype(BF16) for c in chunks]
    nm_l = [(amat_l[c] - f_l[c]).astype(BF16) for c in chunks]
    b2_l = [_mm16(bm_l[c], bm_l[c]) for c in chunks]
    b4_l = [_mm16(b2_l[c], b2_l[c]) for c in chunks]
    b8_l = [_mm16(b4_l[c], b4_l[c]) for c in chunks]
    d_l = [_mm16(eye16 - bm_l[c], eye16 + b2_l[c]) for c in chunks]
    d_l = [_mm16(d_l[c], eye16 + b4_l[c]) for c in chunks]
    dinv_l = [_mm16(d_l[c], eye16 + b8_l[c]) for c in chunks]
    mm_l = [_mm16(dinv_l[c], nm_l[c]) for c in chunks]
    m2_l = [_mm16(mm_l[c], mm_l[c]) for c in chunks]
    t_l = [_mm16(eye16 - mm_l[c], eye16 + m2_l[c]) for c in chunks]
    tinv_l = [_mm16(t_l[c], dinv_l[c]) for c in chunks]
    eg_l = [jnp.exp(gcol_l[c]) for c in chunks]
    sol_l = [_mm(tinv_l[c], jnp.concatenate([vv_l[c] * bcol_l[c], kb_l[c] * eg_l[c]], axis=1)) for c in chunks]
    qk_l = [lax.dot_general(qq_l[c].astype(BF16), k16_l[c], nt, preferred_element_type=F32) * dec_l[c]
            for c in chunks]
    qd_l = [qq_l[c] * eg_l[c] for c in chunks]
    kd_l = [kk_l[c] * jnp.exp(glast_l[c] - gcol_l[c]) for c in chunks]

    for c in chunks:
        rows = crow[c]
        gc = gc_l[c]
        u = sol_l[c][:, :LANES]
        w = sol_l[c][:, LANES:]
        qk, qd, kd = qk_l[c], qd_l[c], kd_l[c]

        ws, qs = [], []
        for h in range(GDN_HEADS):
            hr = slice(CHUNK * h, CHUNK * (h + 1))
            lhs = jnp.concatenate([w[hr], qd[hr]], axis=0).astype(BF16)
            r = bdot(lhs, state_ref[h].astype(BF16))
            ws.append(r[:CHUNK])
            qs.append(r[CHUNK:])
        vn = u - jnp.concatenate(ws, axis=0)
        vn16 = vn.astype(BF16)
        o = jnp.concatenate(qs, axis=0) + bdot(qk.astype(BF16), vn16)
        kd16 = kd.astype(BF16)
        for h in range(GDN_HEADS):
            hr = slice(CHUNK * h, CHUNK * (h + 1))
            hs = slice(LANES * h, LANES * (h + 1))
            gl = jnp.exp(gc[CHUNK - 1:CHUNK, GDN_HEADS + h:GDN_HEADS + h + 1])
            state_ref[h] = state_ref[h] * gl + lax.dot_general(kd16[hr], vn16[hr], tn,
                                                               preferred_element_type=F32)
            oh = o[hr]
            z = a_ref[rows, CONV_CH + LANES * h:CONV_CH + LANES * (h + 1)]
            oh = oh * lax.rsqrt(jnp.mean(oh * oh, -1, keepdims=True) + RMS_EPS) * nw_ref[...]
            o_ref[rows, hs] = (oh * (z * _sigmoid(z))).astype(o_ref.dtype)


def _gdn(a_proj, ba_proj, conv_w, a_log, dt_bias, norm_w, batch, seq):
    t = T_GDN
    nst = seq // t
    rowblk = lambda w: pl.BlockSpec((t, w), lambda b, s: (b * nst + s, 0))
    aux = jnp.zeros((8, LANES), F32)
    aux = aux.at[0, GDN_HEADS:2 * GDN_HEADS].set(a_log).at[1, GDN_HEADS:2 * GDN_HEADS].set(dt_bias)
    return pl.pallas_call(
        _gdn_kernel,
        grid=(batch, nst),
        in_specs=[rowblk(A_COLS), rowblk(LANES),
                  pl.BlockSpec((CONV_WIDTH, CONV_CH), lambda b, s: (0, 0)),
                  pl.BlockSpec((8, LANES), lambda b, s: (0, 0)),
                  pl.BlockSpec((1, GDN_DV), lambda b, s: (0, 0))],
        out_specs=rowblk(GDN_W),
        out_shape=jax.ShapeDtypeStruct((batch * seq, GDN_W), BF16),
        scratch_shapes=[pltpu.VMEM((t + 8, CONV_CH), F32),
                        pltpu.VMEM((t, GDN_W), F32),
                        pltpu.VMEM((t, GDN_W), F32),
                        pltpu.VMEM((t, GDN_W), F32),
                        pltpu.VMEM((GDN_HEADS, GDN_DK, GDN_DV), F32)],
        compiler_params=pltpu.CompilerParams(dimension_semantics=("arbitrary", "arbitrary"),
                                             vmem_limit_bytes=VMEM_LIMIT),
        name="gdn",
    )(a_proj, ba_proj, conv_w, aux, norm_w.reshape(1, GDN_DV))


def _attn_kernel(q_ref, k_ref, vt_ref, lq1_ref, lk1_ref, lq2_ref, lk2_ref, nw_ref, o_ref, *, lam_init):
    tq = q_ref.shape[0]
    qi = pl.program_id(1)
    lane = lax.broadcasted_iota(jnp.int32, (1, LANES), 1)
    qpos = qi * tq + (lax.broadcasted_iota(jnp.int32, (1, 2 * tq), 1) & (tq - 1))
    nt = (((1,), (1,)), ((), ()))
    q2 = []
    for h in range(DIFF_HEADS):
        q = q_ref[:, LANES * h:LANES * (h + 1)]
        zero = jnp.zeros_like(q)
        q2.append(jnp.concatenate([jnp.where(lane < DIFF_D, q, zero), jnp.where(lane >= DIFF_D, q, zero)], axis=0))

    heads = range(DIFF_HEADS)

    def scores(kj):
        off = pl.multiple_of(kj * TKV, TKV)
        return tuple(lax.dot_general(k_ref[pl.ds(off, TKV), LANES * h:LANES * (h + 1)], q2[h], nt,
                                     preferred_element_type=F32) for h in heads)

    def update(kj, ss, state, masked):
        off = pl.multiple_of(kj * TKV, TKV)
        if masked:
            keep = off + lax.broadcasted_iota(jnp.int32, (TKV, 1), 0) <= qpos
            ss = [jnp.where(keep, s, -1e30) for s in ss]
        mns = [jnp.maximum(state[h][0], jnp.max(ss[h], 0, keepdims=True)) for h in heads]
        ps = [jnp.exp2((ss[h] - mns[h]).astype(BF16)) for h in heads]
        als = [jnp.exp2(state[h][0] - mns[h]) for h in heads]
        pvs = [jnp.dot(vt_ref[VT_ROWS * h:VT_ROWS * (h + 1), pl.ds(off, TKV)], ps[h],
                       preferred_element_type=F32) for h in heads]
        return tuple((mns[h], als[h] * state[h][1] + pvs[h]) for h in heads)

    def body(kj, state):
        return update(kj, scores(kj), state, False)

    init = tuple((jnp.full((1, 2 * tq), -1e30, F32), jnp.zeros((VT_ROWS, 2 * tq), F32)) for _ in heads)
    state = lax.fori_loop(0, qi, body, init)
    carry = update(qi, scores(qi), state, True)
    lam = (jnp.exp(jnp.sum(lq1_ref[...] * lk1_ref[...], -1, keepdims=True))
           - jnp.exp(jnp.sum(lq2_ref[...] * lk2_ref[...], -1, keepdims=True)) + lam_init)
    on = [carry[h][1][:DIFF_DV] * (1.0 / carry[h][1][DIFF_DV:DIFF_DV + 1]) for h in heads]
    ot = [(on[h][:, :tq] - lam * on[h][:, tq:]).T for h in heads]
    inv = [lax.rsqrt(jnp.mean(ot[h] * ot[h], -1, keepdims=True) + RMS_EPS) for h in heads]
    for h in heads:
        o_ref[:, LANES * h:LANES * (h + 1)] = (ot[h] * inv[h] * nw_ref[...] * (1.0 - lam_init)).astype(o_ref.dtype)


def _diff_attention(q, k, v3, lq1, lk1, lq2, lk2, norm_w, lam_init, batch, seq):
    q3 = q.reshape(batch, seq, DIFF_W)
    k3 = k.reshape(batch, seq, DIFF_W)
    small = lambda w: pl.BlockSpec((1, w), lambda b, i: (0, 0))
    out = pl.pallas_call(
        functools.partial(_attn_kernel, lam_init=lam_init),
        grid=(batch, seq // TQ),
        in_specs=[pl.BlockSpec((None, TQ, DIFF_W), lambda b, i: (b, i, 0)),
                  pl.BlockSpec((None, seq, DIFF_W), lambda b, i: (b, 0, 0)),
                  pl.BlockSpec((None, DIFF_HEADS * VT_ROWS, seq), lambda b, i: (b, 0, 0)),
                  small(DIFF_D), small(DIFF_D), small(DIFF_D), small(DIFF_D), small(DIFF_DV)],
        out_specs=pl.BlockSpec((None, TQ, DIFF_W), lambda b, i: (b, i, 0)),
        out_shape=jax.ShapeDtypeStruct((batch, seq, DIFF_W), BF16),
        compiler_params=pltpu.CompilerParams(
            dimension_semantics=("arbitrary", "arbitrary"),
            vmem_limit_bytes=VMEM_LIMIT),
        name="diff_attn",
    )(q3, k3, v3, lq1.reshape(1, -1), lk1.reshape(1, -1), lq2.reshape(1, -1), lk2.reshape(1, -1),
      norm_w.reshape(1, -1))
    return out.reshape(batch * seq, DIFF_W)


def _outproj_kernel(og_ref, od_ref, x_ref, wo_ref, g_ref, b_ref, rw_ref, rb_ref,
                    h_ref, idx_ref, gate_ref, cnt_out_ref, tcnt_ref, cnt_ref):
    @pl.when(pl.program_id(0) == 0)
    def _():
        cnt_ref[...] = jnp.zeros(cnt_ref.shape, F32)

    tm = x_ref.shape[0]
    tp = TC_COMB
    parts = range(tm // tp)
    rp = [slice(tp * p, tp * (p + 1)) for p in parts]
    mix = [jnp.dot(og_ref[rp[p], :], wo_ref[:GDN_W, :], preferred_element_type=F32)
           + jnp.dot(od_ref[rp[p], :], wo_ref[GDN_W:, :], preferred_element_type=F32) for p in parts]
    hs = [_layer_norm(DN_ALPHA * x_ref[rp[p], :] + mix[p], g_ref[...], b_ref[...]) for p in parts]
    for p in parts:
        h_ref[rp[p], :] = hs[p]
    h_hi = [hs[p].astype(BF16) for p in parts]
    h_lo = [(hs[p] - h_hi[p].astype(F32)).astype(BF16) for p in parts]
    work = [jnp.dot(jnp.concatenate([h_hi[p], h_hi[p], h_lo[p]], axis=1), rw_ref[...],
                    preferred_element_type=F32) + rb_ref[...] for p in parts]
    lane = lax.broadcasted_iota(jnp.int32, (tp, LANES), 1)
    lane_f = lane.astype(F32)
    vals, idxs = [], []
    for _ in range(TOP_K):
        m = [jnp.max(work[p], -1, keepdims=True) for p in parts]
        sel = [jnp.min(jnp.where(work[p] == m[p], lane_f, float(LANES)), -1, keepdims=True) for p in parts]
        vals.append(m)
        idxs.append(sel)
        work = [jnp.where(lane_f == sel[p], -jnp.inf, work[p]) for p in parts]
    exps = [[jnp.exp(vals[j][p] - vals[0][p]) for p in parts] for j in range(TOP_K)]
    inv = [1.0 / (exps[0][p] + exps[1][p] + exps[2][p] + exps[3][p]) for p in parts]

    hot = [jnp.zeros((tp, LANES), F32) for p in parts]
    for j in range(TOP_K):
        hot = [hot[p] + jnp.where(lane_f == idxs[j][p], 1.0, 0.0) for p in parts]
    ri = lax.broadcasted_iota(jnp.int32, (tp, tp), 0)
    ci = lax.broadcasted_iota(jnp.int32, (tp, tp), 1)
    before = jnp.where(ci < ri, 1.0, 0.0).astype(BF16)
    within = [jnp.dot(before, hot[p].astype(BF16), preferred_element_type=F32) for p in parts]
    base = [cnt_ref[...]]
    for p in parts:
        base.append(base[p] + jnp.sum(hot[p], 0, keepdims=True))
    cnt_ref[...] = base[-1]
    cnt_out_ref[...] = base[-1].astype(jnp.int32)
    pad = jnp.zeros((tcnt_ref.shape[0] - len(parts), LANES), F32)
    tcnt_ref[...] = jnp.concatenate(base[:-1] + [pad], axis=0).astype(jnp.int32)

    for p in parts:
        prefix = within[p] + base[p]
        idx_out = jnp.zeros((tp, LANES), F32)
        gate_out = jnp.zeros((tp, LANES), F32)
        for j in range(TOP_K):
            rank = jnp.sum(jnp.where(lane_f == idxs[j][p], prefix, 0.0), -1, keepdims=True)
            idx_out = jnp.where(lane == j, idxs[j][p], idx_out)
            idx_out = jnp.where(lane == TOP_K + j, rank, idx_out)
            gate_out = jnp.where(lane == j, exps[j][p] * inv[p], gate_out)
        idx_ref[rp[p], :] = idx_out.astype(jnp.int32)
        gate_ref[rp[p], :] = gate_out


def _outproj(og, od, x2d, w_out_b, ln_g, ln_b, rw_pad, rb_pad):
    n = x2d.shape[0]
    tm = TM_PROJ
    row = lambda w: pl.BlockSpec((tm, w), lambda i: (i, 0))
    full = lambda r, c: pl.BlockSpec((r, c), lambda i: (0, 0))
    return pl.pallas_call(
        _outproj_kernel,
        grid=(n // tm,),
        in_specs=[row(GDN_W), row(DIFF_W), row(D_MODEL), full(GDN_W + DIFF_W, D_MODEL),
                  full(1, D_MODEL), full(1, D_MODEL), full(3 * D_MODEL, LANES), full(1, LANES)],
        out_specs=[row(D_MODEL), row(LANES), row(LANES), full(1, LANES),
                   pl.BlockSpec((8, LANES), lambda i: (i, 0))],
        out_shape=[jax.ShapeDtypeStruct((n, D_MODEL), F32),
                   jax.ShapeDtypeStruct((n, LANES), jnp.int32),
                   jax.ShapeDtypeStruct((n, LANES), F32),
                   jax.ShapeDtypeStruct((1, LANES), jnp.int32),
                   jax.ShapeDtypeStruct((n // tm * 8, LANES), jnp.int32)],
        scratch_shapes=[pltpu.VMEM((1, LANES), F32)],
        compiler_params=pltpu.CompilerParams(dimension_semantics=("arbitrary",),
                                             vmem_limit_bytes=VMEM_LIMIT),
        name="outproj_ln1_router",
    )(og, od, x2d, w_out_b, ln_g.reshape(1, -1), ln_b.reshape(1, -1), rw_pad, rb_pad)


def _dispatch_kernel(pos_ref, h_ref, xs_hbm, sem):
    tc = h_ref.shape[0]
    base = pl.program_id(0) * (tc * TOP_K)

    def body(r, carry):
        for j in range(TOP_K):
            s = pos_ref[base + r * TOP_K + j]
            pltpu.make_async_copy(h_ref.at[pl.ds(r, 1), :], xs_hbm.at[pl.ds(s, 1), :], sem).start()
        return carry

    lax.fori_loop(0, tc, body, 0, unroll=4)
    for j in range(TOP_K):
        pltpu.make_async_copy(h_ref, xs_hbm.at[pl.ds(0, tc), :], sem).wait()


def _dispatch(pos, h1):
    n = h1.shape[0]
    tc = TC_DISP
    grid_spec = pltpu.PrefetchScalarGridSpec(
        num_scalar_prefetch=1,
        grid=(n // tc,),
        in_specs=[pl.BlockSpec((tc, D_MODEL), lambda i, pos: (i, 0))],
        out_specs=pl.BlockSpec(memory_space=pl.ANY),
        scratch_shapes=[pltpu.SemaphoreType.DMA],
    )
    return pl.pallas_call(
        _dispatch_kernel,
        grid_spec=grid_spec,
        out_shape=jax.ShapeDtypeStruct((n * TOP_K, D_MODEL), F32),
        compiler_params=pltpu.CompilerParams(dimension_semantics=("arbitrary",),
                                             vmem_limit_bytes=VMEM_LIMIT),
        name="dispatch",
    )(pos, h1)


def _moe_kernel(we_ref, wt_ref, lo_ref, hi_ref, nw_ref, slot_ref, nxt_ref, xs_ref, wgu_hbm, bgu_ref, wd_hbm,
                bd_ref, ys_ref, wgu_f, wd_f, wgu_b, wd_b, acc_ref, sem):
    tm = ys_ref.shape[0]
    w = pl.program_id(0)

    def weights_copy(e, s):
        return (pltpu.make_async_copy(wgu_hbm.at[e], wgu_f.at[s], sem.at[s, 0]),
                pltpu.make_async_copy(wd_hbm.at[e], wd_f.at[s], sem.at[s, 1]))

    @pl.when(w == 0)
    def _():
        acc_ref[...] = jnp.zeros(acc_ref.shape, F32)
        for cp in weights_copy(we_ref[0], slot_ref[0]):
            cp.start()

    @pl.when(w < nw_ref[0])
    def _():
        prev = jnp.maximum(w - 1, 0)

        @pl.when((w == 0) | (we_ref[w] != we_ref[prev]))
        def _():
            s = slot_ref[w]
            for cp in weights_copy(we_ref[w], s):
                cp.wait()

            @pl.when(nxt_ref[w] >= 0)
            def _():
                for cp in weights_copy(nxt_ref[w], 1 - s):
                    cp.start()

            rows = 128

            def cast(j, carry):
                r0 = pl.multiple_of(j * rows, rows)
                wgu_b[pl.ds(r0, rows), :] = wgu_f[s, pl.ds(r0, rows), :].astype(BF16)
                wd_b[pl.ds(r0, rows), :] = wd_f[s, pl.ds(r0, rows), :].astype(BF16)
                return carry

            lax.fori_loop(0, D_MODEL // rows, cast, 0)

        xb = xs_ref[...].astype(BF16)
        hgu = jnp.dot(xb, wgu_b[...], preferred_element_type=F32) + bgu_ref[0]
        gate = jnp.minimum(hgu[:, :D_FF], SWIGLU_LIMIT)
        up = jnp.clip(hgu[:, D_FF:], -SWIGLU_LIMIT, SWIGLU_LIMIT)
        act = (up + 1.0) * gate * _sigmoid(SWIGLU_ALPHA * gate)
        y = jnp.dot(act.astype(BF16), wd_b[...], preferred_element_type=F32) + bd_ref[0]
        rid = lax.broadcasted_iota(jnp.int32, (tm, 1), 0)
        mine = jnp.where(rid >= lo_ref[w], jnp.where(rid < hi_ref[w], 1.0, 0.0), 0.0) > 0.5
        keep = jnp.where((w == 0) | (wt_ref[w] != wt_ref[prev]), 0.0, 1.0)
        merged = jnp.where(mine, y, acc_ref[...] * keep)
        acc_ref[...] = merged
        ys_ref[...] = merged


def _moe(work, xs, w_gu, b_gu, w_down, b_down):
    tm = TM_MOE
    m = xs.shape[0]
    n_items = m // tm + N_EXPERTS - 1
    widx = lambda f: (lambda w, we, wt, lo, hi, nw, slot, nxt: f(w, we, wt))
    grid_spec = pltpu.PrefetchScalarGridSpec(
        num_scalar_prefetch=7,
        grid=(n_items,),
        in_specs=[pl.BlockSpec((tm, D_MODEL), widx(lambda w, we, wt: (wt[w], 0))),
                  pl.BlockSpec(memory_space=pl.ANY),
                  pl.BlockSpec((1, 1, 2 * D_FF), widx(lambda w, we, wt: (we[w], 0, 0))),
                  pl.BlockSpec(memory_space=pl.ANY),
                  pl.BlockSpec((1, 1, D_MODEL), widx(lambda w, we, wt: (we[w], 0, 0)))],
        out_specs=pl.BlockSpec((tm, D_MODEL), widx(lambda w, we, wt: (wt[w], 0))),
        scratch_shapes=[pltpu.VMEM((2, D_MODEL, 2 * D_FF), F32),
                        pltpu.VMEM((2, D_FF, D_MODEL), F32),
                        pltpu.VMEM((D_MODEL, 2 * D_FF), BF16),
                        pltpu.VMEM((D_FF, D_MODEL), BF16),
                        pltpu.VMEM((tm, D_MODEL), F32),
                        pltpu.SemaphoreType.DMA((2, 2))],
    )
    return pl.pallas_call(
        _moe_kernel,
        grid_spec=grid_spec,
        out_shape=jax.ShapeDtypeStruct((m, D_MODEL), F32),
        compiler_params=pltpu.CompilerParams(dimension_semantics=("arbitrary",),
                                             vmem_limit_bytes=VMEM_LIMIT),
        name="moe_ffn",
    )(*work, xs, w_gu, b_gu.reshape(N_EXPERTS, 1, -1), w_down, b_down.reshape(N_EXPERTS, 1, -1))


def _combine_kernel(cs_ref, nch_ref, h_ref, gate_ref, pos_ref, bs_ref, p_ref, ys_hbm, g2_ref, b2_ref, wg_ref,
                    bg_ref, wp_ref, g3_ref, b3_ref, o_ref, ybuf, sem):
    i = pl.program_id(0)
    n = pl.num_programs(0)

    def issue(t, b):
        def body(c, carry):
            dst = pl.multiple_of(c * COMB_CH, COMB_CH)
            src = pl.multiple_of(cs_ref[t * COMB_MAXCH + c], SUBLANES)
            pltpu.make_async_copy(ys_hbm.at[pl.ds(src, COMB_CH), :],
                                  ybuf.at[b, pl.ds(dst, COMB_CH), :], sem.at[b]).start()
            return carry

        lax.fori_loop(0, nch_ref[t], body, 0)

    @pl.when(i == 0)
    def _():
        def zero(j, carry):
            r0 = pl.multiple_of(j * COMB_CH, COMB_CH)
            for b in range(2):
                ybuf[b, pl.ds(r0, COMB_CH), :] = jnp.zeros((COMB_CH, D_MODEL), F32)
            return carry

        lax.fori_loop(0, COMB_MAXCH, zero, 0)
        issue(0, 0)

    @pl.when(i + 1 < n)
    def _():
        issue(i + 1, (i + 1) % 2)

    buf = i % 2

    def wait(c, carry):
        pltpu.make_async_copy(ys_hbm.at[pl.ds(0, COMB_CH), :], ybuf.at[buf, pl.ds(0, COMB_CH), :],
                              sem.at[buf]).wait()
        return carry

    lax.fori_loop(0, nch_ref[i], wait, 0)
    slots = bs_ref[0]
    ybf = ybuf[buf].astype(BF16)
    tp = h_ref.shape[0] // 2
    parts = range(2)
    rp = [slice(tp * p, tp * (p + 1)) for p in parts]
    sel = [jnp.zeros((tp, COMB_MAXCH * COMB_CH), F32) for p in parts]
    for j in range(TOP_K):
        sel = [jnp.where(pos_ref[rp[p], j:j + 1] == slots, gate_ref[rp[p], j:j + 1], sel[p]) for p in parts]
    ffn = [jnp.dot(sel[p].astype(BF16), ybf, preferred_element_type=F32) for p in parts]
    h2 = [_layer_norm(DN_ALPHA * h_ref[rp[p], :] + ffn[p], g2_ref[...], b2_ref[...]) for p in parts]
    gate = [_sigmoid(jnp.dot(h2[p].astype(BF16), wg_ref[...], preferred_element_type=F32) + bg_ref[...])
            for p in parts]
    ple = [gate[p] * jnp.dot(p_ref[rp[p], :].astype(BF16), wp_ref[...], preferred_element_type=F32) for p in parts]
    for p in parts:
        o_ref[rp[p], :] = _layer_norm(DN_ALPHA * h2[p] + ple[p], g3_ref[...], b3_ref[...])


def _combine(chunks, pos2d, h1, gates, p2d, ys, g2, b2, wg_b, bg, wp_b, g3, b3):
    n = h1.shape[0]
    tc = TC_COMB
    cstart, nch, bufslot = chunks
    row = lambda w: pl.BlockSpec((tc, w), lambda i, cs, nc: (i, 0))
    full = lambda r, c: pl.BlockSpec((r, c), lambda i, cs, nc: (0, 0))
    grid_spec = pltpu.PrefetchScalarGridSpec(
        num_scalar_prefetch=2,
        grid=(n // tc,),
        in_specs=[row(D_MODEL), row(LANES), row(TOP_K),
                  pl.BlockSpec((None, 1, COMB_MAXCH * COMB_CH), lambda i, cs, nc: (i, 0, 0)),
                  row(PLE_DIM), pl.BlockSpec(memory_space=pl.ANY),
                  full(1, D_MODEL), full(1, D_MODEL), full(D_MODEL, D_MODEL), full(1, D_MODEL),
                  full(PLE_DIM, D_MODEL), full(1, D_MODEL), full(1, D_MODEL)],
        out_specs=row(D_MODEL),
        scratch_shapes=[pltpu.VMEM((2, COMB_MAXCH * COMB_CH, D_MODEL), F32),
                        pltpu.SemaphoreType.DMA((2,))],
    )
    return pl.pallas_call(
        _combine_kernel,
        grid_spec=grid_spec,
        out_shape=jax.ShapeDtypeStruct((n, D_MODEL), F32),
        compiler_params=pltpu.CompilerParams(dimension_semantics=("arbitrary",),
                                             vmem_limit_bytes=VMEM_LIMIT),
        name="combine_ln2_ple_ln3",
    )(cstart, nch, h1, gates, pos2d, bufslot, p2d, ys, g2.reshape(1, -1), b2.reshape(1, -1), wg_b,
      bg.reshape(1, -1), wp_b, g3.reshape(1, -1), b3.reshape(1, -1))


def _combine_chunks(cntb, counts, grp_start, m):
    n_tiles = cntb.shape[0]
    nxt = jnp.concatenate([cntb[1:], counts[None, :]], axis=0)
    length = nxt - cntb
    first = grp_start[None, :] + cntb
    base = (first // SUBLANES) * SUBLANES
    nq = jnp.where(length > 0, (first - base + length + COMB_CH - 1) // COMB_CH, 0)
    q_end = jnp.cumsum(nq, axis=1)
    q_start = q_end - nq
    nch = q_end[:, -1]
    c = jnp.arange(COMB_MAXCH, dtype=jnp.int32)
    e_c = jnp.clip(jnp.sum((q_end[:, None, :] <= c[None, :, None]).astype(jnp.int32), -1), 0, N_EXPERTS - 1)
    onehot = e_c[:, :, None] == jnp.arange(N_EXPERTS, dtype=jnp.int32)
    pick = lambda a: jnp.sum(jnp.where(onehot, a[:, None, :], 0), -1)
    want = pick(base) + COMB_CH * (c[None, :] - pick(q_start))
    start = jnp.minimum(want, m - COMB_CH)
    valid = c[None, :] < nch[:, None]
    cstart = jnp.where(valid, start, 0).astype(jnp.int32).reshape(-1)
    rows = start[:, :, None] + jnp.arange(COMB_CH, dtype=jnp.int32)
    lo = jnp.maximum(want, pick(first))[:, :, None]
    end = (pick(first) + pick(length))[:, :, None]
    own = valid[:, :, None] & (rows >= lo) & (rows < end)
    bufslot = jnp.where(own, rows, -1).astype(jnp.int32).reshape(n_tiles, 1, COMB_MAXCH * COMB_CH)
    return cstart, nch.astype(jnp.int32), bufslot


def _routing(idx, rank, counts, tm):
    n_tok = idx.shape[0]
    m = n_tok * TOP_K
    n_tiles = m // tm
    n_items = n_tiles + N_EXPERTS - 1
    grp_end = jnp.cumsum(counts)
    grp_start = grp_end - counts
    experts = jnp.arange(N_EXPERTS, dtype=jnp.int32)
    pos = (jnp.sum(jnp.where(idx[:, :, None] == experts, grp_start, 0), -1) + rank).reshape(-1).astype(jnp.int32)
    t_first = grp_start // tm
    t_last = (grp_end - 1) // tm
    n_e = jnp.where(counts > 0, t_last - t_first + 1, 0)
    w_end = jnp.cumsum(n_e)
    w_start = w_end - n_e
    n_work = w_end[-1]
    w = jnp.arange(n_items, dtype=jnp.int32)
    valid = w < n_work
    wq = jnp.minimum(w, n_work - 1)
    e_w = jnp.clip(jnp.sum((w_end[None, :] <= wq[:, None]).astype(jnp.int32), -1), 0, N_EXPERTS - 1)
    onehot = e_w[:, None] == experts
    pick = lambda a: jnp.sum(jnp.where(onehot, a[None, :], 0), -1)
    tile_w = pick(t_first) + (wq - pick(w_start))
    lo = jnp.clip(pick(grp_start) - tile_w * tm, 0, tm)
    hi = jnp.clip(pick(grp_end) - tile_w * tm, 0, tm)
    lo = jnp.where(valid, lo, 0)
    hi = jnp.where(valid, hi, 0)
    nonempty = counts > 0
    slot_e = (jnp.cumsum(nonempty.astype(jnp.int32)) - 1) % 2
    later = nonempty[None, :] & (experts[None, :] > experts[:, None])
    next_e = jnp.min(jnp.where(later, experts[None, :], N_EXPERTS), axis=1)
    next_e = jnp.where(next_e < N_EXPERTS, next_e, -1)
    i32 = lambda a: a.astype(jnp.int32)
    return pos, i32(grp_start), (i32(e_w), i32(tile_w), i32(lo), i32(hi), i32(n_work).reshape(1),
                                 i32(pick(slot_e)), i32(pick(next_e)))


def _layer(h, p_i, cos8, sin8, lam_init, w_in, conv_w, a_log, dt_bias, gdn_norm_w,
           lam_q1, lam_k1, lam_q2, lam_k2, diff_norm_w, w_out, ln1_g, ln1_b,
           router_w, router_b, w_gu, b_gu, w_down, b_down, ln2_g, ln2_b,
           ple_w, ple_gate_w, ple_gate_b, ln3_g, ln3_b):
    batch, seq, d = h.shape
    n = batch * seq
    x2d = h.reshape(n, d)
    o_ba = 4 * GDN_W
    o_d = o_ba + 2 * GDN_HEADS
    w_r = jnp.concatenate([w_in[:, :o_ba], w_in[:, o_d:], w_in[:, o_ba:o_d],
                           jnp.zeros((d, LANES - 2 * GDN_HEADS), w_in.dtype)], -1).astype(BF16)
    a_proj, ba_proj, dq, dk, dv = _inproj(x2d, w_r, jnp.concatenate([cos8, sin8], -1), batch, seq)
    o_gdn = _gdn(a_proj, ba_proj, conv_w, a_log, dt_bias, gdn_norm_w, batch, seq)
    o_diff = _diff_attention(dq, dk, dv, lam_q1, lam_k1, lam_q2, lam_k2, diff_norm_w, lam_init, batch, seq)
    rw_pad = jnp.concatenate([router_w, jnp.zeros((d, LANES - N_EXPERTS), F32)], -1)
    rw_hi = rw_pad.astype(BF16)
    rw_lo = (rw_pad - rw_hi.astype(F32)).astype(BF16)
    rw_pad = jnp.concatenate([rw_hi, rw_lo, rw_hi], axis=0)
    rb_pad = jnp.concatenate([router_b, jnp.full((LANES - N_EXPERTS,), -jnp.inf, F32)]).reshape(1, LANES)
    h1, idx, gates, cnt, tcnt = _outproj(o_gdn, o_diff, x2d, w_out.astype(BF16), ln1_g, ln1_b, rw_pad, rb_pad)
    counts = cnt[0, :N_EXPERTS]
    pos, grp_start, work = _routing(idx[:, :TOP_K], idx[:, TOP_K:2 * TOP_K], counts, TM_MOE)
    xs = _dispatch(pos, h1)
    ys = _moe(work, xs, w_gu, b_gu, w_down, b_down)
    per_step = TM_PROJ // TC_COMB
    cntb = tcnt.reshape(n // TM_PROJ, 8, LANES)[:, :per_step, :N_EXPERTS].reshape(n // TC_COMB, N_EXPERTS)
    chunks = _combine_chunks(cntb, counts, grp_start, n * TOP_K)
    out = _combine(chunks, pos.reshape(n, TOP_K), h1, gates, p_i.reshape(n, PLE_DIM), ys, ln2_g, ln2_b,
                   ple_gate_w.astype(BF16), ple_gate_b, ple_w.astype(BF16), ln3_g, ln3_b)
    return out.reshape(batch, seq, d)


def kernel(x, p, positions, w_in, conv_w, a_log, dt_bias, gdn_norm_w, lam_q1, lam_k1, lam_q2, lam_k2,
           diff_norm_w, w_out, ln1_g, ln1_b, router_w, router_b, w_gu, b_gu, w_down, b_down, ln2_g, ln2_b,
           ple_w, ple_gate_w, ple_gate_b, ln3_g, ln3_b):
    batch, seq, _ = x.shape
    inv_freq = ROPE_THETA ** (-jnp.arange(0, ROPE_DIM, 2, dtype=F32) / ROPE_DIM)
    ang = (positions.astype(F32)[..., None] * inv_freq).reshape(batch * seq, ROPE_DIM // 2)
    cos8 = jnp.cos(ang)
    sin8 = jnp.sin(ang)
    h = x
    for i in range(w_in.shape[0]):
        lam_init = 0.8 - 0.6 * math.exp(-0.3 * i)
        h = _layer(h, p[i], cos8, sin8, lam_init, w_in[i], conv_w[i], a_log[i], dt_bias[i], gdn_norm_w[i],
                   lam_q1[i], lam_k1[i], lam_q2[i], lam_k2[i], diff_norm_w[i], w_out[i], ln1_g[i], ln1_b[i],
                   router_w[i], router_b[i], w_gu[i], b_gu[i], w_down[i], b_down[i], ln2_g[i], ln2_b[i],
                   ple_w[i], ple_gate_w[i], ple_gate_b[i], ln3_g[i], ln3_b[i])
    return h
```

```python
import functools
import math

import jax
import jax.numpy as jnp
import numpy as np
from jax import lax
from jax.experimental import pallas as pl
from jax.experimental.pallas import tpu as pltpu

F32 = jnp.float32
BF16 = jnp.bfloat16
LOG2E = 1.4426950408889634

D_MODEL = 1024
PLE_DIM = 256
GDN_HEADS = 4
GDN_DK = 128
GDN_DV = 128
CONV_WIDTH = 4
CHUNK = 64
DIFF_HEADS = 4
DIFF_D = 64
DIFF_DV = 2 * DIFF_D
ROPE_THETA = 500000.0
ROPE_DIM = DIFF_D // 4
N_EXPERTS = 32
TOP_K = 4
D_FF = D_MODEL
SWIGLU_LIMIT = 7.0
SWIGLU_ALPHA = 1.702
DEPTH = 1
DN_ALPHA = (2 * DEPTH) ** 0.25
LN_EPS = 1e-5
RMS_EPS = 1e-6

LANES = 128
SUBLANES = 8
GDN_W = GDN_HEADS * GDN_DK
CONV_CH = 3 * GDN_W
A_COLS = 4 * GDN_W
DIFF_W = DIFF_HEADS * DIFF_DV
IN_PAD_W = A_COLS + 3 * DIFF_W + LANES
HSTACK = GDN_HEADS * CHUNK
VT_ROWS = DIFF_DV + 16
SUBBLK = 16

VMEM_LIMIT = 56 * 1024 * 1024

TM_PROJ = 512
T_GDN = 1024
GDN_GROUP = 4
TQ = 512
TKV = 512
TM_MOE = 256
TC_DISP = 512
TC_COMB = 256
COMB_CH = 32
COMB_MAXCH = 72
assert COMB_MAXCH >= (TC_COMB * TOP_K + N_EXPERTS * (SUBLANES - 1 + COMB_CH - 1)) // COMB_CH
assert (COMB_MAXCH * COMB_CH) % LANES == 0


def _layer_norm(y, g, b):
    mu = jnp.mean(y, -1, keepdims=True)
    d = y - mu
    var = jnp.mean(d * d, -1, keepdims=True)
    return d * lax.rsqrt(var + LN_EPS) * g + b


def _sigmoid(x):
    return 1.0 / (1.0 + jnp.exp(-x))


def _inproj_kernel(x_ref, w_ref, cs_ref, e_ref, a_ref, ba_ref, q_ref, k_ref, vt_ref):
    xb = x_ref[...].astype(BF16)
    a_ref[...] = jnp.dot(xb, w_ref[:, :A_COLS], preferred_element_type=F32)
    ba_ref[...] = jnp.dot(xb, w_ref[:, A_COLS + 3 * DIFF_W:], preferred_element_type=F32)
    cs = cs_ref[...]
    p1 = cs.astype(BF16)
    r1 = cs - p1.astype(F32)
    p2 = r1.astype(BF16)
    p3 = (r1 - p2.astype(F32)).astype(BF16)
    tabs = (jnp.dot(p1, e_ref[...], preferred_element_type=F32) + jnp.dot(p2, e_ref[...], preferred_element_type=F32)
            + jnp.dot(p3, e_ref[...], preferred_element_type=F32))
    lane = lax.broadcasted_iota(jnp.int32, (1, LANES), 1)
    c = tabs[:, :LANES] + jnp.where((lane & (DIFF_D - 1)) >= ROPE_DIM, 1.0, 0.0)
    sa = tabs[:, LANES:2 * LANES]
    sb = tabs[:, 2 * LANES:]

    def rot(t):
        return t * c + pltpu.roll(t, 8, 1) * sa + pltpu.roll(t, LANES - 8, 1) * sb

    qk = jnp.dot(xb, w_ref[:, A_COLS:A_COLS + 2 * DIFF_W], preferred_element_type=F32)
    for h in range(DIFF_HEADS):
        q = qk[:, LANES * h:LANES * (h + 1)]
        q_ref[:, LANES * h:LANES * (h + 1)] = (rot(q) * (DIFF_D ** -0.5 * LOG2E)).astype(BF16)
        k = qk[:, DIFF_W + LANES * h:DIFF_W + LANES * (h + 1)]
        k_ref[:, LANES * h:LANES * (h + 1)] = rot(k).astype(BF16)
    lo = A_COLS + 2 * DIFF_W
    v = jnp.dot(xb, w_ref[:, lo:lo + DIFF_W], preferred_element_type=F32)
    ones = jnp.ones((VT_ROWS - DIFF_DV, v.shape[0]), BF16)
    for h in range(DIFF_HEADS):
        vt_ref[VT_ROWS * h:VT_ROWS * h + DIFF_DV, :] = v[:, DIFF_DV * h:DIFF_DV * (h + 1)].T.astype(BF16)
        vt_ref[VT_ROWS * h + DIFF_DV:VT_ROWS * (h + 1), :] = ones


def _rope_expand():
    half = ROPE_DIM // 2
    e = np.zeros((ROPE_DIM, 3 * LANES), np.float32)
    for lane in range(LANES):
        j = lane % DIFF_D
        if j < ROPE_DIM:
            e[j % half, lane] = 1.0
            if j >= half:
                e[half + j % half, LANES + lane] = 1.0
            else:
                e[half + j % half, 2 * LANES + lane] = -1.0
    return jnp.asarray(e, BF16)


def _inproj(x2d, w_r, cs16, batch, seq):
    n = x2d.shape[0]
    tm = TM_PROJ
    per_b = seq // tm
    row = lambda w: pl.BlockSpec((tm, w), lambda i: (i, 0))
    return pl.pallas_call(
        _inproj_kernel,
        grid=(n // tm,),
        in_specs=[row(D_MODEL),
                  pl.BlockSpec((D_MODEL, IN_PAD_W), lambda i: (0, 0)),
                  row(ROPE_DIM),
                  pl.BlockSpec((ROPE_DIM, 3 * LANES), lambda i: (0, 0))],
        out_specs=[row(A_COLS), row(LANES), row(DIFF_W), row(DIFF_W),
                   pl.BlockSpec((None, DIFF_HEADS * VT_ROWS, tm), lambda i: (i // per_b, 0, i % per_b))],
        out_shape=[jax.ShapeDtypeStruct((n, A_COLS), F32),
                   jax.ShapeDtypeStruct((n, LANES), F32),
                   jax.ShapeDtypeStruct((n, DIFF_W), BF16),
                   jax.ShapeDtypeStruct((n, DIFF_W), BF16),
                   jax.ShapeDtypeStruct((batch, DIFF_HEADS * VT_ROWS, seq), BF16)],
        compiler_params=pltpu.CompilerParams(dimension_semantics=("arbitrary",),
                                             vmem_limit_bytes=VMEM_LIMIT),
        name="inproj",
    )(x2d, w_r, cs16, _rope_expand())


def _mm(a, b):
    return jnp.dot(a.astype(BF16), b.astype(BF16), preferred_element_type=F32)


def _mm16(a, b):
    return jnp.dot(a, b, preferred_element_type=F32).astype(BF16)


def _gdn_kernel(a_ref, ba_ref, cw_ref, aux_ref, nw_ref, o_ref, xe_ref, q_s, k_s, v_s, state_ref):
    t_rows = a_ref.shape[0]
    st = pl.program_id(1)

    @pl.when(st == 0)
    def _():
        xe_ref[0:8, :] = jnp.zeros((8, CONV_CH), F32)
        state_ref[...] = jnp.zeros(state_ref.shape, F32)

    xe_ref[8:8 + t_rows, :] = a_ref[:, :CONV_CH]

    def phase1(r0, nrows):
        for s in range(CONV_CH // LANES):
            cs = slice(LANES * s, LANES * (s + 1))
            xe = xe_ref[r0:r0 + nrows + 8, cs]
            y = xe[8:] * cw_ref[CONV_WIDTH - 1:CONV_WIDTH, cs]
            for j in range(CONV_WIDTH - 1):
                y = y + pltpu.roll(xe, CONV_WIDTH - 1 - j, 0)[8:] * cw_ref[j:j + 1, cs]
            y = y * _sigmoid(y)
            grp, h = divmod(s, GDN_HEADS)
            hs = slice(LANES * h, LANES * (h + 1))
            if grp == 0:
                q_s[r0:r0 + nrows, hs] = (y * lax.rsqrt(jnp.sum(y * y, -1, keepdims=True) + 1e-6)
                                          * (GDN_DK ** -0.5))
            elif grp == 1:
                k_s[r0:r0 + nrows, hs] = y * lax.rsqrt(jnp.sum(y * y, -1, keepdims=True) + 1e-6)
            else:
                v_s[r0:r0 + nrows, hs] = y
            yield

    ba = ba_ref[...]
    beta_t = _sigmoid(ba)
    gx = ba + aux_ref[1:2, :]
    g_t = -jnp.exp(aux_ref[0:1, :]) * (jnp.maximum(gx, 0.0) + jnp.log(1.0 + jnp.exp(-jnp.abs(gx))))

    ri = lax.broadcasted_iota(jnp.int32, (HSTACK, HSTACK), 0)
    ci = lax.broadcasted_iota(jnp.int32, (HSTACK, HSTACK), 1)
    head_start = ri - (ri & (CHUNK - 1))
    in_head = ci >= head_start
    incl_f = jnp.where(in_head, jnp.where(ci <= ri, 1.0, 0.0), 0.0)
    strict_f = jnp.where(in_head, jnp.where(ci < ri, 1.0, 0.0), 0.0)
    sub_f = jnp.where(ci >= ri - (ri & (SUBBLK - 1)), 1.0, 0.0)
    eye = jnp.where(ri == ci, 1.0, 0.0)
    tri_b = (lax.broadcasted_iota(jnp.int32, (CHUNK, CHUNK), 0)
             >= lax.broadcasted_iota(jnp.int32, (CHUNK, CHUNK), 1)).astype(BF16)
    nt = (((1,), (1,)), ((), ()))
    tn = (((0,), (0,)), ((), ()))
    bdot = functools.partial(jnp.dot, preferred_element_type=F32)

    def stack(fn):
        return jnp.concatenate([fn(h) for h in range(GDN_HEADS)], axis=0)

    crow = [slice(CHUNK * c, CHUNK * (c + 1)) for c in range(t_rows // CHUNK)]

    def cumdecay(g):
        g1 = g.astype(BF16)
        r1 = g - g1.astype(F32)
        g2 = r1.astype(BF16)
        g3 = (r1 - g2.astype(F32)).astype(BF16)
        return bdot(tri_b, g1) + bdot(tri_b, g2) + bdot(tri_b, g3)

    eye16 = eye.astype(BF16)
    res = {}

    def phase2(chunks):
        each = lambda fn: {c: fn(c) for c in chunks}
        gc = each(lambda c: cumdecay(g_t[crow[c]]))
        yield
        gct = each(lambda c: gc[c].T)
        gcol = each(lambda c: stack(lambda h: jnp.broadcast_to(
            gc[c][:, GDN_HEADS + h:GDN_HEADS + h + 1], (CHUNK, LANES))))
        glast = each(lambda c: stack(lambda h: jnp.broadcast_to(
            gc[c][CHUNK - 1:CHUNK, GDN_HEADS + h:GDN_HEADS + h + 1], (CHUNK, LANES))))
        bcol = each(lambda c: stack(lambda h: jnp.broadcast_to(beta_t[crow[c], h:h + 1], (CHUNK, LANES))))
        grow = each(lambda c: jnp.concatenate(
            [gct[c][GDN_HEADS + h:GDN_HEADS + h + 1, :] for h in range(GDN_HEADS)], axis=1))
        yield
        kk = each(lambda c: stack(lambda h: k_s[crow[c], LANES * h:LANES * (h + 1)]))
        qq = each(lambda c: stack(lambda h: q_s[crow[c], LANES * h:LANES * (h + 1)]))
        vv = each(lambda c: stack(lambda h: v_s[crow[c], LANES * h:LANES * (h + 1)]))
        dec = each(lambda c: jnp.exp(jnp.minimum(jnp.concatenate([gcol[c], gcol[c]], axis=1) - grow[c], 0.0))
                   * incl_f)
        yield
        kb = each(lambda c: kk[c] * bcol[c])
        k16 = each(lambda c: kk[c].astype(BF16))
        amat = each(lambda c: lax.dot_general(kb[c].astype(BF16), k16[c], nt, preferred_element_type=F32)
                    * dec[c] * strict_f)
        yield
        f = each(lambda c: amat[c] * sub_f)
        bm = each(lambda c: f[c].astype(BF16))
        nm = each(lambda c: (amat[c] - f[c]).astype(BF16))
        b2 = each(lambda c: _mm16(bm[c], bm[c]))
        yield
        b4 = each(lambda c: _mm16(b2[c], b2[c]))
        yield
        b8 = each(lambda c: _mm16(b4[c], b4[c]))
        yield
        d = each(lambda c: _mm16(eye16 - bm[c], eye16 + b2[c]))
        yield
        d = each(lambda c: _mm16(d[c], eye16 + b4[c]))
        yield
        dinv = each(lambda c: _mm16(d[c], eye16 + b8[c]))
        yield
        mm = each(lambda c: _mm16(dinv[c], nm[c]))
        yield
        m2 = each(lambda c: _mm16(mm[c], mm[c]))
        yield
        t = each(lambda c: _mm16(eye16 - mm[c], eye16 + m2[c]))
        yield
        tinv = each(lambda c: _mm16(t[c], dinv[c]))
        eg = each(lambda c: jnp.exp(gcol[c]))
        yield
        sol = each(lambda c: _mm(tinv[c], jnp.concatenate([vv[c] * bcol[c], kb[c] * eg[c]], axis=1)))
        qk = each(lambda c: lax.dot_general(qq[c].astype(BF16), k16[c], nt, preferred_element_type=F32) * dec[c])
        yield
        for c in chunks:
            res[c] = (gc[c], sol[c], qk[c], qq[c] * eg[c], kk[c] * jnp.exp(glast[c] - gcol[c]))
        yield

    def phase3(chunks):
        for c in chunks:
            rows = crow[c]
            gc, sol, qk, qd, kd = res.pop(c)
            u = sol[:, :LANES]
            w = sol[:, LANES:]
            ws, qs = [], []
            for h in range(GDN_HEADS):
                hr = slice(CHUNK * h, CHUNK * (h + 1))
                lhs = jnp.concatenate([w[hr], qd[hr]], axis=0).astype(BF16)
                r = bdot(lhs, state_ref[h].astype(BF16))
                ws.append(r[:CHUNK])
                qs.append(r[CHUNK:])
            yield
            vn = u - jnp.concatenate(ws, axis=0)
            vn16 = vn.astype(BF16)
            o = jnp.concatenate(qs, axis=0) + bdot(qk.astype(BF16), vn16)
            kd16 = kd.astype(BF16)
            for h in range(GDN_HEADS):
                hr = slice(CHUNK * h, CHUNK * (h + 1))
                gl = jnp.exp(gc[CHUNK - 1:CHUNK, GDN_HEADS + h:GDN_HEADS + h + 1])
                state_ref[h] = state_ref[h] * gl + lax.dot_general(kd16[hr], vn16[hr], tn,
                                                                   preferred_element_type=F32)
            yield
            for h in range(GDN_HEADS):
                hr = slice(CHUNK * h, CHUNK * (h + 1))
                hs = slice(LANES * h, LANES * (h + 1))
                oh = o[hr]
                z = a_ref[rows, CONV_CH + LANES * h:CONV_CH + LANES * (h + 1)]
                oh = oh * lax.rsqrt(jnp.mean(oh * oh, -1, keepdims=True) + RMS_EPS) * nw_ref[...]
                o_ref[rows, hs] = (oh * (z * _sigmoid(z))).astype(o_ref.dtype)
            yield

    def interleave(*gens):
        live = list(gens)
        while live:
            for g in list(live):
                if next(g, StopIteration) is StopIteration:
                    live.remove(g)

    n_chunks = t_rows // CHUNK
    groups = [list(range(c, c + GDN_GROUP)) for c in range(0, n_chunks, GDN_GROUP)]
    rows_of = lambda grp: (CHUNK * grp[0], CHUNK * len(grp))
    n_grp = len(groups)
    for step in range(n_grp + 2):
        gens = []
        if step < n_grp:
            gens.append(phase1(*rows_of(groups[step])))
        if 0 <= step - 1 < n_grp:
            gens.append(phase2(groups[step - 1]))
        if 0 <= step - 2 < n_grp:
            gens.append(phase3(groups[step - 2]))
        interleave(*gens)
    xe_ref[0:8, :] = xe_ref[t_rows:t_rows + 8, :]


def _gdn(a_proj, ba_proj, conv_w, a_log, dt_bias, norm_w, batch, seq):
    t = T_GDN
    nst = seq // t
    rowblk = lambda w: pl.BlockSpec((t, w), lambda b, s: (b * nst + s, 0))
    aux = jnp.zeros((8, LANES), F32)
    aux = aux.at[0, GDN_HEADS:2 * GDN_HEADS].set(a_log).at[1, GDN_HEADS:2 * GDN_HEADS].set(dt_bias)
    return pl.pallas_call(
        _gdn_kernel,
        grid=(batch, nst),
        in_specs=[rowblk(A_COLS), rowblk(LANES),
                  pl.BlockSpec((CONV_WIDTH, CONV_CH), lambda b, s: (0, 0)),
                  pl.BlockSpec((8, LANES), lambda b, s: (0, 0)),
                  pl.BlockSpec((1, GDN_DV), lambda b, s: (0, 0))],
        out_specs=rowblk(GDN_W),
        out_shape=jax.ShapeDtypeStruct((batch * seq, GDN_W), BF16),
        scratch_shapes=[pltpu.VMEM((t + 8, CONV_CH), F32),
                        pltpu.VMEM((t, GDN_W), F32),
                        pltpu.VMEM((t, GDN_W), F32),
                        pltpu.VMEM((t, GDN_W), F32),
                        pltpu.VMEM((GDN_HEADS, GDN_DK, GDN_DV), F32)],
        compiler_params=pltpu.CompilerParams(dimension_semantics=("arbitrary", "arbitrary"),
                                             vmem_limit_bytes=VMEM_LIMIT),
        name="gdn",
    )(a_proj, ba_proj, conv_w, aux, norm_w.reshape(1, GDN_DV))


def _attn_kernel(q_ref, k_ref, vt_ref, lq1_ref, lk1_ref, lq2_ref, lk2_ref, nw_ref, o_ref, *, lam_init):
    tq = q_ref.shape[0]
    qi = pl.program_id(1)
    lane = lax.broadcasted_iota(jnp.int32, (1, LANES), 1)
    qpos = qi * tq + (lax.broadcasted_iota(jnp.int32, (1, 2 * tq), 1) & (tq - 1))
    nt = (((1,), (1,)), ((), ()))
    q2 = []
    for h in range(DIFF_HEADS):
        q = q_ref[:, LANES * h:LANES * (h + 1)]
        zero = jnp.zeros_like(q)
        q2.append(jnp.concatenate([jnp.where(lane < DIFF_D, q, zero), jnp.where(lane >= DIFF_D, q, zero)], axis=0))

    heads = range(DIFF_HEADS)

    def scores(kj):
        off = pl.multiple_of(kj * TKV, TKV)
        return tuple(lax.dot_general(k_ref[pl.ds(off, TKV), LANES * h:LANES * (h + 1)], q2[h], nt,
                                     preferred_element_type=F32) for h in heads)

    def update(kj, ss, state, masked):
        off = pl.multiple_of(kj * TKV, TKV)
        if masked:
            keep = off + lax.broadcasted_iota(jnp.int32, (TKV, 1), 0) <= qpos
            ss = [jnp.where(keep, s, -1e30) for s in ss]
        mns = [jnp.maximum(state[h][0], jnp.max(ss[h], 0, keepdims=True)) for h in heads]
        ps = [jnp.exp2((ss[h] - mns[h]).astype(BF16)) for h in heads]
        als = [jnp.exp2(state[h][0] - mns[h]) for h in heads]
        pvs = [jnp.dot(vt_ref[VT_ROWS * h:VT_ROWS * (h + 1), pl.ds(off, TKV)], ps[h],
                       preferred_element_type=F32) for h in heads]
        return tuple((mns[h], als[h] * state[h][1] + pvs[h]) for h in heads)

    def body(kj, state):
        return update(kj, scores(kj), state, False)

    init = tuple((jnp.full((1, 2 * tq), -1e30, F32), jnp.zeros((VT_ROWS, 2 * tq), F32)) for _ in heads)
    state = lax.fori_loop(0, qi, body, init)
    carry = update(qi, scores(qi), state, True)
    lam = (jnp.exp(jnp.sum(lq1_ref[...] * lk1_ref[...], -1, keepdims=True))
           - jnp.exp(jnp.sum(lq2_ref[...] * lk2_ref[...], -1, keepdims=True)) + lam_init)
    on = [carry[h][1][:DIFF_DV] * (1.0 / carry[h][1][DIFF_DV:DIFF_DV + 1]) for h in heads]
    ot = [(on[h][:, :tq] - lam * on[h][:, tq:]).T for h in heads]
    inv = [lax.rsqrt(jnp.mean(ot[h] * ot[h], -1, keepdims=True) + RMS_EPS) for h in heads]
    for h in heads:
        o_ref[:, LANES * h:LANES * (h + 1)] = (ot[h] * inv[h] * nw_ref[...] * (1.0 - lam_init)).astype(o_ref.dtype)


def _diff_attention(q, k, v3, lq1, lk1, lq2, lk2, norm_w, lam_init, batch, seq):
    q3 = q.reshape(batch, seq, DIFF_W)
    k3 = k.reshape(batch, seq, DIFF_W)
    small = lambda w: pl.BlockSpec((1, w), lambda b, i: (0, 0))
    out = pl.pallas_call(
        functools.partial(_attn_kernel, lam_init=lam_init),
        grid=(batch, seq // TQ),
        in_specs=[pl.BlockSpec((None, TQ, DIFF_W), lambda b, i: (b, i, 0)),
                  pl.BlockSpec((None, seq, DIFF_W), lambda b, i: (b, 0, 0)),
                  pl.BlockSpec((None, DIFF_HEADS * VT_ROWS, seq), lambda b, i: (b, 0, 0)),
                  small(DIFF_D), small(DIFF_D), small(DIFF_D), small(DIFF_D), small(DIFF_DV)],
        out_specs=pl.BlockSpec((None, TQ, DIFF_W), lambda b, i: (b, i, 0)),
        out_shape=jax.ShapeDtypeStruct((batch, seq, DIFF_W), BF16),
        compiler_params=pltpu.CompilerParams(
            dimension_semantics=("arbitrary", "arbitrary"),
            vmem_limit_bytes=VMEM_LIMIT),
        name="diff_attn",
    )(q3, k3, v3, lq1.reshape(1, -1), lk1.reshape(1, -1), lq2.reshape(1, -1), lk2.reshape(1, -1),
      norm_w.reshape(1, -1))
    return out.reshape(batch * seq, DIFF_W)


def _outproj_kernel(og_ref, od_ref, x_ref, wo_ref, g_ref, b_ref, rw_ref, rb_ref,
                    h_ref, idx_ref, gate_ref, cnt_out_ref, tcnt_ref, cnt_ref):
    @pl.when(pl.program_id(0) == 0)
    def _():
        cnt_ref[...] = jnp.zeros(cnt_ref.shape, F32)

    tm = x_ref.shape[0]
    tp = TC_COMB
    parts = range(tm // tp)
    rp = [slice(tp * p, tp * (p + 1)) for p in parts]
    mix = [jnp.dot(og_ref[rp[p], :], wo_ref[:GDN_W, :], preferred_element_type=F32)
           + jnp.dot(od_ref[rp[p], :], wo_ref[GDN_W:, :], preferred_element_type=F32) for p in parts]
    hs = [_layer_norm(DN_ALPHA * x_ref[rp[p], :] + mix[p], g_ref[...], b_ref[...]) for p in parts]
    for p in parts:
        h_ref[rp[p], :] = hs[p]
    h_hi = [hs[p].astype(BF16) for p in parts]
    h_lo = [(hs[p] - h_hi[p].astype(F32)).astype(BF16) for p in parts]
    work = [jnp.dot(jnp.concatenate([h_hi[p], h_hi[p], h_lo[p]], axis=1), rw_ref[...],
                    preferred_element_type=F32) + rb_ref[...] for p in parts]
    lane = lax.broadcasted_iota(jnp.int32, (tp, LANES), 1)
    lane_f = lane.astype(F32)
    vals, idxs = [], []
    for _ in range(TOP_K):
        m = [jnp.max(work[p], -1, keepdims=True) for p in parts]
        sel = [jnp.min(jnp.where(work[p] == m[p], lane_f, float(LANES)), -1, keepdims=True) for p in parts]
        vals.append(m)
        idxs.append(sel)
        work = [jnp.where(lane_f == sel[p], -jnp.inf, work[p]) for p in parts]
    exps = [[jnp.exp(vals[j][p] - vals[0][p]) for p in parts] for j in range(TOP_K)]
    inv = [1.0 / (exps[0][p] + exps[1][p] + exps[2][p] + exps[3][p]) for p in parts]

    hot = [jnp.zeros((tp, LANES), F32) for p in parts]
    for j in range(TOP_K):
        hot = [hot[p] + jnp.where(lane_f == idxs[j][p], 1.0, 0.0) for p in parts]
    ri = lax.broadcasted_iota(jnp.int32, (tp, tp), 0)
    ci = lax.broadcasted_iota(jnp.int32, (tp, tp), 1)
    before = jnp.where(ci < ri, 1.0, 0.0).astype(BF16)
    within = [jnp.dot(before, hot[p].astype(BF16), preferred_element_type=F32) for p in parts]
    base = [cnt_ref[...]]
    for p in parts:
        base.append(base[p] + jnp.sum(hot[p], 0, keepdims=True))
    cnt_ref[...] = base[-1]
    cnt_out_ref[...] = base[-1].astype(jnp.int32)
    pad = jnp.zeros((tcnt_ref.shape[0] - len(parts), LANES), F32)
    tcnt_ref[...] = jnp.concatenate(base[:-1] + [pad], axis=0).astype(jnp.int32)

    for p in parts:
        prefix = within[p] + base[p]
        idx_out = jnp.zeros((tp, LANES), F32)
        gate_out = jnp.zeros((tp, LANES), F32)
        for j in range(TOP_K):
            rank = jnp.sum(jnp.where(lane_f == idxs[j][p], prefix, 0.0), -1, keepdims=True)
            idx_out = jnp.where(lane == j, idxs[j][p], idx_out)
            idx_out = jnp.where(lane == TOP_K + j, rank, idx_out)
            gate_out = jnp.where(lane == j, exps[j][p] * inv[p], gate_out)
        idx_ref[rp[p], :] = idx_out.astype(jnp.int32)
        gate_ref[rp[p], :] = gate_out


def _outproj(og, od, x2d, w_out_b, ln_g, ln_b, rw_pad, rb_pad):
    n = x2d.shape[0]
    tm = TM_PROJ
    row = lambda w: pl.BlockSpec((tm, w), lambda i: (i, 0))
    full = lambda r, c: pl.BlockSpec((r, c), lambda i: (0, 0))
    return pl.pallas_call(
        _outproj_kernel,
        grid=(n // tm,),
        in_specs=[row(GDN_W), row(DIFF_W), row(D_MODEL), full(GDN_W + DIFF_W, D_MODEL),
                  full(1, D_MODEL), full(1, D_MODEL), full(3 * D_MODEL, LANES), full(1, LANES)],
        out_specs=[row(D_MODEL), row(LANES), row(LANES), full(1, LANES),
                   pl.BlockSpec((8, LANES), lambda i: (i, 0))],
        out_shape=[jax.ShapeDtypeStruct((n, D_MODEL), F32),
                   jax.ShapeDtypeStruct((n, LANES), jnp.int32),
                   jax.ShapeDtypeStruct((n, LANES), F32),
                   jax.ShapeDtypeStruct((1, LANES), jnp.int32),
                   jax.ShapeDtypeStruct((n // tm * 8, LANES), jnp.int32)],
        scratch_shapes=[pltpu.VMEM((1, LANES), F32)],
        compiler_params=pltpu.CompilerParams(dimension_semantics=("arbitrary",),
                                             vmem_limit_bytes=VMEM_LIMIT),
        name="outproj_ln1_router",
    )(og, od, x2d, w_out_b, ln_g.reshape(1, -1), ln_b.reshape(1, -1), rw_pad, rb_pad)


def _dispatch_kernel(pos_ref, h_ref, xs_hbm, sem):
    tc = h_ref.shape[0]
    base = pl.program_id(0) * (tc * TOP_K)

    def body(r, carry):
        for j in range(TOP_K):
            s = pos_ref[base + r * TOP_K + j]
            pltpu.make_async_copy(h_ref.at[pl.ds(r, 1), :], xs_hbm.at[pl.ds(s, 1), :], sem).start()
        return carry

    lax.fori_loop(0, tc, body, 0, unroll=4)
    for j in range(TOP_K):
        pltpu.make_async_copy(h_ref, xs_hbm.at[pl.ds(0, tc), :], sem).wait()


def _dispatch(pos, h1):
    n = h1.shape[0]
    tc = TC_DISP
    grid_spec = pltpu.PrefetchScalarGridSpec(
        num_scalar_prefetch=1,
        grid=(n // tc,),
        in_specs=[pl.BlockSpec((tc, D_MODEL), lambda i, pos: (i, 0))],
        out_specs=pl.BlockSpec(memory_space=pl.ANY),
        scratch_shapes=[pltpu.SemaphoreType.DMA],
    )
    return pl.pallas_call(
        _dispatch_kernel,
        grid_spec=grid_spec,
        out_shape=jax.ShapeDtypeStruct((n * TOP_K, D_MODEL), F32),
        compiler_params=pltpu.CompilerParams(dimension_semantics=("arbitrary",),
                                             vmem_limit_bytes=VMEM_LIMIT),
        name="dispatch",
    )(pos, h1)


def _moe_kernel(we_ref, wt_ref, lo_ref, hi_ref, nw_ref, slot_ref, nxt_ref, xs_ref, wgu_hbm, bgu_ref, wd_hbm,
                bd_ref, ys_ref, wgu_f, wd_f, wgu_b, wd_b, acc_ref, sem):
    tm = ys_ref.shape[0]
    w = pl.program_id(0)

    def weights_copy(e, s):
        return (pltpu.make_async_copy(wgu_hbm.at[e], wgu_f.at[s], sem.at[s, 0]),
                pltpu.make_async_copy(wd_hbm.at[e], wd_f.at[s], sem.at[s, 1]))

    @pl.when(w == 0)
    def _():
        acc_ref[...] = jnp.zeros(acc_ref.shape, F32)
        for cp in weights_copy(we_ref[0], slot_ref[0]):
            cp.start()

    @pl.when(w < nw_ref[0])
    def _():
        prev = jnp.maximum(w - 1, 0)

        @pl.when((w == 0) | (we_ref[w] != we_ref[prev]))
        def _():
            s = slot_ref[w]
            for cp in weights_copy(we_ref[w], s):
                cp.wait()

            @pl.when(nxt_ref[w] >= 0)
            def _():
                for cp in weights_copy(nxt_ref[w], 1 - s):
                    cp.start()

            rows = 128

            def cast(j, carry):
                r0 = pl.multiple_of(j * rows, rows)
                wgu_b[pl.ds(r0, rows), :] = wgu_f[s, pl.ds(r0, rows), :].astype(BF16)
                wd_b[pl.ds(r0, rows), :] = wd_f[s, pl.ds(r0, rows), :].astype(BF16)
                return carry

            lax.fori_loop(0, D_MODEL // rows, cast, 0)

        xb = xs_ref[...].astype(BF16)
        hgu = jnp.dot(xb, wgu_b[...], preferred_element_type=F32) + bgu_ref[0]
        gate = jnp.minimum(hgu[:, :D_FF], SWIGLU_LIMIT)
        up = jnp.clip(hgu[:, D_FF:], -SWIGLU_LIMIT, SWIGLU_LIMIT)
        act = (up + 1.0) * gate * _sigmoid(SWIGLU_ALPHA * gate)
        y = jnp.dot(act.astype(BF16), wd_b[...], preferred_element_type=F32) + bd_ref[0]
        rid = lax.broadcasted_iota(jnp.int32, (tm, 1), 0)
        mine = jnp.where(rid >= lo_ref[w], jnp.where(rid < hi_ref[w], 1.0, 0.0), 0.0) > 0.5
        keep = jnp.where((w == 0) | (wt_ref[w] != wt_ref[prev]), 0.0, 1.0)
        merged = jnp.where(mine, y, acc_ref[...] * keep)
        acc_ref[...] = merged
        ys_ref[...] = merged


def _moe(work, xs, w_gu, b_gu, w_down, b_down):
    tm = TM_MOE
    m = xs.shape[0]
    n_items = m // tm + N_EXPERTS - 1
    widx = lambda f: (lambda w, we, wt, lo, hi, nw, slot, nxt: f(w, we, wt))
    grid_spec = pltpu.PrefetchScalarGridSpec(
        num_scalar_prefetch=7,
        grid=(n_items,),
        in_specs=[pl.BlockSpec((tm, D_MODEL), widx(lambda w, we, wt: (wt[w], 0))),
                  pl.BlockSpec(memory_space=pl.ANY),
                  pl.BlockSpec((1, 1, 2 * D_FF), widx(lambda w, we, wt: (we[w], 0, 0))),
                  pl.BlockSpec(memory_space=pl.ANY),
                  pl.BlockSpec((1, 1, D_MODEL), widx(lambda w, we, wt: (we[w], 0, 0)))],
        out_specs=pl.BlockSpec((tm, D_MODEL), widx(lambda w, we, wt: (wt[w], 0))),
        scratch_shapes=[pltpu.VMEM((2, D_MODEL, 2 * D_FF), F32),
                        pltpu.VMEM((2, D_FF, D_MODEL), F32),
                        pltpu.VMEM((D_MODEL, 2 * D_FF), BF16),
                        pltpu.VMEM((D_FF, D_MODEL), BF16),
                        pltpu.VMEM((tm, D_MODEL), F32),
                        pltpu.SemaphoreType.DMA((2, 2))],
    )
    return pl.pallas_call(
        _moe_kernel,
        grid_spec=grid_spec,
        out_shape=jax.ShapeDtypeStruct((m, D_MODEL), F32),
        compiler_params=pltpu.CompilerParams(dimension_semantics=("arbitrary",),
                                             vmem_limit_bytes=VMEM_LIMIT),
        name="moe_ffn",
    )(*work, xs, w_gu, b_gu.reshape(N_EXPERTS, 1, -1), w_down, b_down.reshape(N_EXPERTS, 1, -1))


def _combine_kernel(cs_ref, nch_ref, h_ref, gate_ref, pos_ref, bs_ref, p_ref, ys_hbm, g2_ref, b2_ref, wg_ref,
                    bg_ref, wp_ref, g3_ref, b3_ref, o_ref, ybuf, sem):
    i = pl.program_id(0)
    n = pl.num_programs(0)

    def issue(t, b):
        def body(c, carry):
            dst = pl.multiple_of(c * COMB_CH, COMB_CH)
            src = pl.multiple_of(cs_ref[t * COMB_MAXCH + c], SUBLANES)
            pltpu.make_async_copy(ys_hbm.at[pl.ds(src, COMB_CH), :],
                                  ybuf.at[b, pl.ds(dst, COMB_CH), :], sem.at[b]).start()
            return carry

        lax.fori_loop(0, nch_ref[t], body, 0)

    @pl.when(i == 0)
    def _():
        def zero(j, carry):
            r0 = pl.multiple_of(j * COMB_CH, COMB_CH)
            for b in range(2):
                ybuf[b, pl.ds(r0, COMB_CH), :] = jnp.zeros((COMB_CH, D_MODEL), F32)
            return carry

        lax.fori_loop(0, COMB_MAXCH, zero, 0)
        issue(0, 0)

    @pl.when(i + 1 < n)
    def _():
        issue(i + 1, (i + 1) % 2)

    buf = i % 2

    def wait(c, carry):
        pltpu.make_async_copy(ys_hbm.at[pl.ds(0, COMB_CH), :], ybuf.at[buf, pl.ds(0, COMB_CH), :],
                              sem.at[buf]).wait()
        return carry

    lax.fori_loop(0, nch_ref[i], wait, 0)
    slots = bs_ref[0]
    ybf = ybuf[buf].astype(BF16)
    tp = h_ref.shape[0] // 2
    parts = range(2)
    rp = [slice(tp * p, tp * (p + 1)) for p in parts]
    sel = [jnp.zeros((tp, COMB_MAXCH * COMB_CH), F32) for p in parts]
    for j in range(TOP_K):
        sel = [jnp.where(pos_ref[rp[p], j:j + 1] == slots, gate_ref[rp[p], j:j + 1], sel[p]) for p in parts]
    ffn = [jnp.dot(sel[p].astype(BF16), ybf, preferred_element_type=F32) for p in parts]
    h2 = [_layer_norm(DN_ALPHA * h_ref[rp[p], :] + ffn[p], g2_ref[...], b2_ref[...]) for p in parts]
    gate = [_sigmoid(jnp.dot(h2[p].astype(BF16), wg_ref[...], preferred_element_type=F32) + bg_ref[...])
            for p in parts]
    ple = [gate[p] * jnp.dot(p_ref[rp[p], :].astype(BF16), wp_ref[...], preferred_element_type=F32) for p in parts]
    for p in parts:
        o_ref[rp[p], :] = _layer_norm(DN_ALPHA * h2[p] + ple[p], g3_ref[...], b3_ref[...])


def _combine(chunks, pos2d, h1, gates, p2d, ys, g2, b2, wg_b, bg, wp_b, g3, b3):
    n = h1.shape[0]
    tc = TC_COMB
    cstart, nch, bufslot = chunks
    row = lambda w: pl.BlockSpec((tc, w), lambda i, cs, nc: (i, 0))
    full = lambda r, c: pl.BlockSpec((r, c), lambda i, cs, nc: (0, 0))
    grid_spec = pltpu.PrefetchScalarGridSpec(
        num_scalar_prefetch=2,
        grid=(n // tc,),
        in_specs=[row(D_MODEL), row(LANES), row(TOP_K),
                  pl.BlockSpec((None, 1, COMB_MAXCH * COMB_CH), lambda i, cs, nc: (i, 0, 0)),
                  row(PLE_DIM), pl.BlockSpec(memory_space=pl.ANY),
                  full(1, D_MODEL), full(1, D_MODEL), full(D_MODEL, D_MODEL), full(1, D_MODEL),
                  full(PLE_DIM, D_MODEL), full(1, D_MODEL), full(1, D_MODEL)],
        out_specs=row(D_MODEL),
        scratch_shapes=[pltpu.VMEM((2, COMB_MAXCH * COMB_CH, D_MODEL), F32),
                        pltpu.SemaphoreType.DMA((2,))],
    )
    return pl.pallas_call(
        _combine_kernel,
        grid_spec=grid_spec,
        out_shape=jax.ShapeDtypeStruct((n, D_MODEL), F32),
        compiler_params=pltpu.CompilerParams(dimension_semantics=("arbitrary",),
                                             vmem_limit_bytes=VMEM_LIMIT),
        name="combine_ln2_ple_ln3",
    )(cstart, nch, h1, gates, pos2d, bufslot, p2d, ys, g2.reshape(1, -1), b2.reshape(1, -1), wg_b,
      bg.reshape(1, -1), wp_b, g3.reshape(1, -1), b3.reshape(1, -1))


def _combine_chunks(cntb, counts, grp_start, m):
    n_tiles = cntb.shape[0]
    nxt = jnp.concatenate([cntb[1:], counts[None, :]], axis=0)
    length = nxt - cntb
    first = grp_start[None, :] + cntb
    base = (first // SUBLANES) * SUBLANES
    nq = jnp.where(length > 0, (first - base + length + COMB_CH - 1) // COMB_CH, 0)
    q_end = jnp.cumsum(nq, axis=1)
    q_start = q_end - nq
    nch = q_end[:, -1]
    c = jnp.arange(COMB_MAXCH, dtype=jnp.int32)
    e_c = jnp.clip(jnp.sum((q_end[:, None, :] <= c[None, :, None]).astype(jnp.int32), -1), 0, N_EXPERTS - 1)
    onehot = e_c[:, :, None] == jnp.arange(N_EXPERTS, dtype=jnp.int32)
    pick = lambda a: jnp.sum(jnp.where(onehot, a[:, None, :], 0), -1)
    want = pick(base) + COMB_CH * (c[None, :] - pick(q_start))
    start = jnp.minimum(want, m - COMB_CH)
    valid = c[None, :] < nch[:, None]
    cstart = jnp.where(valid, start, 0).astype(jnp.int32).reshape(-1)
    rows = start[:, :, None] + jnp.arange(COMB_CH, dtype=jnp.int32)
    lo = jnp.maximum(want, pick(first))[:, :, None]
    end = (pick(first) + pick(length))[:, :, None]
    own = valid[:, :, None] & (rows >= lo) & (rows < end)
    bufslot = jnp.where(own, rows, -1).astype(jnp.int32).reshape(n_tiles, 1, COMB_MAXCH * COMB_CH)
    return cstart, nch.astype(jnp.int32), bufslot


def _routing(idx, rank, counts, tm):
    n_tok = idx.shape[0]
    m = n_tok * TOP_K
    n_tiles = m // tm
    n_items = n_tiles + N_EXPERTS - 1
    grp_end = jnp.cumsum(counts)
    grp_start = grp_end - counts
    experts = jnp.arange(N_EXPERTS, dtype=jnp.int32)
    pos = (jnp.sum(jnp.where(idx[:, :, None] == experts, grp_start, 0), -1) + rank).reshape(-1).astype(jnp.int32)
    t_first = grp_start // tm
    t_last = (grp_end - 1) // tm
    n_e = jnp.where(counts > 0, t_last - t_first + 1, 0)
    w_end = jnp.cumsum(n_e)
    w_start = w_end - n_e
    n_work = w_end[-1]
    w = jnp.arange(n_items, dtype=jnp.int32)
    valid = w < n_work
    wq = jnp.minimum(w, n_work - 1)
    e_w = jnp.clip(jnp.sum((w_end[None, :] <= wq[:, None]).astype(jnp.int32), -1), 0, N_EXPERTS - 1)
    onehot = e_w[:, None] == experts
    pick = lambda a: jnp.sum(jnp.where(onehot, a[None, :], 0), -1)
    tile_w = pick(t_first) + (wq - pick(w_start))
    lo = jnp.clip(pick(grp_start) - tile_w * tm, 0, tm)
    hi = jnp.clip(pick(grp_end) - tile_w * tm, 0, tm)
    lo = jnp.where(valid, lo, 0)
    hi = jnp.where(valid, hi, 0)
    nonempty = counts > 0
    slot_e = (jnp.cumsum(nonempty.astype(jnp.int32)) - 1) % 2
    later = nonempty[None, :] & (experts[None, :] > experts[:, None])
    next_e = jnp.min(jnp.where(later, experts[None, :], N_EXPERTS), axis=1)
    next_e = jnp.where(next_e < N_EXPERTS, next_e, -1)
    i32 = lambda a: a.astype(jnp.int32)
    return pos, i32(grp_start), (i32(e_w), i32(tile_w), i32(lo), i32(hi), i32(n_work).reshape(1),
                                 i32(pick(slot_e)), i32(pick(next_e)))


def _layer(h, p_i, cos8, sin8, lam_init, w_in, conv_w, a_log, dt_bias, gdn_norm_w,
           lam_q1, lam_k1, lam_q2, lam_k2, diff_norm_w, w_out, ln1_g, ln1_b,
           router_w, router_b, w_gu, b_gu, w_down, b_down, ln2_g, ln2_b,
           ple_w, ple_gate_w, ple_gate_b, ln3_g, ln3_b):
    batch, seq, d = h.shape
    n = batch * seq
    x2d = h.reshape(n, d)
    o_ba = 4 * GDN_W
    o_d = o_ba + 2 * GDN_HEADS
    w_r = jnp.concatenate([w_in[:, :o_ba], w_in[:, o_d:], w_in[:, o_ba:o_d],
                           jnp.zeros((d, LANES - 2 * GDN_HEADS), w_in.dtype)], -1).astype(BF16)
    a_proj, ba_proj, dq, dk, dv = _inproj(x2d, w_r, jnp.concatenate([cos8, sin8], -1), batch, seq)
    o_gdn = _gdn(a_proj, ba_proj, conv_w, a_log, dt_bias, gdn_norm_w, batch, seq)
    o_diff = _diff_attention(dq, dk, dv, lam_q1, lam_k1, lam_q2, lam_k2, diff_norm_w, lam_init, batch, seq)
    rw_pad = jnp.concatenate([router_w, jnp.zeros((d, LANES - N_EXPERTS), F32)], -1)
    rw_hi = rw_pad.astype(BF16)
    rw_lo = (rw_pad - rw_hi.astype(F32)).astype(BF16)
    rw_pad = jnp.concatenate([rw_hi, rw_lo, rw_hi], axis=0)
    rb_pad = jnp.concatenate([router_b, jnp.full((LANES - N_EXPERTS,), -jnp.inf, F32)]).reshape(1, LANES)
    h1, idx, gates, cnt, tcnt = _outproj(o_gdn, o_diff, x2d, w_out.astype(BF16), ln1_g, ln1_b, rw_pad, rb_pad)
    counts = cnt[0, :N_EXPERTS]
    pos, grp_start, work = _routing(idx[:, :TOP_K], idx[:, TOP_K:2 * TOP_K], counts, TM_MOE)
    xs = _dispatch(pos, h1)
    ys = _moe(work, xs, w_gu, b_gu, w_down, b_down)
    per_step = TM_PROJ // TC_COMB
    cntb = tcnt.reshape(n // TM_PROJ, 8, LANES)[:, :per_step, :N_EXPERTS].reshape(n // TC_COMB, N_EXPERTS)
    chunks = _combine_chunks(cntb, counts, grp_start, n * TOP_K)
    out = _combine(chunks, pos.reshape(n, TOP_K), h1, gates, p_i.reshape(n, PLE_DIM), ys, ln2_g, ln2_b,
                   ple_gate_w.astype(BF16), ple_gate_b, ple_w.astype(BF16), ln3_g, ln3_b)
    return out.reshape(batch, seq, d)


def kernel(x, p, positions, w_in, conv_w, a_log, dt_bias, gdn_norm_w, lam_q1, lam_k1, lam_q2, lam_k2,
           diff_norm_w, w_out, ln1_g, ln1_b, router_w, router_b, w_gu, b_gu, w_down, b_down, ln2_g, ln2_b,
           ple_w, ple_gate_w, ple_gate_b, ln3_g, ln3_b):
    batch, seq, _ = x.shape
    inv_freq = ROPE_THETA ** (-jnp.arange(0, ROPE_DIM, 2, dtype=F32) / ROPE_DIM)
    ang = (positions.astype(F32)[..., None] * inv_freq).reshape(batch * seq, ROPE_DIM // 2)
    cos8 = jnp.cos(ang)
    sin8 = jnp.sin(ang)
    h = x
    for i in range(w_in.shape[0]):
        lam_init = 0.8 - 0.6 * math.exp(-0.3 * i)
        h = _layer(h, p[i], cos8, sin8, lam_init, w_in[i], conv_w[i], a_log[i], dt_bias[i], gdn_norm_w[i],
                   lam_q1[i], lam_k1[i], lam_q2[i], lam_k2[i], diff_norm_w[i], w_out[i], ln1_g[i], ln1_b[i],
                   router_w[i], router_b[i], w_gu[i], b_gu[i], w_down[i], b_down[i], ln2_g[i], ln2_b[i],
                   ple_w[i], ple_gate_w[i], ple_gate_b[i], ln3_g[i], ln3_b[i])
    return h
```

```python
import functools
import math

import jax
import jax.numpy as jnp
import numpy as np
from jax import lax
from jax.experimental import pallas as pl
from jax.experimental.pallas import tpu as pltpu

F32 = jnp.float32
BF16 = jnp.bfloat16
LOG2E = 1.4426950408889634

D_MODEL = 1024
PLE_DIM = 256
GDN_HEADS = 4
GDN_DK = 128
GDN_DV = 128
CONV_WIDTH = 4
CHUNK = 64
DIFF_HEADS = 4
DIFF_D = 64
DIFF_DV = 2 * DIFF_D
ROPE_THETA = 500000.0
ROPE_DIM = DIFF_D // 4
N_EXPERTS = 32
TOP_K = 4
D_FF = D_MODEL
SWIGLU_LIMIT = 7.0
SWIGLU_ALPHA = 1.702
DEPTH = 1
DN_ALPHA = (2 * DEPTH) ** 0.25
LN_EPS = 1e-5
RMS_EPS = 1e-6

LANES = 128
SUBLANES = 8
GDN_W = GDN_HEADS * GDN_DK
CONV_CH = 3 * GDN_W
A_COLS = 4 * GDN_W
DIFF_W = DIFF_HEADS * DIFF_DV
IN_PAD_W = A_COLS + 3 * DIFF_W + LANES
HSTACK = GDN_HEADS * CHUNK
VT_ROWS = DIFF_DV + 16
SUBBLK = 16

VMEM_LIMIT = 56 * 1024 * 1024

TM_PROJ = 512
T_GDN = 1024
GDN_GROUP = 4
TQ = 512
TKV = 512
TM_MOE = 256
TC_DISP = 512
DISP_SLOTS = 3
TC_COMB = 256
COMB_CH = 32
COMB_MAXCH = 72
assert COMB_MAXCH >= (TC_COMB * TOP_K + N_EXPERTS * (SUBLANES - 1 + COMB_CH - 1)) // COMB_CH
assert (COMB_MAXCH * COMB_CH) % LANES == 0


def _layer_norm(y, g, b):
    mu = jnp.mean(y, -1, keepdims=True)
    d = y - mu
    var = jnp.mean(d * d, -1, keepdims=True)
    return d * lax.rsqrt(var + LN_EPS) * g + b


def _sigmoid(x):
    return 1.0 / (1.0 + jnp.exp(-x))


def _inproj_kernel(x_ref, w_ref, cs_ref, e_ref, a_ref, ba_ref, q_ref, k_ref, vt_ref):
    xb = x_ref[...].astype(BF16)
    a_ref[...] = jnp.dot(xb, w_ref[:, :A_COLS], preferred_element_type=F32)
    ba_ref[...] = jnp.dot(xb, w_ref[:, A_COLS + 3 * DIFF_W:], preferred_element_type=F32)
    cs = cs_ref[...]
    p1 = cs.astype(BF16)
    r1 = cs - p1.astype(F32)
    p2 = r1.astype(BF16)
    p3 = (r1 - p2.astype(F32)).astype(BF16)
    tabs = (jnp.dot(p1, e_ref[...], preferred_element_type=F32) + jnp.dot(p2, e_ref[...], preferred_element_type=F32)
            + jnp.dot(p3, e_ref[...], preferred_element_type=F32))
    lane = lax.broadcasted_iota(jnp.int32, (1, LANES), 1)
    c = tabs[:, :LANES] + jnp.where((lane & (DIFF_D - 1)) >= ROPE_DIM, 1.0, 0.0)
    sa = tabs[:, LANES:2 * LANES]
    sb = tabs[:, 2 * LANES:]

    def rot(t):
        return t * c + pltpu.roll(t, 8, 1) * sa + pltpu.roll(t, LANES - 8, 1) * sb

    qk = jnp.dot(xb, w_ref[:, A_COLS:A_COLS + 2 * DIFF_W], preferred_element_type=F32)
    for h in range(DIFF_HEADS):
        q = qk[:, LANES * h:LANES * (h + 1)]
        q_ref[:, LANES * h:LANES * (h + 1)] = (rot(q) * (DIFF_D ** -0.5 * LOG2E)).astype(BF16)
        k = qk[:, DIFF_W + LANES * h:DIFF_W + LANES * (h + 1)]
        k_ref[:, LANES * h:LANES * (h + 1)] = rot(k).astype(BF16)
    lo = A_COLS + 2 * DIFF_W
    v = jnp.dot(xb, w_ref[:, lo:lo + DIFF_W], preferred_element_type=F32)
    ones = jnp.ones((VT_ROWS - DIFF_DV, v.shape[0]), BF16)
    for h in range(DIFF_HEADS):
        vt_ref[VT_ROWS * h:VT_ROWS * h + DIFF_DV, :] = v[:, DIFF_DV * h:DIFF_DV * (h + 1)].T.astype(BF16)
        vt_ref[VT_ROWS * h + DIFF_DV:VT_ROWS * (h + 1), :] = ones


def _rope_expand():
    half = ROPE_DIM // 2
    e = np.zeros((ROPE_DIM, 3 * LANES), np.float32)
    for lane in range(LANES):
        j = lane % DIFF_D
        if j < ROPE_DIM:
            e[j % half, lane] = 1.0
            if j >= half:
                e[half + j % half, LANES + lane] = 1.0
            else:
                e[half + j % half, 2 * LANES + lane] = -1.0
    return jnp.asarray(e, BF16)


def _inproj(x2d, w_r, cs16, batch, seq):
    n = x2d.shape[0]
    tm = TM_PROJ
    per_b = seq // tm
    row = lambda w: pl.BlockSpec((tm, w), lambda i: (i, 0))
    return pl.pallas_call(
        _inproj_kernel,
        grid=(n // tm,),
        in_specs=[row(D_MODEL),
                  pl.BlockSpec((D_MODEL, IN_PAD_W), lambda i: (0, 0)),
                  row(ROPE_DIM),
                  pl.BlockSpec((ROPE_DIM, 3 * LANES), lambda i: (0, 0))],
        out_specs=[row(A_COLS), row(LANES), row(DIFF_W), row(DIFF_W),
                   pl.BlockSpec((None, DIFF_HEADS * VT_ROWS, tm), lambda i: (i // per_b, 0, i % per_b))],
        out_shape=[jax.ShapeDtypeStruct((n, A_COLS), F32),
                   jax.ShapeDtypeStruct((n, LANES), F32),
                   jax.ShapeDtypeStruct((n, DIFF_W), BF16),
                   jax.ShapeDtypeStruct((n, DIFF_W), BF16),
                   jax.ShapeDtypeStruct((batch, DIFF_HEADS * VT_ROWS, seq), BF16)],
        compiler_params=pltpu.CompilerParams(dimension_semantics=("arbitrary",),
                                             vmem_limit_bytes=VMEM_LIMIT),
        name="inproj",
    )(x2d, w_r, cs16, _rope_expand())


def _mm(a, b):
    return jnp.dot(a.astype(BF16), b.astype(BF16), preferred_element_type=F32)


def _mm16(a, b):
    return jnp.dot(a, b, preferred_element_type=F32).astype(BF16)


def _gdn_kernel(a_ref, ba_ref, cw_ref, aux_ref, nw_ref, o_ref, xe_ref, q_s, k_s, v_s, state_ref):
    t_rows = a_ref.shape[0]
    st = pl.program_id(1)

    @pl.when(st == 0)
    def _():
        xe_ref[0:8, :] = jnp.zeros((8, CONV_CH), F32)
        state_ref[...] = jnp.zeros(state_ref.shape, F32)

    xe_ref[8:8 + t_rows, :] = a_ref[:, :CONV_CH]

    def phase1(r0, nrows):
        for s in range(CONV_CH // LANES):
            cs = slice(LANES * s, LANES * (s + 1))
            xe = xe_ref[r0:r0 + nrows + 8, cs]
            y = xe[8:] * cw_ref[CONV_WIDTH - 1:CONV_WIDTH, cs]
            for j in range(CONV_WIDTH - 1):
                y = y + pltpu.roll(xe, CONV_WIDTH - 1 - j, 0)[8:] * cw_ref[j:j + 1, cs]
            y = y * _sigmoid(y)
            grp, h = divmod(s, GDN_HEADS)
            hs = slice(LANES * h, LANES * (h + 1))
            if grp == 0:
                q_s[r0:r0 + nrows, hs] = (y * lax.rsqrt(jnp.sum(y * y, -1, keepdims=True) + 1e-6)
                                          * (GDN_DK ** -0.5))
            elif grp == 1:
                k_s[r0:r0 + nrows, hs] = y * lax.rsqrt(jnp.sum(y * y, -1, keepdims=True) + 1e-6)
            else:
                v_s[r0:r0 + nrows, hs] = y
            yield

    ba = ba_ref[...]
    beta_t = _sigmoid(ba)
    gx = ba + aux_ref[1:2, :]
    g_t = -jnp.exp(aux_ref[0:1, :]) * (jnp.maximum(gx, 0.0) + jnp.log(1.0 + jnp.exp(-jnp.abs(gx))))

    ri = lax.broadcasted_iota(jnp.int32, (HSTACK, HSTACK), 0)
    ci = lax.broadcasted_iota(jnp.int32, (HSTACK, HSTACK), 1)
    head_start = ri - (ri & (CHUNK - 1))
    in_head = ci >= head_start
    incl_f = jnp.where(in_head, jnp.where(ci <= ri, 1.0, 0.0), 0.0)
    strict_f = jnp.where(in_head, jnp.where(ci < ri, 1.0, 0.0), 0.0)
    sub_f = jnp.where(ci >= ri - (ri & (SUBBLK - 1)), 1.0, 0.0)
    eye = jnp.where(ri == ci, 1.0, 0.0)
    tri_b = (lax.broadcasted_iota(jnp.int32, (CHUNK, CHUNK), 0)
             >= lax.broadcasted_iota(jnp.int32, (CHUNK, CHUNK), 1)).astype(BF16)
    nt = (((1,), (1,)), ((), ()))
    tn = (((0,), (0,)), ((), ()))
    bdot = functools.partial(jnp.dot, preferred_element_type=F32)

    def stack(fn):
        return jnp.concatenate([fn(h) for h in range(GDN_HEADS)], axis=0)

    crow = [slice(CHUNK * c, CHUNK * (c + 1)) for c in range(t_rows // CHUNK)]

    def cumdecay(g):
        g1 = g.astype(BF16)
        r1 = g - g1.astype(F32)
        g2 = r1.astype(BF16)
        g3 = (r1 - g2.astype(F32)).astype(BF16)
        return bdot(tri_b, g1) + bdot(tri_b, g2) + bdot(tri_b, g3)

    eye16 = eye.astype(BF16)
    res = {}

    def phase2(chunks):
        each = lambda fn: {c: fn(c) for c in chunks}
        gc = each(lambda c: cumdecay(g_t[crow[c]]))
        yield
        gct = each(lambda c: gc[c].T)
        gcol = each(lambda c: stack(lambda h: jnp.broadcast_to(
            gc[c][:, GDN_HEADS + h:GDN_HEADS + h + 1], (CHUNK, LANES))))
        glast = each(lambda c: stack(lambda h: jnp.broadcast_to(
            gc[c][CHUNK - 1:CHUNK, GDN_HEADS + h:GDN_HEADS + h + 1], (CHUNK, LANES))))
        bcol = each(lambda c: stack(lambda h: jnp.broadcast_to(beta_t[crow[c], h:h + 1], (CHUNK, LANES))))
        grow = each(lambda c: jnp.concatenate(
            [gct[c][GDN_HEADS + h:GDN_HEADS + h + 1, :] for h in range(GDN_HEADS)], axis=1))
        yield
        kk = each(lambda c: stack(lambda h: k_s[crow[c], LANES * h:LANES * (h + 1)]))
        qq = each(lambda c: stack(lambda h: q_s[crow[c], LANES * h:LANES * (h + 1)]))
        vv = each(lambda c: stack(lambda h: v_s[crow[c], LANES * h:LANES * (h + 1)]))
        dec = each(lambda c: jnp.exp(jnp.minimum(jnp.concatenate([gcol[c], gcol[c]], axis=1) - grow[c], 0.0))
                   * incl_f)
        yield
        kb = each(lambda c: kk[c] * bcol[c])
        k16 = each(lambda c: kk[c].astype(BF16))
        amat = each(lambda c: lax.dot_general(kb[c].astype(BF16), k16[c], nt, preferred_element_type=F32)
                    * dec[c] * strict_f)
        yield
        f = each(lambda c: amat[c] * sub_f)
        bm = each(lambda c: f[c].astype(BF16))
        nm = each(lambda c: (amat[c] - f[c]).astype(BF16))
        b2 = each(lambda c: _mm16(bm[c], bm[c]))
        yield
        b4 = each(lambda c: _mm16(b2[c], b2[c]))
        yield
        b8 = each(lambda c: _mm16(b4[c], b4[c]))
        yield
        d = each(lambda c: _mm16(eye16 - bm[c], eye16 + b2[c]))
        yield
        d = each(lambda c: _mm16(d[c], eye16 + b4[c]))
        yield
        dinv = each(lambda c: _mm16(d[c], eye16 + b8[c]))
        yield
        mm = each(lambda c: _mm16(dinv[c], nm[c]))
        yield
        m2 = each(lambda c: _mm16(mm[c], mm[c]))
        yield
        t = each(lambda c: _mm16(eye16 - mm[c], eye16 + m2[c]))
        yield
        tinv = each(lambda c: _mm16(t[c], dinv[c]))
        eg = each(lambda c: jnp.exp(gcol[c]))
        yield
        sol = each(lambda c: _mm(tinv[c], jnp.concatenate([vv[c] * bcol[c], kb[c] * eg[c]], axis=1)))
        qk = each(lambda c: lax.dot_general(qq[c].astype(BF16), k16[c], nt, preferred_element_type=F32) * dec[c])
        yield
        for c in chunks:
            res[c] = (gc[c], sol[c], qk[c], qq[c] * eg[c], kk[c] * jnp.exp(glast[c] - gcol[c]))
        yield

    def phase3(chunks):
        for c in chunks:
            rows = crow[c]
            gc, sol, qk, qd, kd = res.pop(c)
            u = sol[:, :LANES]
            w = sol[:, LANES:]
            ws, qs = [], []
            for h in range(GDN_HEADS):
                hr = slice(CHUNK * h, CHUNK * (h + 1))
                lhs = jnp.concatenate([w[hr], qd[hr]], axis=0).astype(BF16)
                r = bdot(lhs, state_ref[h].astype(BF16))
                ws.append(r[:CHUNK])
                qs.append(r[CHUNK:])
            yield
            vn = u - jnp.concatenate(ws, axis=0)
            vn16 = vn.astype(BF16)
            o = jnp.concatenate(qs, axis=0) + bdot(qk.astype(BF16), vn16)
            kd16 = kd.astype(BF16)
            for h in range(GDN_HEADS):
                hr = slice(CHUNK * h, CHUNK * (h + 1))
                gl = jnp.exp(gc[CHUNK - 1:CHUNK, GDN_HEADS + h:GDN_HEADS + h + 1])
                state_ref[h] = state_ref[h] * gl + lax.dot_general(kd16[hr], vn16[hr], tn,
                                                                   preferred_element_type=F32)
            yield
            for h in range(GDN_HEADS):
                hr = slice(CHUNK * h, CHUNK * (h + 1))
                hs = slice(LANES * h, LANES * (h + 1))
                oh = o[hr]
                z = a_ref[rows, CONV_CH + LANES * h:CONV_CH + LANES * (h + 1)]
                oh = oh * lax.rsqrt(jnp.mean(oh * oh, -1, keepdims=True) + RMS_EPS) * nw_ref[...]
                o_ref[rows, hs] = (oh * (z * _sigmoid(z))).astype(o_ref.dtype)
            yield

    def interleave(*gens):
        live = list(gens)
        while live:
            for g in list(live):
                if next(g, StopIteration) is StopIteration:
                    live.remove(g)

    n_chunks = t_rows // CHUNK
    groups = [list(range(c, c + GDN_GROUP)) for c in range(0, n_chunks, GDN_GROUP)]
    rows_of = lambda grp: (CHUNK * grp[0], CHUNK * len(grp))
    n_grp = len(groups)
    for step in range(n_grp + 2):
        gens = []
        if step < n_grp:
            gens.append(phase1(*rows_of(groups[step])))
        if 0 <= step - 1 < n_grp:
            gens.append(phase2(groups[step - 1]))
        if 0 <= step - 2 < n_grp:
            gens.append(phase3(groups[step - 2]))
        interleave(*gens)
    xe_ref[0:8, :] = xe_ref[t_rows:t_rows + 8, :]


def _gdn(a_proj, ba_proj, conv_w, a_log, dt_bias, norm_w, batch, seq):
    t = T_GDN
    nst = seq // t
    rowblk = lambda w: pl.BlockSpec((t, w), lambda b, s: (b * nst + s, 0))
    aux = jnp.zeros((8, LANES), F32)
    aux = aux.at[0, GDN_HEADS:2 * GDN_HEADS].set(a_log).at[1, GDN_HEADS:2 * GDN_HEADS].set(dt_bias)
    return pl.pallas_call(
        _gdn_kernel,
        grid=(batch, nst),
        in_specs=[rowblk(A_COLS), rowblk(LANES),
                  pl.BlockSpec((CONV_WIDTH, CONV_CH), lambda b, s: (0, 0)),
                  pl.BlockSpec((8, LANES), lambda b, s: (0, 0)),
                  pl.BlockSpec((1, GDN_DV), lambda b, s: (0, 0))],
        out_specs=rowblk(GDN_W),
        out_shape=jax.ShapeDtypeStruct((batch * seq, GDN_W), BF16),
        scratch_shapes=[pltpu.VMEM((t + 8, CONV_CH), F32),
                        pltpu.VMEM((t, GDN_W), F32),
                        pltpu.VMEM((t, GDN_W), F32),
                        pltpu.VMEM((t, GDN_W), F32),
                        pltpu.VMEM((GDN_HEADS, GDN_DK, GDN_DV), F32)],
        compiler_params=pltpu.CompilerParams(dimension_semantics=("arbitrary", "arbitrary"),
                                             vmem_limit_bytes=VMEM_LIMIT),
        name="gdn",
    )(a_proj, ba_proj, conv_w, aux, norm_w.reshape(1, GDN_DV))


def _attn_kernel(q_ref, k_ref, vt_ref, lq1_ref, lk1_ref, lq2_ref, lk2_ref, nw_ref, o_ref, *, lam_init):
    tq = q_ref.shape[0]
    qi = pl.program_id(1)
    lane = lax.broadcasted_iota(jnp.int32, (1, LANES), 1)
    qpos = qi * tq + (lax.broadcasted_iota(jnp.int32, (1, 2 * tq), 1) & (tq - 1))
    nt = (((1,), (1,)), ((), ()))
    q2 = []
    for h in range(DIFF_HEADS):
        q = q_ref[:, LANES * h:LANES * (h + 1)]
        zero = jnp.zeros_like(q)
        q2.append(jnp.concatenate([jnp.where(lane < DIFF_D, q, zero), jnp.where(lane >= DIFF_D, q, zero)], axis=0))

    heads = range(DIFF_HEADS)

    def update(off, nk, state, masked):
        off = pl.multiple_of(off, TKV)
        ss = [lax.dot_general(k_ref[pl.ds(off, nk), LANES * h:LANES * (h + 1)], q2[h], nt,
                              preferred_element_type=F32) for h in heads]
        if masked:
            keep = off + lax.broadcasted_iota(jnp.int32, (nk, 1), 0) <= qpos
            ss = [jnp.where(keep, s, -1e30) for s in ss]
        mns = [jnp.maximum(state[h][0], jnp.max(ss[h], 0, keepdims=True)) for h in heads]
        ps = [jnp.exp2((ss[h] - mns[h]).astype(BF16)) for h in heads]
        als = [jnp.exp2(state[h][0] - mns[h]) for h in heads]
        pvs = [jnp.dot(vt_ref[VT_ROWS * h:VT_ROWS * (h + 1), pl.ds(off, nk)], ps[h],
                       preferred_element_type=F32) for h in heads]
        return tuple((mns[h], als[h] * state[h][1] + pvs[h]) for h in heads)

    init = tuple((jnp.full((1, 2 * tq), -1e30, F32), jnp.zeros((VT_ROWS, 2 * tq), F32)) for _ in heads)
    state = lax.fori_loop(0, qi, lambda j, st: update(j * TKV, TKV, st, False), init)
    carry = update(qi * TKV, TKV, state, True)
    lam = (jnp.exp(jnp.sum(lq1_ref[...] * lk1_ref[...], -1, keepdims=True))
           - jnp.exp(jnp.sum(lq2_ref[...] * lk2_ref[...], -1, keepdims=True)) + lam_init)
    on = [carry[h][1][:DIFF_DV] * (1.0 / carry[h][1][DIFF_DV:DIFF_DV + 1]) for h in heads]
    ot = [(on[h][:, :tq] - lam * on[h][:, tq:]).T for h in heads]
    inv = [lax.rsqrt(jnp.mean(ot[h] * ot[h], -1, keepdims=True) + RMS_EPS) for h in heads]
    for h in heads:
        o_ref[:, LANES * h:LANES * (h + 1)] = (ot[h] * inv[h] * nw_ref[...] * (1.0 - lam_init)).astype(o_ref.dtype)


def _diff_attention(q, k, v3, lq1, lk1, lq2, lk2, norm_w, lam_init, batch, seq):
    q3 = q.reshape(batch, seq, DIFF_W)
    k3 = k.reshape(batch, seq, DIFF_W)
    small = lambda w: pl.BlockSpec((1, w), lambda b, i: (0, 0))
    out = pl.pallas_call(
        functools.partial(_attn_kernel, lam_init=lam_init),
        grid=(batch, seq // TQ),
        in_specs=[pl.BlockSpec((None, TQ, DIFF_W), lambda b, i: (b, i, 0)),
                  pl.BlockSpec((None, seq, DIFF_W), lambda b, i: (b, 0, 0)),
                  pl.BlockSpec((None, DIFF_HEADS * VT_ROWS, seq), lambda b, i: (b, 0, 0)),
                  small(DIFF_D), small(DIFF_D), small(DIFF_D), small(DIFF_D), small(DIFF_DV)],
        out_specs=pl.BlockSpec((None, TQ, DIFF_W), lambda b, i: (b, i, 0)),
        out_shape=jax.ShapeDtypeStruct((batch, seq, DIFF_W), BF16),
        compiler_params=pltpu.CompilerParams(
            dimension_semantics=("arbitrary", "arbitrary"),
            vmem_limit_bytes=VMEM_LIMIT),
        name="diff_attn",
    )(q3, k3, v3, lq1.reshape(1, -1), lk1.reshape(1, -1), lq2.reshape(1, -1), lk2.reshape(1, -1),
      norm_w.reshape(1, -1))
    return out.reshape(batch * seq, DIFF_W)


def _outproj_kernel(og_ref, od_ref, x_ref, wo_ref, g_ref, b_ref, rw_ref, rb_ref,
                    h_ref, idx_ref, gate_ref, cnt_out_ref, tcnt_ref, cnt_ref):
    @pl.when(pl.program_id(0) == 0)
    def _():
        cnt_ref[...] = jnp.zeros(cnt_ref.shape, F32)

    tm = x_ref.shape[0]
    tp = TC_COMB
    parts = range(tm // tp)
    rp = [slice(tp * p, tp * (p + 1)) for p in parts]
    mix = [jnp.dot(og_ref[rp[p], :], wo_ref[:GDN_W, :], preferred_element_type=F32)
           + jnp.dot(od_ref[rp[p], :], wo_ref[GDN_W:, :], preferred_element_type=F32) for p in parts]
    hs = [_layer_norm(DN_ALPHA * x_ref[rp[p], :] + mix[p], g_ref[...], b_ref[...]) for p in parts]
    for p in parts:
        h_ref[rp[p], :] = hs[p]
    h_hi = [hs[p].astype(BF16) for p in parts]
    h_lo = [(hs[p] - h_hi[p].astype(F32)).astype(BF16) for p in parts]
    work = [jnp.dot(jnp.concatenate([h_hi[p], h_hi[p], h_lo[p]], axis=1), rw_ref[...],
                    preferred_element_type=F32) + rb_ref[...] for p in parts]
    lane = lax.broadcasted_iota(jnp.int32, (tp, LANES), 1)
    lane_f = lane.astype(F32)
    vals, idxs = [], []
    for _ in range(TOP_K):
        m = [jnp.max(work[p], -1, keepdims=True) for p in parts]
        sel = [jnp.min(jnp.where(work[p] == m[p], lane_f, float(LANES)), -1, keepdims=True) for p in parts]
        vals.append(m)
        idxs.append(sel)
        work = [jnp.where(lane_f == sel[p], -jnp.inf, work[p]) for p in parts]
    exps = [[jnp.exp(vals[j][p] - vals[0][p]) for p in parts] for j in range(TOP_K)]
    inv = [1.0 / (exps[0][p] + exps[1][p] + exps[2][p] + exps[3][p]) for p in parts]

    hot = [jnp.zeros((tp, LANES), F32) for p in parts]
    for j in range(TOP_K):
        hot = [hot[p] + jnp.where(lane_f == idxs[j][p], 1.0, 0.0) for p in parts]
    ri = lax.broadcasted_iota(jnp.int32, (tp, tp), 0)
    ci = lax.broadcasted_iota(jnp.int32, (tp, tp), 1)
    before = jnp.where(ci < ri, 1.0, 0.0).astype(BF16)
    within = [jnp.dot(before, hot[p].astype(BF16), preferred_element_type=F32) for p in parts]
    base = [cnt_ref[...]]
    for p in parts:
        base.append(base[p] + jnp.sum(hot[p], 0, keepdims=True))
    cnt_ref[...] = base[-1]
    cnt_out_ref[...] = base[-1].astype(jnp.int32)
    pad = jnp.zeros((tcnt_ref.shape[0] - len(parts), LANES), F32)
    tcnt_ref[...] = jnp.concatenate(base[:-1] + [pad], axis=0).astype(jnp.int32)

    for p in parts:
        prefix = within[p] + base[p]
        idx_out = jnp.zeros((tp, LANES), F32)
        gate_out = jnp.zeros((tp, LANES), F32)
        for j in range(TOP_K):
            rank = jnp.sum(jnp.where(lane_f == idxs[j][p], prefix, 0.0), -1, keepdims=True)
            idx_out = jnp.where(lane == j, idxs[j][p], idx_out)
            idx_out = jnp.where(lane == TOP_K + j, rank, idx_out)
            gate_out = jnp.where(lane == j, exps[j][p] * inv[p], gate_out)
        idx_ref[rp[p], :] = idx_out.astype(jnp.int32)
        gate_ref[rp[p], :] = gate_out


def _outproj(og, od, x2d, w_out_b, ln_g, ln_b, rw_pad, rb_pad):
    n = x2d.shape[0]
    tm = TM_PROJ
    row = lambda w: pl.BlockSpec((tm, w), lambda i: (i, 0))
    full = lambda r, c: pl.BlockSpec((r, c), lambda i: (0, 0))
    return pl.pallas_call(
        _outproj_kernel,
        grid=(n // tm,),
        in_specs=[row(GDN_W), row(DIFF_W), row(D_MODEL), full(GDN_W + DIFF_W, D_MODEL),
                  full(1, D_MODEL), full(1, D_MODEL), full(3 * D_MODEL, LANES), full(1, LANES)],
        out_specs=[row(D_MODEL), row(LANES), row(LANES), full(1, LANES),
                   pl.BlockSpec((8, LANES), lambda i: (i, 0))],
        out_shape=[jax.ShapeDtypeStruct((n, D_MODEL), F32),
                   jax.ShapeDtypeStruct((n, LANES), jnp.int32),
                   jax.ShapeDtypeStruct((n, LANES), F32),
                   jax.ShapeDtypeStruct((1, LANES), jnp.int32),
                   jax.ShapeDtypeStruct((n // tm * 8, LANES), jnp.int32)],
        scratch_shapes=[pltpu.VMEM((1, LANES), F32)],
        compiler_params=pltpu.CompilerParams(dimension_semantics=("arbitrary",),
                                             vmem_limit_bytes=VMEM_LIMIT),
        name="outproj_ln1_router",
    )(og, od, x2d, w_out_b, ln_g.reshape(1, -1), ln_b.reshape(1, -1), rw_pad, rb_pad)


def _dispatch_kernel(pos_ref, h_hbm, xs_hbm, hbuf, in_sem, out_sem):
    tc = hbuf.shape[1]
    i = pl.program_id(0)
    n = pl.num_programs(0)
    slot = i % DISP_SLOTS
    nxt = (i + 1) % DISP_SLOTS

    def tile_in(t, s):
        return pltpu.make_async_copy(h_hbm.at[pl.ds(pl.multiple_of(t * tc, tc), tc), :], hbuf.at[s], in_sem.at[s])

    def wait_rows_out(s):
        for j in range(TOP_K):
            pltpu.make_async_copy(hbuf.at[s], xs_hbm.at[pl.ds(0, tc), :], out_sem.at[s]).wait()

    @pl.when(i == 0)
    def _():
        tile_in(0, 0).start()

    @pl.when(i >= DISP_SLOTS - 1)
    def _():
        wait_rows_out(nxt)

    @pl.when(i + 1 < n)
    def _():
        tile_in(i + 1, nxt).start()

    tile_in(i, slot).wait()
    base = i * (tc * TOP_K)

    def body(r, carry):
        for j in range(TOP_K):
            s = pos_ref[base + r * TOP_K + j]
            pltpu.make_async_copy(hbuf.at[slot, pl.ds(r, 1), :], xs_hbm.at[pl.ds(s, 1), :],
                                  out_sem.at[slot]).start()
        return carry

    lax.fori_loop(0, tc, body, 0, unroll=4)

    @pl.when(i == n - 1)
    def _():
        wait_rows_out((i - 1) % DISP_SLOTS)
        wait_rows_out(slot)


def _dispatch(pos, h1):
    n = h1.shape[0]
    tc = TC_DISP
    grid_spec = pltpu.PrefetchScalarGridSpec(
        num_scalar_prefetch=1,
        grid=(n // tc,),
        in_specs=[pl.BlockSpec(memory_space=pl.ANY)],
        out_specs=pl.BlockSpec(memory_space=pl.ANY),
        scratch_shapes=[pltpu.VMEM((DISP_SLOTS, tc, D_MODEL), F32),
                        pltpu.SemaphoreType.DMA((DISP_SLOTS,)),
                        pltpu.SemaphoreType.DMA((DISP_SLOTS,))],
    )
    assert n // tc >= DISP_SLOTS
    return pl.pallas_call(
        _dispatch_kernel,
        grid_spec=grid_spec,
        out_shape=jax.ShapeDtypeStruct((n * TOP_K, D_MODEL), F32),
        compiler_params=pltpu.CompilerParams(dimension_semantics=("arbitrary",),
                                             vmem_limit_bytes=VMEM_LIMIT),
        name="dispatch",
    )(pos, h1)


def _moe_kernel(we_ref, wt_ref, lo_ref, hi_ref, nw_ref, slot_ref, nxt_ref, xs_ref, wgu_hbm, bgu_ref, wd_hbm,
                bd_ref, ys_ref, wgu_f, wd_f, wgu_b, wd_b, acc_ref, sem):
    tm = ys_ref.shape[0]
    w = pl.program_id(0)

    def weights_copy(e, s):
        return (pltpu.make_async_copy(wgu_hbm.at[e], wgu_f.at[s], sem.at[s, 0]),
                pltpu.make_async_copy(wd_hbm.at[e], wd_f.at[s], sem.at[s, 1]))

    @pl.when(w == 0)
    def _():
        acc_ref[...] = jnp.zeros(acc_ref.shape, F32)
        for cp in weights_copy(we_ref[0], slot_ref[0]):
            cp.start()

    @pl.when(w < nw_ref[0])
    def _():
        prev = jnp.maximum(w - 1, 0)

        @pl.when((w == 0) | (we_ref[w] != we_ref[prev]))
        def _():
            s = slot_ref[w]
            for cp in weights_copy(we_ref[w], s):
                cp.wait()

            @pl.when(nxt_ref[w] >= 0)
            def _():
                for cp in weights_copy(nxt_ref[w], 1 - s):
                    cp.start()

            rows = 128

            def cast(j, carry):
                r0 = pl.multiple_of(j * rows, rows)
                wgu_b[pl.ds(r0, rows), :] = wgu_f[s, pl.ds(r0, rows), :].astype(BF16)
                wd_b[pl.ds(r0, rows), :] = wd_f[s, pl.ds(r0, rows), :].astype(BF16)
                return carry

            lax.fori_loop(0, D_MODEL // rows, cast, 0)

        xb = xs_ref[...].astype(BF16)
        hgu = jnp.dot(xb, wgu_b[...], preferred_element_type=F32) + bgu_ref[0]
        gate = jnp.minimum(hgu[:, :D_FF], SWIGLU_LIMIT)
        up = jnp.clip(hgu[:, D_FF:], -SWIGLU_LIMIT, SWIGLU_LIMIT)
        act = (up + 1.0) * gate * _sigmoid(SWIGLU_ALPHA * gate)
        y = jnp.dot(act.astype(BF16), wd_b[...], preferred_element_type=F32) + bd_ref[0]
        rid = lax.broadcasted_iota(jnp.int32, (tm, 1), 0)
        mine = jnp.where(rid >= lo_ref[w], jnp.where(rid < hi_ref[w], 1.0, 0.0), 0.0) > 0.5
        keep = jnp.where((w == 0) | (wt_ref[w] != wt_ref[prev]), 0.0, 1.0)
        merged = jnp.where(mine, y, acc_ref[...] * keep)
        acc_ref[...] = merged
        ys_ref[...] = merged


def _moe(work, xs, w_gu, b_gu, w_down, b_down):
    tm = TM_MOE
    m = xs.shape[0]
    n_items = m // tm + N_EXPERTS - 1
    widx = lambda f: (lambda w, we, wt, lo, hi, nw, slot, nxt: f(w, we, wt))
    grid_spec = pltpu.PrefetchScalarGridSpec(
        num_scalar_prefetch=7,
        grid=(n_items,),
        in_specs=[pl.BlockSpec((tm, D_MODEL), widx(lambda w, we, wt: (wt[w], 0))),
                  pl.BlockSpec(memory_space=pl.ANY),
                  pl.BlockSpec((1, 1, 2 * D_FF), widx(lambda w, we, wt: (we[w], 0, 0))),
                  pl.BlockSpec(memory_space=pl.ANY),
                  pl.BlockSpec((1, 1, D_MODEL), widx(lambda w, we, wt: (we[w], 0, 0)))],
        out_specs=pl.BlockSpec((tm, D_MODEL), widx(lambda w, we, wt: (wt[w], 0))),
        scratch_shapes=[pltpu.VMEM((2, D_MODEL, 2 * D_FF), F32),
                        pltpu.VMEM((2, D_FF, D_MODEL), F32),
                        pltpu.VMEM((D_MODEL, 2 * D_FF), BF16),
                        pltpu.VMEM((D_FF, D_MODEL), BF16),
                        pltpu.VMEM((tm, D_MODEL), F32),
                        pltpu.SemaphoreType.DMA((2, 2))],
    )
    return pl.pallas_call(
        _moe_kernel,
        grid_spec=grid_spec,
        out_shape=jax.ShapeDtypeStruct((m, D_MODEL), F32),
        compiler_params=pltpu.CompilerParams(dimension_semantics=("arbitrary",),
                                             vmem_limit_bytes=VMEM_LIMIT),
        name="moe_ffn",
    )(*work, xs, w_gu, b_gu.reshape(N_EXPERTS, 1, -1), w_down, b_down.reshape(N_EXPERTS, 1, -1))


def _combine_kernel(cs_ref, nch_ref, h_ref, gate_ref, pos_ref, bs_ref, p_ref, ys_hbm, g2_ref, b2_ref, wg_ref,
                    bg_ref, wp_ref, g3_ref, b3_ref, o_ref, ybuf, sem):
    i = pl.program_id(0)
    n = pl.num_programs(0)

    def issue(t, b):
        def body(c, carry):
            dst = pl.multiple_of(c * COMB_CH, COMB_CH)
            src = pl.multiple_of(cs_ref[t * COMB_MAXCH + c], SUBLANES)
            pltpu.make_async_copy(ys_hbm.at[pl.ds(src, COMB_CH), :],
                                  ybuf.at[b, pl.ds(dst, COMB_CH), :], sem.at[b]).start()
            return carry

        lax.fori_loop(0, nch_ref[t], body, 0)

    @pl.when(i == 0)
    def _():
        def zero(j, carry):
            r0 = pl.multiple_of(j * COMB_CH, COMB_CH)
            for b in range(2):
                ybuf[b, pl.ds(r0, COMB_CH), :] = jnp.zeros((COMB_CH, D_MODEL), F32)
            return carry

        lax.fori_loop(0, COMB_MAXCH, zero, 0)
        issue(0, 0)

    @pl.when(i + 1 < n)
    def _():
        issue(i + 1, (i + 1) % 2)

    buf = i % 2

    def wait(c, carry):
        pltpu.make_async_copy(ys_hbm.at[pl.ds(0, COMB_CH), :], ybuf.at[buf, pl.ds(0, COMB_CH), :],
                              sem.at[buf]).wait()
        return carry

    lax.fori_loop(0, nch_ref[i], wait, 0)
    slots = bs_ref[0]
    ybf = ybuf[buf].astype(BF16)
    tp = h_ref.shape[0] // 2
    parts = range(2)
    rp = [slice(tp * p, tp * (p + 1)) for p in parts]
    sel = [jnp.zeros((tp, COMB_MAXCH * COMB_CH), F32) for p in parts]
    for j in range(TOP_K):
        sel = [jnp.where(pos_ref[rp[p], j:j + 1] == slots, gate_ref[rp[p], j:j + 1], sel[p]) for p in parts]
    ffn = [jnp.dot(sel[p].astype(BF16), ybf, preferred_element_type=F32) for p in parts]
    h2 = [_layer_norm(DN_ALPHA * h_ref[rp[p], :] + ffn[p], g2_ref[...], b2_ref[...]) for p in parts]
    gate = [_sigmoid(jnp.dot(h2[p].astype(BF16), wg_ref[...], preferred_element_type=F32) + bg_ref[...])
            for p in parts]
    ple = [gate[p] * jnp.dot(p_ref[rp[p], :].astype(BF16), wp_ref[...], preferred_element_type=F32) for p in parts]
    for p in parts:
        o_ref[rp[p], :] = _layer_norm(DN_ALPHA * h2[p] + ple[p], g3_ref[...], b3_ref[...])


def _combine(chunks, pos2d, h1, gates, p2d, ys, g2, b2, wg_b, bg, wp_b, g3, b3):
    n = h1.shape[0]
    tc = TC_COMB
    cstart, nch, bufslot = chunks
    row = lambda w: pl.BlockSpec((tc, w), lambda i, cs, nc: (i, 0))
    full = lambda r, c: pl.BlockSpec((r, c), lambda i, cs, nc: (0, 0))
    grid_spec = pltpu.PrefetchScalarGridSpec(
        num_scalar_prefetch=2,
        grid=(n // tc,),
        in_specs=[row(D_MODEL), row(LANES), row(TOP_K),
                  pl.BlockSpec((None, 1, COMB_MAXCH * COMB_CH), lambda i, cs, nc: (i, 0, 0)),
                  row(PLE_DIM), pl.BlockSpec(memory_space=pl.ANY),
                  full(1, D_MODEL), full(1, D_MODEL), full(D_MODEL, D_MODEL), full(1, D_MODEL),
                  full(PLE_DIM, D_MODEL), full(1, D_MODEL), full(1, D_MODEL)],
        out_specs=row(D_MODEL),
        scratch_shapes=[pltpu.VMEM((2, COMB_MAXCH * COMB_CH, D_MODEL), F32),
                        pltpu.SemaphoreType.DMA((2,))],
    )
    return pl.pallas_call(
        _combine_kernel,
        grid_spec=grid_spec,
        out_shape=jax.ShapeDtypeStruct((n, D_MODEL), F32),
        compiler_params=pltpu.CompilerParams(dimension_semantics=("arbitrary",),
                                             vmem_limit_bytes=VMEM_LIMIT),
        name="combine_ln2_ple_ln3",
    )(cstart, nch, h1, gates, pos2d, bufslot, p2d, ys, g2.reshape(1, -1), b2.reshape(1, -1), wg_b,
      bg.reshape(1, -1), wp_b, g3.reshape(1, -1), b3.reshape(1, -1))


def _combine_chunks(cntb, counts, grp_start, m):
    n_tiles = cntb.shape[0]
    nxt = jnp.concatenate([cntb[1:], counts[None, :]], axis=0)
    length = nxt - cntb
    first = grp_start[None, :] + cntb
    base = (first // SUBLANES) * SUBLANES
    nq = jnp.where(length > 0, (first - base + length + COMB_CH - 1) // COMB_CH, 0)
    q_end = jnp.cumsum(nq, axis=1)
    q_start = q_end - nq
    nch = q_end[:, -1]
    c = jnp.arange(COMB_MAXCH, dtype=jnp.int32)
    e_c = jnp.clip(jnp.sum((q_end[:, None, :] <= c[None, :, None]).astype(jnp.int32), -1), 0, N_EXPERTS - 1)
    onehot = e_c[:, :, None] == jnp.arange(N_EXPERTS, dtype=jnp.int32)
    pick = lambda a: jnp.sum(jnp.where(onehot, a[:, None, :], 0), -1)
    want = pick(base) + COMB_CH * (c[None, :] - pick(q_start))
    start = jnp.minimum(want, m - COMB_CH)
    valid = c[None, :] < nch[:, None]
    cstart = jnp.where(valid, start, 0).astype(jnp.int32).reshape(-1)
    rows = start[:, :, None] + jnp.arange(COMB_CH, dtype=jnp.int32)
    lo = jnp.maximum(want, pick(first))[:, :, None]
    end = (pick(first) + pick(length))[:, :, None]
    own = valid[:, :, None] & (rows >= lo) & (rows < end)
    bufslot = jnp.where(own, rows, -1).astype(jnp.int32).reshape(n_tiles, 1, COMB_MAXCH * COMB_CH)
    return cstart, nch.astype(jnp.int32), bufslot


def _routing(idx, rank, counts, tm):
    n_tok = idx.shape[0]
    m = n_tok * TOP_K
    n_tiles = m // tm
    n_items = n_tiles + N_EXPERTS - 1
    grp_end = jnp.cumsum(counts)
    grp_start = grp_end - counts
    experts = jnp.arange(N_EXPERTS, dtype=jnp.int32)
    pos = (jnp.sum(jnp.where(idx[:, :, None] == experts, grp_start, 0), -1) + rank).reshape(-1).astype(jnp.int32)
    t_first = grp_start // tm
    t_last = (grp_end - 1) // tm
    n_e = jnp.where(counts > 0, t_last - t_first + 1, 0)
    w_end = jnp.cumsum(n_e)
    w_start = w_end - n_e
    n_work = w_end[-1]
    w = jnp.arange(n_items, dtype=jnp.int32)
    valid = w < n_work
    wq = jnp.minimum(w, n_work - 1)
    e_w = jnp.clip(jnp.sum((w_end[None, :] <= wq[:, None]).astype(jnp.int32), -1), 0, N_EXPERTS - 1)
    onehot = e_w[:, None] == experts
    pick = lambda a: jnp.sum(jnp.where(onehot, a[None, :], 0), -1)
    tile_w = pick(t_first) + (wq - pick(w_start))
    lo = jnp.clip(pick(grp_start) - tile_w * tm, 0, tm)
    hi = jnp.clip(pick(grp_end) - tile_w * tm, 0, tm)
    lo = jnp.where(valid, lo, 0)
    hi = jnp.where(valid, hi, 0)
    nonempty = counts > 0
    slot_e = (jnp.cumsum(nonempty.astype(jnp.int32)) - 1) % 2
    later = nonempty[None, :] & (experts[None, :] > experts[:, None])
    next_e = jnp.min(jnp.where(later, experts[None, :], N_EXPERTS), axis=1)
    next_e = jnp.where(next_e < N_EXPERTS, next_e, -1)
    i32 = lambda a: a.astype(jnp.int32)
    return pos, i32(grp_start), (i32(e_w), i32(tile_w), i32(lo), i32(hi), i32(n_work).reshape(1),
                                 i32(pick(slot_e)), i32(pick(next_e)))


def _layer(h, p_i, cos8, sin8, lam_init, w_in, conv_w, a_log, dt_bias, gdn_norm_w,
           lam_q1, lam_k1, lam_q2, lam_k2, diff_norm_w, w_out, ln1_g, ln1_b,
           router_w, router_b, w_gu, b_gu, w_down, b_down, ln2_g, ln2_b,
           ple_w, ple_gate_w, ple_gate_b, ln3_g, ln3_b):
    batch, seq, d = h.shape
    n = batch * seq
    x2d = h.reshape(n, d)
    o_ba = 4 * GDN_W
    o_d = o_ba + 2 * GDN_HEADS
    w_r = jnp.concatenate([w_in[:, :o_ba], w_in[:, o_d:], w_in[:, o_ba:o_d],
                           jnp.zeros((d, LANES - 2 * GDN_HEADS), w_in.dtype)], -1).astype(BF16)
    a_proj, ba_proj, dq, dk, dv = _inproj(x2d, w_r, jnp.concatenate([cos8, sin8], -1), batch, seq)
    o_gdn = _gdn(a_proj, ba_proj, conv_w, a_log, dt_bias, gdn_norm_w, batch, seq)
    o_diff = _diff_attention(dq, dk, dv, lam_q1, lam_k1, lam_q2, lam_k2, diff_norm_w, lam_init, batch, seq)
    rw_pad = jnp.concatenate([router_w, jnp.zeros((d, LANES - N_EXPERTS), F32)], -1)
    rw_hi = rw_pad.astype(BF16)
    rw_lo = (rw_pad - rw_hi.astype(F32)).astype(BF16)
    rw_pad = jnp.concatenate([rw_hi, rw_lo, rw_hi], axis=0)
    rb_pad = jnp.concatenate([router_b, jnp.full((LANES - N_EXPERTS,), -jnp.inf, F32)]).reshape(1, LANES)
    h1, idx, gates, cnt, tcnt = _outproj(o_gdn, o_diff, x2d, w_out.astype(BF16), ln1_g, ln1_b, rw_pad, rb_pad)
    counts = cnt[0, :N_EXPERTS]
    pos, grp_start, work = _routing(idx[:, :TOP_K], idx[:, TOP_K:2 * TOP_K], counts, TM_MOE)
    xs = _dispatch(pos, h1)
    ys = _moe(work, xs, w_gu, b_gu, w_down, b_down)
    per_step = TM_PROJ // TC_COMB
    cntb = tcnt.reshape(n // TM_PROJ, 8, LANES)[:, :per_step, :N_EXPERTS].reshape(n // TC_COMB, N_EXPERTS)
    chunks = _combine_chunks(cntb, counts, grp_start, n * TOP_K)
    out = _combine(chunks, pos.reshape(n, TOP_K), h1, gates, p_i.reshape(n, PLE_DIM), ys, ln2_g, ln2_b,
                   ple_gate_w.astype(BF16), ple_gate_b, ple_w.astype(BF16), ln3_g, ln3_b)
    return out.reshape(batch, seq, d)


def kernel(x, p, positions, w_in, conv_w, a_log, dt_bias, gdn_norm_w, lam_q1, lam_k1, lam_q2, lam_k2,
           diff_norm_w, w_out, ln1_g, ln1_b, router_w, router_b, w_gu, b_gu, w_down, b_down, ln2_g, ln2_b,
           ple_w, ple_gate_w, ple_gate_b, ln3_g, ln3_b):
    batch, seq, _ = x.shape
    inv_freq = ROPE_THETA ** (-jnp.arange(0, ROPE_DIM, 2, dtype=F32) / ROPE_DIM)
    ang = (positions.astype(F32)[..., None] * inv_freq).reshape(batch * seq, ROPE_DIM // 2)
    cos8 = jnp.cos(ang)
    sin8 = jnp.sin(ang)
    h = x
    for i in range(w_in.shape[0]):
        lam_init = 0.8 - 0.6 * math.exp(-0.3 * i)
        h = _layer(h, p[i], cos8, sin8, lam_init, w_in[i], conv_w[i], a_log[i], dt_bias[i], gdn_norm_w[i],
                   lam_q1[i], lam_k1[i], lam_q2[i], lam_k2[i], diff_norm_w[i], w_out[i], ln1_g[i], ln1_b[i],
                   router_w[i], router_b[i], w_gu[i], b_gu[i], w_down[i], b_down[i], ln2_g[i], ln2_b[i],
                   ple_w[i], ple_gate_w[i], ple_gate_b[i], ln3_g[i], ln3_b[i])
    return h
```
